```python
import math
import jax, jax.numpy as jnp
from jax import lax
import numpy as np

D_MODEL = 1024
BATCH = 4
SEQ = 4096
DEPTH = 2
DEC_BATCH = 128
DEC_SEQ = 8
PAST_LEN = 8192
PAGE_SIZE = 128

N_EVEN = (DEPTH + 1) // 2
N_ODD = DEPTH // 2
HEAD_DIM = 64
NSA_HEADS = 8
NSA_KV_HEADS = 2
NSA_GROUP = NSA_HEADS // NSA_KV_HEADS
CMP_BLOCK = 64
SEL_BLOCK = CMP_BLOCK
CMP_HIDDEN = 64
SEL_TOPK = 16
NSA_WINDOW = 512
SEL_FORCE = 1.0e4
SSM_HEADS = 8
SSM_HEAD_DIM = 64
SSM_D_INNER = SSM_HEADS * SSM_HEAD_DIM
SSM_GROUPS = 2
SSM_STATE = 128
SSM_CONV = 4
SSM_CHUNK = 128
SSM_CONV_DIM = SSM_D_INNER + 2 * SSM_GROUPS * SSM_STATE
SWA_HEADS = 16
SWA_KV_HEADS = 2
SWA_GROUP = SWA_HEADS // SWA_KV_HEADS
SWA_WINDOW = 128
FFN_HIDDEN = -(-8 * D_MODEL // (3 * 256)) * 256
Q_BLK = 128
RMS_EPS = 1e-6
NSA_WIDTH = NSA_HEADS * HEAD_DIM
NSA_KV_WIDTH = 2 * NSA_KV_HEADS * HEAD_DIM
EVEN_SPLITS = (NSA_WIDTH, NSA_KV_WIDTH, NSA_KV_WIDTH, NSA_KV_WIDTH, 3 * NSA_HEADS, SSM_D_INNER, SSM_CONV_DIM, SSM_HEADS)
EVEN_PROJ = sum(EVEN_SPLITS)
EVEN_MIX = NSA_WIDTH + SSM_D_INNER
SWA_WIDTH = SWA_HEADS * HEAD_DIM
SWA_KV_WIDTH = 2 * SWA_KV_HEADS * HEAD_DIM
ODD_PROJ = SWA_WIDTH + SWA_KV_WIDTH
F32 = jnp.float32

kernel_name = "hybrid_nsa_mamba2_swa_decoder_step"


def rmsnorm(x, g):
    xf = x.astype(F32)
    y = xf * lax.rsqrt(jnp.mean(xf * xf, axis=-1, keepdims=True) + RMS_EPS)
    return (y * g.astype(F32)).astype(x.dtype)


def split_cols(x, sizes):
    return jnp.split(x, np.cumsum(sizes)[:-1].tolist(), axis=-1)


def alibi_slopes(n_heads, n_groups):
    s = 2.0 ** (-8.0 * np.arange(1, n_heads + 1) / n_heads)
    return jnp.asarray(s, F32).reshape(n_groups, n_heads // n_groups)


def masked_softmax(s, mask, sink=None):
    s = jnp.where(mask, s.astype(F32), -jnp.inf)
    m = jnp.max(s, axis=-1, keepdims=True)
    if sink is not None:
        m = jnp.maximum(m, sink)
    m = jnp.where(jnp.isfinite(m), m, 0.0)
    e = jnp.exp(s - m)
    den = jnp.sum(e, axis=-1, keepdims=True)
    if sink is not None:
        den = den + jnp.exp(sink - m)
    return e / jnp.maximum(den, 1e-30)


def gather_pages(pool, page_table):
    b, n_pages = page_table.shape
    rows = pool[page_table]
    return rows.reshape((b, n_pages * pool.shape[1]) + pool.shape[2:])


def banded_attention(q, kv_all, pos0, window, slopes, sinks=None):
    b, L, G, R, D = q.shape
    P = kv_all.shape[1] - L
    qb = math.gcd(L, Q_BLK)
    nc = L // qb
    kl = P + qb
    rows = jnp.arange(nc)[:, None] * qb + jnp.arange(kl)[None, :]
    kv = kv_all[:, rows]
    qc = q.reshape(b, nc, qb, G, R, D)
    s = jnp.einsum('bcqgrd,bckgd->bcqgrk', qc, kv[:, :, :, 0]).astype(F32) * (D ** -0.5)
    dist = P + jnp.arange(qb)[:, None] - jnp.arange(kl)[None, :]
    s = s - slopes[None, None, None, :, :, None] * dist[None, None, :, None, None, :].astype(F32)
    mask = ((dist >= 0) & (dist <= window))[None, None, :, None, None, :] & (pos0 - P + rows >= 0)[None, :, None, None, None, :]
    sk = None if sinks is None else sinks.astype(F32)[None, None, None, :, :, None]
    p = masked_softmax(s, mask, sk)
    o = jnp.einsum('bcqgrk,bckgd->bcqgrd', p.astype(q.dtype), kv[:, :, :, 1])
    return o.reshape(b, L, G, R, D)


def compress_blocks(rows, pe, w1, w2):
    h = jnp.einsum('bnlgd,ldh->bngh', rows + pe[None, None, :, None, :], w1)
    return jnp.einsum('bngh,hd->bngd', jax.nn.gelu(h), w2)


def select_attention(qg, idx, t_pos, kv_blocks, slopes):
    b, L, G, R, D = qg.shape
    K = idx.shape[-1]
    qb = math.gcd(L, Q_BLK)
    nc = L // qb
    bi = jnp.arange(b)[:, None, None, None]
    gi = jnp.arange(G)[None, None, :, None]
    offs = jnp.arange(SEL_BLOCK)

    def attend_block(args):
        qc, ic, tc = args
        kv = kv_blocks[bi, ic, :, :, gi].reshape(b, qb, G, K * SEL_BLOCK, 2, D)
        s_pos = (ic[..., None] * SEL_BLOCK + offs).reshape(b, qb, G, K * SEL_BLOCK)
        dist = (tc[None, :, None, None] - s_pos)[:, :, :, None, :]
        s = jnp.einsum('bqgrd,bqgkd->bqgrk', qc, kv[..., 0, :]).astype(F32) * (D ** -0.5)
        s = s - slopes[None, None, :, :, None] * dist.astype(F32)
        p = masked_softmax(s, dist >= 0)
        return jnp.einsum('bqgrk,bqgkd->bqgrd', p.astype(qc.dtype), kv[..., 1, :])

    xs = (jnp.moveaxis(qg.reshape(b, nc, qb, G, R, D), 1, 0),
          jnp.moveaxis(idx.reshape(b, nc, qb, G, K), 1, 0),
          t_pos.reshape(nc, qb))
    out = lax.map(attend_block, xs)
    return jnp.moveaxis(out, 0, 1).reshape(b, L, G, R, D)


def nsa_mixer(q, gates, kv_cmp_all, kv_slc_all, kv_win_all, pos0, pe_k, w1_k, w2_k, pe_v, w1_v, w2_v):
    b, L = q.shape[:2]
    T = kv_cmp_all.shape[1]
    G, R, D = NSA_KV_HEADS, NSA_GROUP, HEAD_DIM
    slopes = alibi_slopes(NSA_HEADS, NSA_KV_HEADS)
    qg = q.reshape(b, L, G, R, D)
    t_pos = pos0 + jnp.arange(L)
    nb = T // CMP_BLOCK
    blk = kv_cmp_all[:, :nb * CMP_BLOCK].reshape(b, nb, CMP_BLOCK, 2, G, D)
    k_c = compress_blocks(blk[:, :, :, 0], pe_k, w1_k, w2_k)
    v_c = compress_blocks(blk[:, :, :, 1], pe_v, w1_v, w2_v)
    dist_c = t_pos[:, None] - ((jnp.arange(nb) + 1) * CMP_BLOCK - 1)[None, :]
    s_c = jnp.einsum('blgrd,bngd->blgrn', qg, k_c).astype(F32) * (D ** -0.5)
    s_c = s_c - slopes[None, None, :, :, None] * dist_c[None, :, None, None, :].astype(F32)
    p_c = masked_softmax(s_c, (dist_c >= 0)[None, :, None, None, :])
    o_cmp = jnp.einsum('blgrn,bngd->blgrd', p_c.astype(q.dtype), v_c)
    nsel = -(-T // SEL_BLOCK)
    imp = jnp.pad(jnp.sum(p_c, axis=3), ((0, 0), (0, 0), (0, 0), (0, nsel - nb)))
    rel = jnp.arange(nsel)[None, :] - (t_pos // SEL_BLOCK)[:, None]
    score = jnp.where(rel[None, :, None, :] < 0, imp,
                      jnp.where(rel == 0, SEL_FORCE, -SEL_FORCE)[None, :, None, :])
    _, idx = lax.top_k(score, min(SEL_TOPK, nsel))
    kv_pad = jnp.pad(kv_slc_all, ((0, 0), (0, nsel * SEL_BLOCK - T), (0, 0), (0, 0), (0, 0)))
    kv_blocks = kv_pad.reshape(b, nsel, SEL_BLOCK, 2, G, D)
    o_slc = select_attention(qg, idx, t_pos, kv_blocks, slopes)
    o_win = banded_attention(qg, kv_win_all, pos0, NSA_WINDOW, slopes)
    g = gates.reshape(b, L, G, R, 3)
    o = g[..., 0:1] * o_cmp + g[..., 1:2] * o_slc + g[..., 2:3] * o_win
    return o.reshape(b, L, NSA_WIDTH)


def ssd_scan(x, dt, a, bm, cm, s0):
    b, L, H, P = x.shape
    q = math.gcd(L, SSM_CHUNK)
    nc = L // q
    hpg = SSM_HEADS // SSM_GROUPS
    xdt = (x.astype(F32) * dt[..., None]).reshape(b, nc, q, H, P)
    bh = jnp.repeat(bm.astype(F32), hpg, axis=2).reshape(b, nc, q, H, SSM_STATE)
    ch = jnp.repeat(cm.astype(F32), hpg, axis=2).reshape(b, nc, q, H, SSM_STATE)
    acum = jnp.cumsum((dt * a).reshape(b, nc, q, H), axis=2)
    causal = jnp.tril(jnp.ones((q, q), bool))[None, None, :, :, None]
    seg = acum[:, :, :, None, :] - acum[:, :, None, :, :]
    lmat = jnp.where(causal, jnp.exp(jnp.where(causal, seg, 0.0)), 0.0)
    cb = jnp.einsum('bcihn,bcjhn->bcijh', ch, bh)
    y_diag = jnp.einsum('bcijh,bcjhp->bcihp', cb * lmat, xdt)
    decay_end = jnp.exp(acum[:, :, -1:, :] - acum)
    chunk_states = jnp.einsum('bcjhn,bcjh,bcjhp->bchpn', bh, decay_end, xdt)
    chunk_decay = jnp.exp(acum[:, :, -1, :])

    def step(s, inp):
        st, dec = inp
        return s * dec[:, :, None, None] + st, s

    s_final, starts = lax.scan(step, s0.astype(F32), (jnp.moveaxis(chunk_states, 1, 0), jnp.moveaxis(chunk_decay, 1, 0)))
    starts = jnp.moveaxis(starts, 0, 1)
    y_off = jnp.einsum('bcihn,bchpn,bcih->bcihp', ch, starts, jnp.exp(acum))
    y = (y_diag + y_off).reshape(b, L, H, P)
    return y.astype(x.dtype), s_final.astype(s0.dtype)


def mamba_mixer(z, xbc, dt_raw, conv0, ssm0, conv_w, conv_b, dt_bias, a_log, d_skip, norm_g):
    b, L = z.shape[:2]
    xpad = jnp.concatenate([conv0.astype(xbc.dtype), xbc], axis=1)
    acc = conv_b
    for k in range(SSM_CONV):
        acc = acc + xpad[:, k:k + L] * conv_w[k]
    new_conv = xpad[:, L:]
    xs, bm, cm = split_cols(jax.nn.silu(acc), (SSM_D_INNER, SSM_GROUPS * SSM_STATE, SSM_GROUPS * SSM_STATE))
    x = xs.reshape(b, L, SSM_HEADS, SSM_HEAD_DIM)
    dt = jax.nn.softplus(dt_raw.astype(F32) + dt_bias.astype(F32))
    a = -jnp.exp(a_log.astype(F32))
    y, s_final = ssd_scan(x, dt, a, bm.reshape(b, L, SSM_GROUPS, SSM_STATE), cm.reshape(b, L, SSM_GROUPS, SSM_STATE), ssm0)
    y = (y + d_skip[None, None, :, None] * x).reshape(b, L, SSM_D_INNER)
    y = rmsnorm(y * jax.nn.silu(z), norm_g)
    return y, s_final, new_conv


def even_layer(x, pos0, past_cmp, past_slc, win_buf, win_out_len, conv0, ssm0, p):
    (norm_g, w_in, w_out, pe_k, w1_k, w2_k, pe_v, w1_v, w2_v,
     conv_w, conv_b, dt_bias, a_log, d_skip, ssm_norm) = p
    b, L = x.shape[:2]
    h = rmsnorm(x, norm_g)
    q, kvc, kvs, kvw, gt, z, xbc, dtr = split_cols(h @ w_in, EVEN_SPLITS)
    kv_shape = (b, L, 2, NSA_KV_HEADS, HEAD_DIM)
    kvc, kvs, kvw = kvc.reshape(kv_shape), kvs.reshape(kv_shape), kvw.reshape(kv_shape)
    kv_win_all = jnp.concatenate([win_buf.astype(x.dtype), kvw], axis=1)
    o_nsa = nsa_mixer(q.reshape(b, L, NSA_HEADS, HEAD_DIM), jax.nn.sigmoid(gt).reshape(b, L, NSA_HEADS, 3),
                      jnp.concatenate([past_cmp.astype(x.dtype), kvc], axis=1),
                      jnp.concatenate([past_slc.astype(x.dtype), kvs], axis=1),
                      kv_win_all, pos0, pe_k, w1_k, w2_k, pe_v, w1_v, w2_v)
    o_ssm, ssm_new, conv_new = mamba_mixer(z, xbc, dtr, conv0, ssm0, conv_w, conv_b, dt_bias, a_log, d_skip, ssm_norm)
    x = x + jnp.concatenate([o_nsa, o_ssm], axis=-1) @ w_out
    return x, (kvc, kvs, kv_win_all[:, -win_out_len:], ssm_new, conv_new)


def odd_layer(x, pos0, swa_buf, out_len, p):
    norm_g, w_in, w_out, sinks = p
    b, L = x.shape[:2]
    h = rmsnorm(x, norm_g)
    q, kv = split_cols(h @ w_in, (SWA_WIDTH, SWA_KV_WIDTH))
    kv_all = jnp.concatenate([swa_buf.astype(x.dtype), kv.reshape(b, L, 2, SWA_KV_HEADS, HEAD_DIM)], axis=1)
    o = banded_attention(q.reshape(b, L, SWA_KV_HEADS, SWA_GROUP, HEAD_DIM), kv_all, pos0, SWA_WINDOW,
                         alibi_slopes(SWA_HEADS, SWA_KV_HEADS), sinks.reshape(SWA_KV_HEADS, SWA_GROUP))
    x = x + o.reshape(b, L, SWA_WIDTH) @ w_out
    return x, kv_all[:, -out_len:]


def ffn(x, norm_g, w_gu, w_down):
    h = rmsnorm(x, norm_g)
    g, u = jnp.split(h @ w_gu, 2, axis=-1)
    return x + (jax.nn.silu(g) * u) @ w_down


def setup_inputs(seed: int = 0) -> dict:
    key = jax.random.key(seed)
    ks = iter(jax.random.split(key, 48))

    def nrm(shape, scale=1.0):
        return jax.random.normal(next(ks), shape, jnp.float32) * scale

    n_pages = PAST_LEN // PAGE_SIZE
    n_pool = (DEC_BATCH * n_pages * 5) // 4
    win_nsa = min(NSA_WINDOW, PAST_LEN)
    win_swa = min(SWA_WINDOW, PAST_LEN)
    kv_nsa = (2, NSA_KV_HEADS, HEAD_DIM)
    kv_swa = (2, SWA_KV_HEADS, HEAD_DIM)
    page_table = jax.random.permutation(next(ks), n_pool)[: DEC_BATCH * n_pages].reshape(DEC_BATCH, n_pages).astype(jnp.int32)
    dt0 = jnp.exp(jax.random.uniform(next(ks), (N_EVEN, SSM_HEADS), jnp.float32, math.log(1e-3), math.log(1e-1)))
    a0 = jax.random.uniform(next(ks), (N_EVEN, SSM_HEADS), jnp.float32, 1.0, 16.0)
    return {
        "x_prompt": nrm((BATCH, SEQ, D_MODEL)),
        "x_sample": nrm((DEC_BATCH, DEC_SEQ, D_MODEL)),
        "cache_nsa_cmp_kv": nrm((N_EVEN, n_pool, PAGE_SIZE) + kv_nsa),
        "cache_nsa_slc_kv": nrm((N_EVEN, n_pool, PAGE_SIZE) + kv_nsa),
        "state_nsa_win_kv": nrm((N_EVEN, DEC_BATCH, win_nsa) + kv_nsa),
        "state_ssm": nrm((N_EVEN, DEC_BATCH, SSM_HEADS, SSM_HEAD_DIM, SSM_STATE), 0.5),
        "state_ssm_conv": nrm((N_EVEN, DEC_BATCH, SSM_CONV - 1, SSM_CONV_DIM)),
        "state_swa_kv": nrm((N_ODD, DEC_BATCH, win_swa) + kv_swa),
        "page_table": page_table,
        "norm_mix": 1.0 + nrm((DEPTH, D_MODEL), 0.02),
        "norm_ffn": 1.0 + nrm((DEPTH, D_MODEL), 0.02),
        "norm_final": 1.0 + nrm((D_MODEL,), 0.02),
        "w_in_even": nrm((N_EVEN, D_MODEL, EVEN_PROJ), D_MODEL ** -0.5),
        "w_out_even": nrm((N_EVEN, EVEN_MIX, D_MODEL), EVEN_MIX ** -0.5),
        "cmp_pe_k": nrm((N_EVEN, CMP_BLOCK, HEAD_DIM), 0.1),
        "cmp_w1_k": nrm((N_EVEN, CMP_BLOCK, HEAD_DIM, CMP_HIDDEN), (CMP_BLOCK * HEAD_DIM) ** -0.5),
        "cmp_w2_k": nrm((N_EVEN, CMP_HIDDEN, HEAD_DIM), CMP_HIDDEN ** -0.5),
        "cmp_pe_v": nrm((N_EVEN, CMP_BLOCK, HEAD_DIM), 0.1),
        "cmp_w1_v": nrm((N_EVEN, CMP_BLOCK, HEAD_DIM, CMP_HIDDEN), (CMP_BLOCK * HEAD_DIM) ** -0.5),
        "cmp_w2_v": nrm((N_EVEN, CMP_HIDDEN, HEAD_DIM), CMP_HIDDEN ** -0.5),
        "ssm_conv_w": nrm((N_EVEN, SSM_CONV, SSM_CONV_DIM), SSM_CONV ** -0.5),
        "ssm_conv_b": nrm((N_EVEN, SSM_CONV_DIM), 0.02),
        "ssm_dt_bias": dt0 + jnp.log(-jnp.expm1(-dt0)),
        "ssm_a_log": jnp.log(a0),
        "ssm_d": 1.0 + nrm((N_EVEN, SSM_HEADS), 0.1),
        "ssm_norm": 1.0 + nrm((N_EVEN, SSM_D_INNER), 0.02),
        "w_in_odd": nrm((N_ODD, D_MODEL, ODD_PROJ), D_MODEL ** -0.5),
        "w_out_odd": nrm((N_ODD, SWA_WIDTH, D_MODEL), SWA_WIDTH ** -0.5),
        "swa_sinks": nrm((N_ODD, SWA_HEADS), 0.5),
        "w_gate_up": nrm((DEPTH, D_MODEL, 2 * FFN_HIDDEN), D_MODEL ** -0.5),
        "w_down": nrm((DEPTH, FFN_HIDDEN, D_MODEL), FFN_HIDDEN ** -0.5),
    }


def reference(x_prompt, x_sample, cache_nsa_cmp_kv, cache_nsa_slc_kv, state_nsa_win_kv, state_ssm, state_ssm_conv,
              state_swa_kv, page_table, norm_mix, norm_ffn, norm_final, w_in_even, w_out_even,
              cmp_pe_k, cmp_w1_k, cmp_w2_k, cmp_pe_v, cmp_w1_v, cmp_w2_v,
              ssm_conv_w, ssm_conv_b, ssm_dt_bias, ssm_a_log, ssm_d, ssm_norm,
              w_in_odd, w_out_odd, swa_sinks, w_gate_up, w_down):
    bp, lp = x_prompt.shape[:2]
    bs = x_sample.shape[0]
    past_len = page_table.shape[1] * cache_nsa_cmp_kv.shape[2]
    dt = x_prompt.dtype
    kv_nsa = (2, NSA_KV_HEADS, HEAD_DIM)
    xp, xs = x_prompt, x_sample
    p_cmp, p_slc, p_win, p_ssm, p_conv, p_swa = [], [], [], [], [], []
    s_cmp, s_slc, s_win, s_ssm, s_conv, s_swa = [], [], [], [], [], []
    for layer in range(DEPTH):
        if layer % 2 == 0:
            e = layer // 2
            p = (norm_mix[layer], w_in_even[e], w_out_even[e], cmp_pe_k[e], cmp_w1_k[e], cmp_w2_k[e],
                 cmp_pe_v[e], cmp_w1_v[e], cmp_w2_v[e], ssm_conv_w[e], ssm_conv_b[e], ssm_dt_bias[e],
                 ssm_a_log[e], ssm_d[e], ssm_norm[e])
            xp, (kc, kslc, kw, sh, cv) = even_layer(
                xp, 0, jnp.zeros((bp, 0) + kv_nsa, dt), jnp.zeros((bp, 0) + kv_nsa, dt),
                jnp.zeros((bp, NSA_WINDOW) + kv_nsa, dt), min(NSA_WINDOW, lp),
                jnp.zeros((bp, SSM_CONV - 1, SSM_CONV_DIM), dt),
                jnp.zeros((bp, SSM_HEADS, SSM_HEAD_DIM, SSM_STATE), dt), p)
            p_cmp.append(kc); p_slc.append(kslc); p_win.append(kw); p_ssm.append(sh); p_conv.append(cv)
            xs, (kc, kslc, kw, sh, cv) = even_layer(
                xs, past_len, gather_pages(cache_nsa_cmp_kv[e], page_table),
                gather_pages(cache_nsa_slc_kv[e], page_table),
                state_nsa_win_kv[e], state_nsa_win_kv.shape[2], state_ssm_conv[e], state_ssm[e], p)
            s_cmp.append(kc); s_slc.append(kslc); s_win.append(kw); s_ssm.append(sh); s_conv.append(cv)
        else:
            o = layer // 2
            p = (norm_mix[layer], w_in_odd[o], w_out_odd[o], swa_sinks[o])
            xp, kw = odd_layer(xp, 0, jnp.zeros((bp, SWA_WINDOW, 2, SWA_KV_HEADS, HEAD_DIM), dt), min(SWA_WINDOW, lp), p)
            p_swa.append(kw)
            xs, kw = odd_layer(xs, past_len, state_swa_kv[o], state_swa_kv.shape[2], p)
            s_swa.append(kw)
        xp = ffn(xp, norm_ffn[layer], w_gate_up[layer], w_down[layer])
        xs = ffn(xs, norm_ffn[layer], w_gate_up[layer], w_down[layer])
    y_prompt = rmsnorm(xp, norm_final)
    y_sample = rmsnorm(xs, norm_final)
    return (y_prompt, y_sample,
            jnp.stack(p_cmp), jnp.stack(p_slc), jnp.stack(p_win), jnp.stack(p_ssm), jnp.stack(p_conv), jnp.stack(p_swa),
            jnp.stack(s_cmp), jnp.stack(s_slc), jnp.stack(s_win), jnp.stack(s_ssm), jnp.stack(s_conv), jnp.stack(s_swa))
```

```python
import functools
import math

import numpy as np
import jax
import jax.numpy as jnp
from jax import lax
from jax.experimental import pallas as pl
from jax.experimental.pallas import tpu as pltpu

F32 = jnp.float32
BF16 = jnp.bfloat16

HEAD_DIM = 64
NSA_HEADS = 8
NSA_KV_HEADS = 2
NSA_GROUP = NSA_HEADS // NSA_KV_HEADS
CMP_BLOCK = 64
SEL_TOPK = 16
NSA_WINDOW = 512
SSM_HEADS = 8
SSM_HEAD_DIM = 64
SSM_D_INNER = SSM_HEADS * SSM_HEAD_DIM
SSM_GROUPS = 2
SSM_STATE = 128
SSM_CONV = 4
SSM_CONV_DIM = SSM_D_INNER + 2 * SSM_GROUPS * SSM_STATE
SWA_HEADS = 16
SWA_KV_HEADS = 2
SWA_GROUP = SWA_HEADS // SWA_KV_HEADS
SWA_WINDOW = 128
RMS_EPS = 1e-6
NSA_WIDTH = NSA_HEADS * HEAD_DIM
SWA_WIDTH = SWA_HEADS * HEAD_DIM
SCALE = HEAD_DIM ** -0.5

LANE = 128
SUBLANE = 8
KV_ROW = 2 * NSA_KV_HEADS * HEAD_DIM
QPAD = KV_ROW
TQ = 128
NEG = -1e30
VMEM_LIMIT = 56 * 1024 * 1024
DT_LANE = 3 * NSA_HEADS


def _cparams(*sem):
    return pltpu.CompilerParams(dimension_semantics=sem, vmem_limit_bytes=VMEM_LIMIT)


def _dot(a, b):
    return jnp.dot(a, b, preferred_element_type=F32)


def _dot_nt(a, b):
    return lax.dot_general(a, b, (((1,), (1,)), ((), ())), preferred_element_type=F32)


def _rms(x, g):
    return x * lax.rsqrt(jnp.mean(x * x, axis=-1, keepdims=True) + RMS_EPS) * g


def _sigmoid(x):
    return 1.0 / (1.0 + jnp.exp(-x))


def _alibi(n_heads):
    return [float(2.0 ** (-8.0 * i / n_heads)) for i in range(1, n_heads + 1)]


def _col(vals, rows):
    return jnp.concatenate([jnp.full((rows, 1), v, F32) for v in vals], axis=0)


def _norm_proj_kernel(x_ref, g_ref, w_ref, *o_refs, pieces):
    hb = _rms(x_ref[...], g_ref[...]).astype(BF16)
    for o_ref, (off, n) in zip(o_refs, pieces):
        o_ref[...] = _dot(hb, w_ref[:, off:off + n]).astype(o_ref.dtype)


def norm_proj(x2d, g, w_bf, pieces, dtypes, tm=256):
    m, d = x2d.shape
    n_tot = w_bf.shape[1]
    tm = min(tm, m)
    return pl.pallas_call(
        functools.partial(_norm_proj_kernel, pieces=tuple(pieces)),
        grid=(m // tm,),
        in_specs=[pl.BlockSpec((tm, d), lambda i: (i, 0)),
                  pl.BlockSpec((1, d), lambda i: (0, 0)),
                  pl.BlockSpec((d, n_tot), lambda i: (0, 0))],
        out_specs=[pl.BlockSpec((tm, n), lambda i: (i, 0)) for _, n in pieces],
        out_shape=[jax.ShapeDtypeStruct((m, n), dt) for (_, n), dt in zip(pieces, dtypes)],
        compiler_params=_cparams("parallel"),
        name="norm_proj",
    )(x2d, g.reshape(1, d), w_bf)


def _proj_res_kernel(*refs, n_in):
    a_refs, w_refs = refs[:n_in], refs[n_in:2 * n_in]
    res_ref, o_ref = refs[2 * n_in], refs[2 * n_in + 1]
    acc = res_ref[...]
    for a_ref, w_ref in zip(a_refs, w_refs):
        acc = acc + _dot(a_ref[...].astype(BF16), w_ref[...])
    o_ref[...] = acc


def proj_res(a_list, w_list, res, tm=512):
    m, d = res.shape
    tm = min(tm, m)
    n_in = len(a_list)
    in_specs = ([pl.BlockSpec((tm, a.shape[1]), lambda i: (i, 0)) for a in a_list]
                + [pl.BlockSpec(w.shape, lambda i: (0, 0)) for w in w_list]
                + [pl.BlockSpec((tm, d), lambda i: (i, 0))])
    return pl.pallas_call(
        functools.partial(_proj_res_kernel, n_in=n_in),
        grid=(m // tm,),
        in_specs=in_specs,
        out_specs=pl.BlockSpec((tm, d), lambda i: (i, 0)),
        out_shape=jax.ShapeDtypeStruct((m, d), F32),
        compiler_params=_cparams("parallel"),
        name="proj_res",
    )(*a_list, *w_list, res)


def _ffn_kernel(x_ref, g_ref, wg_ref, wu_ref, wd_ref, *rest, final):
    if final:
        gf_ref, o_ref, h_scr, acc_scr = rest
    else:
        o_ref, h_scr, acc_scr = rest
    j = pl.program_id(1)

    @pl.when(j == 0)
    def _():
        x = x_ref[...]
        h_scr[...] = _rms(x, g_ref[...]).astype(BF16)
        acc_scr[...] = x

    hb = h_scr[...]
    gate = _dot(hb, wg_ref[...])
    up = _dot(hb, wu_ref[...])
    act = gate * _sigmoid(gate) * up
    acc_scr[...] += _dot(act.astype(BF16), wd_ref[...])

    @pl.when(j == pl.num_programs(1) - 1)
    def _():
        y = acc_scr[...]
        if final:
            y = _rms(y, gf_ref[...])
        o_ref[...] = y


def ffn(x2d, g, w_gu_bf, w_down_bf, g_final=None, tm=512):
    m, d = x2d.shape
    f = w_down_bf.shape[0]
    tm = min(tm, m)
    nf = 2 if (f // 2) % LANE == 0 else 1
    tf = f // nf
    final = g_final is not None
    in_specs = [pl.BlockSpec((tm, d), lambda i, j: (i, 0)),
                pl.BlockSpec((1, d), lambda i, j: (0, 0)),
                pl.BlockSpec((d, tf), lambda i, j: (0, j)),
                pl.BlockSpec((d, tf), lambda i, j: (0, j + nf)),
                pl.BlockSpec((tf, d), lambda i, j: (j, 0))]
    args = [x2d, g.reshape(1, d), w_gu_bf, w_gu_bf, w_down_bf]
    if final:
        in_specs.append(pl.BlockSpec((1, d), lambda i, j: (0, 0)))
        args.append(g_final.reshape(1, d))
    return pl.pallas_call(
        functools.partial(_ffn_kernel, final=final),
        grid=(m // tm, nf),
        in_specs=in_specs,
        out_specs=pl.BlockSpec((tm, d), lambda i, j: (i, 0)),
        out_shape=jax.ShapeDtypeStruct((m, d), F32),
        scratch_shapes=[pltpu.VMEM((tm, d), BF16), pltpu.VMEM((tm, d), F32)],
        compiler_params=_cparams("parallel", "arbitrary"),
        name="ffn",
    )(*args)


def _gelu_tanh(x):
    c = math.sqrt(2.0 / math.pi)
    return x * (0.5 * (1.0 + jnp.tanh(c * (x + 0.044715 * (x * x * x)))))


def _compress_kernel(x_ref, pe_ref, w1_ref, w2_ref, o_ref, acc_scr):
    k = pl.program_id(1)

    @pl.when(k == 0)
    def _():
        acc_scr[...] = jnp.zeros_like(acc_scr)

    acc_scr[...] += _dot((x_ref[...] + pe_ref[...]).astype(BF16), w1_ref[...])

    @pl.when(k == pl.num_programs(1) - 1)
    def _():
        o_ref[...] = _dot(_gelu_tanh(acc_scr[...]).astype(BF16), w2_ref[...]).astype(o_ref.dtype)


def compress(x_blocks, pe_big, w1_big, w2_big, tm=256, tk=4096):
    nblk, kdim = x_blocks.shape
    tm = tm if nblk % tm == 0 else nblk
    tk = min(tk, kdim)
    return pl.pallas_call(
        _compress_kernel,
        grid=(nblk // tm, kdim // tk),
        in_specs=[pl.BlockSpec((tm, tk), lambda i, k: (i, k)),
                  pl.BlockSpec((1, tk), lambda i, k: (0, k)),
                  pl.BlockSpec((tk, KV_ROW), lambda i, k: (k, 0)),
                  pl.BlockSpec((KV_ROW, KV_ROW), lambda i, k: (0, 0))],
        out_specs=pl.BlockSpec((tm, KV_ROW), lambda i, k: (i, 0)),
        out_shape=jax.ShapeDtypeStruct((nblk, KV_ROW), BF16),
        scratch_shapes=[pltpu.VMEM((tm, KV_ROW), F32)],
        compiler_params=_cparams("parallel", "arbitrary"),
        name="compress",
    )(x_blocks, pe_big, w1_big, w2_big)


def _compress_weights(pe_k, w1_k, w2_k, pe_v, w1_v, w2_v):
    nc = 2 * NSA_KV_HEADS
    l, d, h = w1_k.shape
    w1 = jnp.zeros((l, nc, d, nc, h), F32)
    w2 = jnp.zeros((nc, h, nc, w2_k.shape[1]), F32)
    pe = []
    for c in range(nc):
        is_k = c < NSA_KV_HEADS
        w1 = w1.at[:, c, :, c, :].set(w1_k if is_k else w1_v)
        w2 = w2.at[c, :, c, :].set(w2_k if is_k else w2_v)
        pe.append(pe_k if is_k else pe_v)
    pe_big = jnp.stack(pe, axis=1).reshape(1, l * nc * d)
    return pe_big, w1.reshape(l * nc * d, nc * h).astype(BF16), w2.reshape(nc * h, nc * d).astype(BF16)


def _flash_update(m_scr, l_scr, acc_scr, rows, s, valid, x):
    s = jnp.where(valid, s, NEG)
    m_old = m_scr[rows, :]
    m_new = jnp.maximum(m_old, jnp.max(s, axis=-1, keepdims=True))
    alpha = jnp.exp(m_old - m_new)
    p = jnp.where(valid, jnp.exp(s - m_new), 0.0)
    l_scr[rows, :] = alpha * l_scr[rows, :] + jnp.sum(p, axis=-1, keepdims=True)
    acc_scr[rows, :] = alpha * acc_scr[rows, :] + _dot(p.astype(BF16), x)
    m_scr[rows, :] = m_new


def _flash_result(l_scr, acc_scr, rows):
    return acc_scr[rows, :] / jnp.maximum(l_scr[rows, :], 1e-30)


def _banded_kernel(*refs, n_groups, n_rep, window, slopes, has_sink):
    if has_sink:
        sink_ref, q_ref, kv_ref, o_ref, m_scr, l_scr, acc_scr = refs
    else:
        q_ref, kv_ref, o_ref, m_scr, l_scr, acc_scr = refs
    qt = pl.program_id(1)
    nrow = n_rep * TQ
    rows = slice(0, nrow)
    ri = lax.broadcasted_iota(jnp.int32, (TQ, TQ), 0)
    ci = lax.broadcasted_iota(jnp.int32, (TQ, TQ), 1)
    diff = jnp.concatenate([ri - ci] * n_rep, axis=0)
    for g in range(n_groups):
        heads = [g * n_rep + r for r in range(n_rep)]
        qrows = jnp.concatenate([q_ref[0, :, h * QPAD:(h + 1) * QPAD] for h in heads], axis=0)
        slope_col = _col([slopes[h] for h in heads], TQ)
        if has_sink:
            m_scr[...] = jnp.concatenate([jnp.full((TQ, 1), sink_ref[h], F32) for h in heads], axis=0)
            l_scr[...] = jnp.ones_like(l_scr)
        else:
            m_scr[...] = jnp.full_like(m_scr, NEG)
            l_scr[...] = jnp.zeros_like(l_scr)
        acc_scr[...] = jnp.zeros_like(acc_scr)

        def body(kt, carry):
            x = kv_ref[0, pl.ds(pl.multiple_of(kt * TQ, TQ), TQ), :]
            dist = diff + (qt - kt) * TQ
            s = _dot_nt(qrows, x) * SCALE - slope_col * dist.astype(F32)
            valid = (dist >= 0) & (dist <= window)
            _flash_update(m_scr, l_scr, acc_scr, rows, s, valid, x)
            return carry

        lax.fori_loop(jnp.maximum(qt - window // TQ, 0), qt + 1, body, 0)
        o = _flash_result(l_scr, acc_scr, rows)
        v0 = (NSA_KV_HEADS + g) * HEAD_DIM
        for r, h in enumerate(heads):
            o_ref[0, :, h * HEAD_DIM:(h + 1) * HEAD_DIM] = o[r * TQ:(r + 1) * TQ, v0:v0 + HEAD_DIM]


def banded_attention(q_pad, kv_bf, kv_block, n_groups, n_rep, window, sinks=None):
    b, seq = q_pad.shape[:2]
    n_heads = n_groups * n_rep
    has_sink = sinks is not None
    in_specs = [pl.BlockSpec((1, TQ, n_heads * QPAD), lambda i, j: (i, j, 0)),
                pl.BlockSpec((1, seq, KV_ROW), lambda i, j: (i, 0, kv_block))]
    args = [q_pad, kv_bf]
    if has_sink:
        in_specs.insert(0, pl.BlockSpec(memory_space=pltpu.SMEM))
        args.insert(0, sinks)
    nrow = n_rep * TQ
    return pl.pallas_call(
        functools.partial(_banded_kernel, n_groups=n_groups, n_rep=n_rep, window=window,
                          slopes=_alibi(n_heads), has_sink=has_sink),
        grid=(b, seq // TQ),
        in_specs=in_specs,
        out_specs=pl.BlockSpec((1, TQ, n_heads * HEAD_DIM), lambda i, j: (i, j, 0)),
        out_shape=jax.ShapeDtypeStruct((b, seq, n_heads * HEAD_DIM), F32),
        scratch_shapes=[pltpu.VMEM((nrow, 1), F32), pltpu.VMEM((nrow, 1), F32),
                        pltpu.VMEM((nrow, KV_ROW), F32)],
        compiler_params=_cparams("parallel", "arbitrary"),
        name="banded_attention",
    )(*args)


def _banded_decode_kernel(*refs, n_groups, n_rep, window, slopes, has_sink, pos0):
    if has_sink:
        sink_ref, q_ref, st_ref, new_ref, o_ref = refs
    else:
        q_ref, st_ref, new_ref, o_ref = refs
    nq = q_ref.shape[1]
    npast = st_ref.shape[1]
    nrow = n_rep * nq
    xs = st_ref[0].astype(BF16)
    xn = jnp.concatenate([new_ref[0], jnp.zeros((TQ - nq, KV_ROW), F32)], axis=0).astype(BF16)
    qi_s = jnp.concatenate([lax.broadcasted_iota(jnp.int32, (nq, npast), 0)] * n_rep, axis=0)
    kj_s = lax.broadcasted_iota(jnp.int32, (nrow, npast), 1)
    dist_s = npast + qi_s - kj_s
    valid_s = (dist_s >= 0) & (dist_s <= window) & (pos0 - npast + kj_s >= 0)
    qi_n = jnp.concatenate([lax.broadcasted_iota(jnp.int32, (nq, TQ), 0)] * n_rep, axis=0)
    kj_n = lax.broadcasted_iota(jnp.int32, (nrow, TQ), 1)
    dist_n = qi_n - kj_n
    valid_n = (dist_n >= 0) & (dist_n <= window) & (kj_n < nq)
    for g in range(n_groups):
        heads = [g * n_rep + r for r in range(n_rep)]
        qrows = jnp.concatenate([q_ref[0, :, h * QPAD:(h + 1) * QPAD] for h in heads], axis=0).astype(BF16)
        slope_col = _col([slopes[h] for h in heads], nq)
        s_s = jnp.where(valid_s, _dot_nt(qrows, xs) * SCALE - slope_col * dist_s.astype(F32), NEG)
        s_n = jnp.where(valid_n, _dot_nt(qrows, xn) * SCALE - slope_col * dist_n.astype(F32), NEG)
        m = jnp.maximum(jnp.max(s_s, axis=-1, keepdims=True), jnp.max(s_n, axis=-1, keepdims=True))
        if has_sink:
            sink_col = jnp.concatenate([jnp.full((nq, 1), sink_ref[h], F32) for h in heads], axis=0)
            m = jnp.maximum(m, sink_col)
        p_s = jnp.where(valid_s, jnp.exp(s_s - m), 0.0)
        p_n = jnp.where(valid_n, jnp.exp(s_n - m), 0.0)
        den = jnp.sum(p_s, axis=-1, keepdims=True) + jnp.sum(p_n, axis=-1, keepdims=True)
        if has_sink:
            den = den + jnp.exp(sink_col - m)
        o = (_dot(p_s.astype(BF16), xs) + _dot(p_n.astype(BF16), xn)) / jnp.maximum(den, 1e-30)
        v0 = (NSA_KV_HEADS + g) * HEAD_DIM
        for r, h in enumerate(heads):
            o_ref[0, :, h * HEAD_DIM:(h + 1) * HEAD_DIM] = o[r * nq:(r + 1) * nq, v0:v0 + HEAD_DIM]


def banded_decode(q_pad, state_kv, new_kv, n_groups, n_rep, window, pos0, sinks=None):
    b, nq = q_pad.shape[:2]
    npast = state_kv.shape[1]
    n_heads = n_groups * n_rep
    has_sink = sinks is not None
    in_specs = [pl.BlockSpec((1, nq, n_heads * QPAD), lambda i: (i, 0, 0)),
                pl.BlockSpec((1, npast, KV_ROW), lambda i: (i, 0, 0)),
                pl.BlockSpec((1, nq, KV_ROW), lambda i: (i, 0, 0))]
    args = [q_pad, state_kv, new_kv]
    if has_sink:
        in_specs.insert(0, pl.BlockSpec(memory_space=pltpu.SMEM))
        args.insert(0, sinks)
    return pl.pallas_call(
        functools.partial(_banded_decode_kernel, n_groups=n_groups, n_rep=n_rep, window=window,
                          slopes=_alibi(n_heads), has_sink=has_sink, pos0=pos0),
        grid=(b,),
        in_specs=in_specs,
        out_specs=pl.BlockSpec((1, nq, n_heads * HEAD_DIM), lambda i: (i, 0, 0)),
        out_shape=jax.ShapeDtypeStruct((b, nq, n_heads * HEAD_DIM), F32),
        compiler_params=_cparams("parallel"),
        name="banded_decode",
    )(*args)


def _compressed_branch(qrows, kcb, slope_col, tpos, n_rep, nq):
    nb = kcb.shape[0]
    ncol = lax.broadcasted_iota(jnp.int32, tpos.shape, 1)
    dist = tpos - ((ncol + 1) * CMP_BLOCK - 1)
    valid = dist >= 0
    s = jnp.where(valid, _dot_nt(qrows, kcb) * SCALE - slope_col * dist.astype(F32), NEG)
    m = jnp.max(s, axis=-1, keepdims=True)
    e = jnp.where(valid, jnp.exp(s - m), 0.0)
    p = e / jnp.maximum(jnp.sum(e, axis=-1, keepdims=True), 1e-30)
    o = _dot(p.astype(BF16), kcb)
    imp = p[0:nq]
    for r in range(1, n_rep):
        imp = imp + p[r * nq:(r + 1) * nq]
    return o, imp


def _select_blocks(imp, cur, n_sel):
    nb = imp.shape[1]
    ncol = lax.broadcasted_iota(jnp.int32, imp.shape, 1)
    count = jnp.zeros(imp.shape, F32)
    for i in range(nb):
        ci = imp[:, i:i + 1]
        ahead = jnp.where(ci > imp, 1.0, jnp.where(ci == imp, jnp.where(ncol > i, 1.0, 0.0), 0.0))
        count = count + jnp.where(cur > i, ahead, 0.0)
    taken = jnp.where(ncol < cur, jnp.where(count < n_sel - 1, 1.0, 0.0), jnp.where(ncol == cur, 1.0, 0.0))
    return taken


def _nsa_prompt_kernel(q_ref, gd_ref, kc_ref, kv_ref, ow_ref, o_ref, m_scr, l_scr, acc_scr, *, n_sel):
    qt = pl.program_id(1)
    n_rep = NSA_GROUP
    nrow = n_rep * TQ
    rows = slice(0, nrow)
    slopes = _alibi(NSA_HEADS)
    nb = kc_ref.shape[1]
    kcb = kc_ref[0]
    gates = _sigmoid(gd_ref[0])
    ri = lax.broadcasted_iota(jnp.int32, (TQ, TQ), 0)
    ci = lax.broadcasted_iota(jnp.int32, (TQ, TQ), 1)
    diff = jnp.concatenate([ri - ci] * n_rep, axis=0)
    tq_nb = qt * TQ + lax.broadcasted_iota(jnp.int32, (TQ, nb), 0)
    tpos = jnp.concatenate([tq_nb] * n_rep, axis=0)
    cur = tq_nb // CMP_BLOCK
    blk_row = lax.broadcasted_iota(jnp.int32, (nb, TQ), 0)
    key_blk = lax.broadcasted_iota(jnp.int32, (nb, TQ), 1) // CMP_BLOCK
    for g in range(NSA_KV_HEADS):
        heads = [g * n_rep + r for r in range(n_rep)]
        qrows = jnp.concatenate([q_ref[0, :, h * QPAD:(h + 1) * QPAD] for h in heads], axis=0)
        slope_col = _col([slopes[h] for h in heads], TQ)
        o_cmp, imp = _compressed_branch(qrows, kcb, slope_col, tpos, n_rep, TQ)
        taken = _select_blocks(imp, cur, n_sel).astype(BF16)
        m_scr[...] = jnp.full_like(m_scr, NEG)
        l_scr[...] = jnp.zeros_like(l_scr)
        acc_scr[...] = jnp.zeros_like(acc_scr)

        def body(kt, carry):
            x = kv_ref[0, pl.ds(pl.multiple_of(kt * TQ, TQ), TQ), :]
            dist = diff + (qt - kt) * TQ
            s = _dot_nt(qrows, x) * SCALE - slope_col * dist.astype(F32)
            expand = jnp.where(blk_row == key_blk + kt * (TQ // CMP_BLOCK), 1.0, 0.0).astype(BF16)
            hit = _dot(taken, expand)
            hit = jnp.concatenate([hit] * n_rep, axis=0)
            valid = jnp.where(dist >= 0, hit, 0.0) > 0.5
            _flash_update(m_scr, l_scr, acc_scr, rows, s, valid, x)
            return carry

        lax.fori_loop(0, qt + 1, body, 0)
        o_slc = _flash_result(l_scr, acc_scr, rows)
        v0 = (NSA_KV_HEADS + g) * HEAD_DIM
        for r, h in enumerate(heads):
            rr = slice(r * TQ, (r + 1) * TQ)
            hh = slice(h * HEAD_DIM, (h + 1) * HEAD_DIM)
            o_ref[0, :, hh] = (gates[:, 3 * h:3 * h + 1] * o_cmp[rr, v0:v0 + HEAD_DIM]
                               + gates[:, 3 * h + 1:3 * h + 2] * o_slc[rr, v0:v0 + HEAD_DIM]
                               + gates[:, 3 * h + 2:3 * h + 3] * ow_ref[0, :, hh])


def nsa_prompt(q_pad, gd, kc, kv_bf, kv_block, o_win):
    b, seq = q_pad.shape[:2]
    nb = kc.shape[1]
    nrow = NSA_GROUP * TQ
    return pl.pallas_call(
        functools.partial(_nsa_prompt_kernel, n_sel=min(SEL_TOPK, nb)),
        grid=(b, seq // TQ),
        in_specs=[pl.BlockSpec((1, TQ, NSA_HEADS * QPAD), lambda i, j: (i, j, 0)),
                  pl.BlockSpec((1, TQ, LANE), lambda i, j: (i, j, 0)),
                  pl.BlockSpec((1, nb, KV_ROW), lambda i, j: (i, 0, 0)),
                  pl.BlockSpec((1, seq, KV_ROW), lambda i, j: (i, 0, kv_block)),
                  pl.BlockSpec((1, TQ, NSA_WIDTH), lambda i, j: (i, j, 0))],
        out_specs=pl.BlockSpec((1, TQ, NSA_WIDTH), lambda i, j: (i, j, 0)),
        out_shape=jax.ShapeDtypeStruct((b, seq, NSA_WIDTH), F32),
        scratch_shapes=[pltpu.VMEM((nrow, 1), F32), pltpu.VMEM((nrow, 1), F32),
                        pltpu.VMEM((nrow, KV_ROW), F32)],
        compiler_params=_cparams("parallel", "arbitrary"),
        name="nsa_prompt",
    )(q_pad, gd, kc, kv_bf, o_win)


def _nsa_decode_kernel(pt_ref, q_ref, gd_ref, kc_ref, new_ref, ow_ref, *rest, pages_per_step, pos0, n_sel):
    page_refs = rest[:pages_per_step]
    o_ref, m_scr, l_scr, acc_scr, taken_scr, ocmp_scr = rest[pages_per_step:]
    p = pl.program_id(1)
    n_rep = NSA_GROUP
    nq = q_ref.shape[1]
    nrow = n_rep * nq
    nb = kc_ref.shape[1]
    page = page_refs[0].shape[0]
    nkey = pages_per_step * page
    slopes = _alibi(NSA_HEADS)

    def group_rows(g):
        heads = [g * n_rep + r for r in range(n_rep)]
        qrows = jnp.concatenate([q_ref[0, :, h * QPAD:(h + 1) * QPAD] for h in heads], axis=0).astype(BF16)
        return heads, qrows, _col([slopes[h] for h in heads], nq), slice(g * nrow, (g + 1) * nrow)

    @pl.when(p == 0)
    def _():
        kcb = kc_ref[0]
        tq_nb = pos0 + lax.broadcasted_iota(jnp.int32, (nq, nb), 0)
        tpos = jnp.concatenate([tq_nb] * n_rep, axis=0)
        for g in range(NSA_KV_HEADS):
            _, qrows, slope_col, rows = group_rows(g)
            o_cmp, imp = _compressed_branch(qrows, kcb, slope_col, tpos, n_rep, nq)
            ocmp_scr[rows, :] = o_cmp
            taken_scr[g * nq:(g + 1) * nq, :] = _select_blocks(imp, jnp.full((nq, nb), nb, jnp.int32), n_sel)
        m_scr[...] = jnp.full_like(m_scr, NEG)
        l_scr[...] = jnp.zeros_like(l_scr)
        acc_scr[...] = jnp.zeros_like(acc_scr)

    x = jnp.concatenate([r[...] for r in page_refs], axis=0).astype(BF16)
    key0 = p * nkey
    qi = jnp.concatenate([lax.broadcasted_iota(jnp.int32, (nq, nkey), 0)] * n_rep, axis=0)
    kj = lax.broadcasted_iota(jnp.int32, (nrow, nkey), 1)
    dist = pos0 + qi - (key0 + kj)
    blk_row = lax.broadcasted_iota(jnp.int32, (nb, nkey), 0)
    key_blk = (key0 + lax.broadcasted_iota(jnp.int32, (nb, nkey), 1)) // CMP_BLOCK
    expand = jnp.where(blk_row == key_blk, 1.0, 0.0).astype(BF16)
    for g in range(NSA_KV_HEADS):
        _, qrows, slope_col, rows = group_rows(g)
        s = _dot_nt(qrows, x) * SCALE - slope_col * dist.astype(F32)
        hit = _dot(taken_scr[g * nq:(g + 1) * nq, :].astype(BF16), expand)
        hit = jnp.concatenate([hit] * n_rep, axis=0)
        valid = jnp.where(dist >= 0, hit, 0.0) > 0.5
        _flash_update(m_scr, l_scr, acc_scr, rows, s, valid, x)

    @pl.when(p == pl.num_programs(1) - 1)
    def _():
        xn = jnp.concatenate([new_ref[0], jnp.zeros((TQ - nq, KV_ROW), F32)], axis=0).astype(BF16)
        qi_n = jnp.concatenate([lax.broadcasted_iota(jnp.int32, (nq, TQ), 0)] * n_rep, axis=0)
        kj_n = lax.broadcasted_iota(jnp.int32, (nrow, TQ), 1)
        dist_n = qi_n - kj_n
        valid_n = (dist_n >= 0) & (kj_n < nq)
        gates = _sigmoid(gd_ref[0])
        for g in range(NSA_KV_HEADS):
            heads, qrows, slope_col, rows = group_rows(g)
            s = _dot_nt(qrows, xn) * SCALE - slope_col * dist_n.astype(F32)
            _flash_update(m_scr, l_scr, acc_scr, rows, s, valid_n, xn)
            o_slc = _flash_result(l_scr, acc_scr, rows)
            o_cmp = ocmp_scr[rows, :]
            v0 = (NSA_KV_HEADS + g) * HEAD_DIM
            for r, h in enumerate(heads):
                rr = slice(r * nq, (r + 1) * nq)
                hh = slice(h * HEAD_DIM, (h + 1) * HEAD_DIM)
                o_ref[0, :, hh] = (gates[:, 3 * h:3 * h + 1] * o_cmp[rr, v0:v0 + HEAD_DIM]
                                   + gates[:, 3 * h + 1:3 * h + 2] * o_slc[rr, v0:v0 + HEAD_DIM]
                                   + gates[:, 3 * h + 2:3 * h + 3] * ow_ref[0, :, hh])


def nsa_decode(page_table, q_pad, gd, kc, new_kv, o_win, pool, pos0, pages_per_step=8):
    b, nq = q_pad.shape[:2]
    nb = kc.shape[1]
    n_pages = page_table.shape[1]
    page = pool.shape[1]
    pps = math.gcd(pages_per_step, n_pages)
    nrow_all = NSA_HEADS * nq

    def page_spec(j):
        return pl.BlockSpec((None, page, KV_ROW), lambda i, p, pt: (pt[i, p * pps + j], 0, 0))

    grid_spec = pltpu.PrefetchScalarGridSpec(
        num_scalar_prefetch=1,
        grid=(b, n_pages // pps),
        in_specs=[pl.BlockSpec((1, nq, NSA_HEADS * QPAD), lambda i, p, pt: (i, 0, 0)),
                  pl.BlockSpec((1, nq, LANE), lambda i, p, pt: (i, 0, 0)),
                  pl.BlockSpec((1, nb, KV_ROW), lambda i, p, pt: (i, 0, 0)),
                  pl.BlockSpec((1, nq, KV_ROW), lambda i, p, pt: (i, 0, 0)),
                  pl.BlockSpec((1, nq, NSA_WIDTH), lambda i, p, pt: (i, 0, 0))]
                 + [page_spec(j) for j in range(pps)],
        out_specs=pl.BlockSpec((1, nq, NSA_WIDTH), lambda i, p, pt: (i, 0, 0)),
        scratch_shapes=[pltpu.VMEM((nrow_all, 1), F32), pltpu.VMEM((nrow_all, 1), F32),
                        pltpu.VMEM((nrow_all, KV_ROW), F32),
                        pltpu.VMEM((NSA_KV_HEADS * nq, nb), F32),
                        pltpu.VMEM((nrow_all, KV_ROW), F32)])
    return pl.pallas_call(
        functools.partial(_nsa_decode_kernel, pages_per_step=pps, pos0=pos0, n_sel=min(SEL_TOPK, nb + 1)),
        grid_spec=grid_spec,
        out_shape=jax.ShapeDtypeStruct((b, nq, NSA_WIDTH), F32),
        compiler_params=_cparams("parallel", "arbitrary"),
        name="nsa_decode",
    )(page_table, q_pad, gd, kc, new_kv, o_win, *([pool] * pps))


def _cumsum_rows(v):
    n = v.shape[0]
    ri = lax.broadcasted_iota(jnp.int32, v.shape, 0)
    sh = 1
    while sh < n:
        v = v + jnp.where(ri >= sh, pltpu.roll(v, sh, 0), 0.0)
        sh *= 2
    return v


def _pair_lanes(c0, c1):
    lane = lax.broadcasted_iota(jnp.int32, (c0.shape[0], LANE), 1)
    return jnp.where(lane < SSM_HEAD_DIM, c0, c1)


def _ssd_kernel(z_ref, xbc_ref, gd_ref, tail_ref, s0_ref, cw_ref, cb_ref, dtb_ref, alog_ref, dsk_ref, ng_ref,
                y_ref, sfin_ref, xp_scr, st_scr, y_scr, *, n_valid):
    c = pl.program_id(1)
    q = xbc_ref.shape[1]

    @pl.when(c == 0)
    def _():
        xp_scr[0:SUBLANE, :] = tail_ref[0]
        st_scr[...] = s0_ref[0]

    xp_scr[SUBLANE:SUBLANE + q, :] = xbc_ref[0]
    acc = cb_ref[...]
    for k in range(SSM_CONV):
        lo = SUBLANE - (SSM_CONV - 1) + k
        acc = acc + xp_scr[lo:lo + q, :] * cw_ref[k:k + 1, :]
    nxt = xp_scr[q:q + SUBLANE, :]
    xp_scr[0:SUBLANE, :] = nxt
    act = acc * _sigmoid(acc)

    raw = gd_ref[0] + dtb_ref[...]
    dt = jnp.maximum(raw, 0.0) + jnp.log1p(jnp.exp(-jnp.abs(raw)))
    if n_valid < q:
        dt = jnp.where(lax.broadcasted_iota(jnp.int32, dt.shape, 0) < n_valid, dt, 0.0)
    acum = _cumsum_rows(dt * (-jnp.exp(alog_ref[...])))
    acum_t = acum.T
    ri = lax.broadcasted_iota(jnp.int32, (q, q), 0)
    ci = lax.broadcasted_iota(jnp.int32, (q, q), 1)
    causal = ri >= ci
    half = lax.broadcasted_iota(jnp.int32, (2 * SSM_HEAD_DIM, 1), 0) < SSM_HEAD_DIM
    hpg = SSM_HEADS // SSM_GROUPS
    for pair in range(SSM_HEADS // 2):
        grp = (2 * pair) // hpg
        lanes = slice(pair * LANE, (pair + 1) * LANE)
        bm = act[:, SSM_D_INNER + grp * SSM_STATE:SSM_D_INNER + (grp + 1) * SSM_STATE].astype(BF16)
        cm_lo = SSM_D_INNER + SSM_GROUPS * SSM_STATE + grp * SSM_STATE
        cm = act[:, cm_lo:cm_lo + SSM_STATE].astype(BF16)
        cb = _dot_nt(cm, bm)
        xs = act[:, lanes]
        cols = []
        for h in (2 * pair, 2 * pair + 1):
            ln = DT_LANE + h
            cols.append((acum[:, ln:ln + 1], acum_t[ln:ln + 1, :], acum[q - 1:q, ln:ln + 1], dt[:, ln:ln + 1]))
        xdt = xs * _pair_lanes(cols[0][3], cols[1][3])
        xdt_bf = xdt.astype(BF16)
        y_parts = []
        for a_col, a_row, _, _ in cols:
            seg = a_col - a_row
            lmat = jnp.where(causal, jnp.exp(jnp.where(causal, seg, 0.0)), 0.0)
            y_parts.append(_dot((cb * lmat).astype(BF16), xdt_bf))
        lane = lax.broadcasted_iota(jnp.int32, (q, LANE), 1)
        y_diag = jnp.where(lane < SSM_HEAD_DIM, y_parts[0], y_parts[1])
        st = st_scr[pair * LANE:(pair + 1) * LANE, :]
        y_off = _dot_nt(cm, st.astype(BF16)) * _pair_lanes(jnp.exp(cols[0][0]), jnp.exp(cols[1][0]))
        dec_end = _pair_lanes(jnp.exp(cols[0][2] - cols[0][0]), jnp.exp(cols[1][2] - cols[1][0]))
        cs = _dot((xdt * dec_end).T.astype(BF16), bm)
        st_scr[pair * LANE:(pair + 1) * LANE, :] = st * jnp.where(half, jnp.exp(cols[0][2]), jnp.exp(cols[1][2])) + cs
        dsk = _pair_lanes(jnp.zeros((q, 1), F32) + dsk_ref[:, 2 * pair:2 * pair + 1],
                          jnp.zeros((q, 1), F32) + dsk_ref[:, 2 * pair + 1:2 * pair + 2])
        y_scr[:, lanes] = y_diag + y_off + dsk * xs

    z = z_ref[0]
    y_ref[0] = _rms(y_scr[...] * (z * _sigmoid(z)), ng_ref[...])

    @pl.when(c == pl.num_programs(1) - 1)
    def _():
        sfin_ref[0] = st_scr[...]


def ssd_mixer(z, xbc, gd, tail8, s0, conv_w, conv_b, dt_bias, a_log, d_skip, norm_g, n_valid):
    b, seq = z.shape[:2]
    nst = SSM_HEADS * SSM_HEAD_DIM

    def lane_piece(v):
        return jnp.zeros((1, LANE), F32).at[0, DT_LANE:DT_LANE + SSM_HEADS].set(v)

    alog_piece = jnp.full((1, LANE), -100.0, F32).at[0, DT_LANE:DT_LANE + SSM_HEADS].set(a_log)
    full2 = lambda shape: pl.BlockSpec(shape, lambda i, c: (0, 0))
    return pl.pallas_call(
        functools.partial(_ssd_kernel, n_valid=n_valid),
        grid=(b, seq // TQ),
        in_specs=[pl.BlockSpec((1, TQ, SSM_D_INNER), lambda i, c: (i, c, 0)),
                  pl.BlockSpec((1, TQ, SSM_CONV_DIM), lambda i, c: (i, c, 0)),
                  pl.BlockSpec((1, TQ, LANE), lambda i, c: (i, c, 0)),
                  pl.BlockSpec((1, SUBLANE, SSM_CONV_DIM), lambda i, c: (i, 0, 0)),
                  pl.BlockSpec((1, nst, SSM_STATE), lambda i, c: (i, 0, 0)),
                  full2((SSM_CONV, SSM_CONV_DIM)), full2((1, SSM_CONV_DIM)), full2((1, LANE)),
                  full2((1, LANE)), full2((1, SSM_HEADS)), full2((1, SSM_D_INNER))],
        out_specs=[pl.BlockSpec((1, TQ, SSM_D_INNER), lambda i, c: (i, c, 0)),
                   pl.BlockSpec((1, nst, SSM_STATE), lambda i, c: (i, 0, 0))],
        out_shape=[jax.ShapeDtypeStruct((b, seq, SSM_D_INNER), F32),
                   jax.ShapeDtypeStruct((b, nst, SSM_STATE), F32)],
        scratch_shapes=[pltpu.VMEM((SUBLANE + TQ, SSM_CONV_DIM), F32),
                        pltpu.VMEM((nst, SSM_STATE), F32),
                        pltpu.VMEM((TQ, SSM_D_INNER), F32)],
        compiler_params=_cparams("parallel", "arbitrary"),
        name="ssd_mixer",
    )(z, xbc, gd, tail8, s0, conv_w, conv_b.reshape(1, -1), lane_piece(dt_bias), alog_piece,
      d_skip.reshape(1, -1), norm_g.reshape(1, -1))


def _pad_q_cols(w_q, n_heads, n_rep):
    d = w_q.shape[0]
    out = jnp.zeros((d, n_heads, QPAD), w_q.dtype)
    wq = w_q.reshape(d, n_heads, HEAD_DIM)
    for h in range(n_heads):
        g = h // n_rep
        out = out.at[:, h, g * HEAD_DIM:(g + 1) * HEAD_DIM].set(wq[:, h])
    return out.reshape(d, n_heads * QPAD)


def _even_weights(w_in):
    d = w_in.shape[0]
    o = np.cumsum([0, NSA_WIDTH, KV_ROW, KV_ROW, KV_ROW, 3 * NSA_HEADS, SSM_D_INNER, SSM_CONV_DIM, SSM_HEADS])
    q, kvc, kvs, kvw, gt, z, xbc, dtr = (w_in[:, o[i]:o[i + 1]] for i in range(8))
    gd = jnp.zeros((d, LANE), F32).at[:, :3 * NSA_HEADS].set(gt).at[:, DT_LANE:DT_LANE + SSM_HEADS].set(dtr)
    w = jnp.concatenate([_pad_q_cols(q, NSA_HEADS, NSA_GROUP), kvc, kvs, kvw, z, xbc, gd], axis=1).astype(BF16)
    qw = NSA_HEADS * QPAD
    kv0 = qw
    z0 = kv0 + 3 * KV_ROW
    x0 = z0 + SSM_D_INNER
    g0 = x0 + SSM_CONV_DIM
    pieces = [(0, qw), (kv0, KV_ROW), (kv0 + KV_ROW, KV_ROW), (kv0 + 2 * KV_ROW, KV_ROW), (kv0, 3 * KV_ROW),
              (z0, SSM_D_INNER), (x0, SSM_CONV_DIM), (g0, LANE)]
    return w, pieces


def _odd_weights(w_in):
    q, kv = w_in[:, :SWA_WIDTH], w_in[:, SWA_WIDTH:]
    w = jnp.concatenate([_pad_q_cols(q, SWA_HEADS, SWA_GROUP), kv], axis=1).astype(BF16)
    qw = SWA_HEADS * QPAD
    return w, [(0, qw), (qw, KV_ROW), (qw, KV_ROW)]


def _even_layer(x, decode, p, cache):
    (norm_g, w_in, w_out, cw, ssm_p) = p
    b, seq, d = x.shape
    m = b * seq
    w, pieces = _even_weights(w_in)
    qdt = F32 if decode else BF16
    q, kvc, kvs, kvw, kvbf, z, xbc, gd = norm_proj(
        x.reshape(m, d), norm_g, w, pieces, [qdt, F32, F32, F32, BF16, F32, F32, F32])
    q = q.reshape(b, seq, -1)
    gd3 = gd.reshape(b, seq, LANE)
    conv_w, conv_b, dt_bias, a_log, d_skip, ssm_norm = ssm_p
    if not decode:
        kc = compress(kvc.reshape(m // CMP_BLOCK, CMP_BLOCK * KV_ROW), *cw).reshape(b, seq // CMP_BLOCK, KV_ROW)
        kvbf3 = kvbf.reshape(b, seq, 3 * KV_ROW)
        o_win = banded_attention(q, kvbf3, 2, NSA_KV_HEADS, NSA_GROUP, NSA_WINDOW)
        o_nsa = nsa_prompt(q, gd3, kc, kvbf3, 1, o_win)
        tail8 = jnp.zeros((b, SUBLANE, SSM_CONV_DIM), F32)
        s0 = jnp.zeros((b, SSM_D_INNER, SSM_STATE), F32)
        o_ssm, s_fin = ssd_mixer(z.reshape(b, seq, -1), xbc.reshape(b, seq, -1), gd3, tail8, s0,
                                 conv_w, conv_b, dt_bias, a_log, d_skip, ssm_norm, TQ)
        xbc3 = xbc.reshape(b, seq, -1)
        conv_new = xbc3[:, seq - (SSM_CONV - 1):]
        win_out = kvw.reshape(b, seq, KV_ROW)[:, seq - min(NSA_WINDOW, seq):]
    else:
        cmp_pool, slc_pool, page_table, win_state, conv0, ssm0 = cache
        n_pool, page = cmp_pool.shape[:2]
        past = page_table.shape[1] * page
        assert page % CMP_BLOCK == 0 and seq < CMP_BLOCK and TQ % seq == 0
        bpp = page // CMP_BLOCK
        kc_pool = compress(cmp_pool.reshape(n_pool * bpp, CMP_BLOCK * KV_ROW), *cw)
        kc = kc_pool.reshape(n_pool, bpp, KV_ROW)[page_table].reshape(b, past // CMP_BLOCK, KV_ROW)
        win_state = win_state.reshape(b, -1, KV_ROW)
        kvw3 = kvw.reshape(b, seq, KV_ROW)
        o_win = banded_decode(q, win_state, kvw3, NSA_KV_HEADS, NSA_GROUP, NSA_WINDOW, past)
        o_nsa = nsa_decode(page_table, q, gd3, kc, kvs.reshape(b, seq, KV_ROW), o_win,
                           slc_pool.reshape(n_pool, page, KV_ROW), past)
        padr = lambda a: jnp.pad(a.reshape(b, seq, -1), ((0, 0), (0, TQ - seq), (0, 0)))
        tail8 = jnp.pad(conv0, ((0, 0), (SUBLANE - (SSM_CONV - 1), 0), (0, 0)))
        o_ssm, s_fin = ssd_mixer(padr(z), padr(xbc), padr(gd), tail8, ssm0.reshape(b, SSM_D_INNER, SSM_STATE),
                                 conv_w, conv_b, dt_bias, a_log, d_skip, ssm_norm, seq)
        o_ssm = o_ssm[:, :seq]
        conv_new = jnp.concatenate([conv0, xbc.reshape(b, seq, -1)], axis=1)[:, -(SSM_CONV - 1):]
        win_out = jnp.concatenate([win_state, kvw3], axis=1)[:, -win_state.shape[1]:]
    x = proj_res([o_nsa.reshape(m, -1), o_ssm.reshape(m, -1)],
                 [w_out[:NSA_WIDTH].astype(BF16), w_out[NSA_WIDTH:].astype(BF16)], x.reshape(m, d))
    kv6 = (b, seq, 2, NSA_KV_HEADS, HEAD_DIM)
    outs = (kvc.reshape(kv6), kvs.reshape(kv6), win_out.reshape((b, -1) + kv6[2:]),
            s_fin.reshape(b, SSM_HEADS, SSM_HEAD_DIM, SSM_STATE), conv_new)
    return x.reshape(b, seq, d), outs


def _odd_layer(x, decode, p, swa_state):
    norm_g, w_in, w_out, sinks = p
    b, seq, d = x.shape
    m = b * seq
    w, pieces = _odd_weights(w_in)
    q, kv, kvbf = norm_proj(x.reshape(m, d), norm_g, w, pieces, [F32 if decode else BF16, F32, BF16])
    q = q.reshape(b, seq, -1)
    kv3 = kv.reshape(b, seq, KV_ROW)
    if not decode:
        o = banded_attention(q, kvbf.reshape(b, seq, KV_ROW), 0, SWA_KV_HEADS, SWA_GROUP, SWA_WINDOW, sinks)
        kv_out = kv3[:, seq - min(SWA_WINDOW, seq):]
    else:
        st = swa_state.reshape(b, -1, KV_ROW)
        past = decode
        o = banded_decode(q, st, kv3, SWA_KV_HEADS, SWA_GROUP, SWA_WINDOW, past, sinks)
        kv_out = jnp.concatenate([st, kv3], axis=1)[:, -st.shape[1]:]
    x = proj_res([o.reshape(m, -1)], [w_out.astype(BF16)], x.reshape(m, d))
    return x.reshape(b, seq, d), kv_out.reshape(b, -1, 2, SWA_KV_HEADS, HEAD_DIM)


def kernel(x_prompt, x_sample, cache_nsa_cmp_kv, cache_nsa_slc_kv, state_nsa_win_kv, state_ssm, state_ssm_conv, state_swa_kv, page_table, norm_mix, norm_ffn, norm_final, w_in_even, w_out_even, cmp_pe_k, cmp_w1_k, cmp_w2_k, cmp_pe_v, cmp_w1_v, cmp_w2_v, ssm_conv_w, ssm_conv_b, ssm_dt_bias, ssm_a_log, ssm_d, ssm_norm, w_in_odd, w_out_odd, swa_sinks, w_gate_up, w_down):
    depth = norm_mix.shape[0]
    page = cache_nsa_cmp_kv.shape[2]
    past = page_table.shape[1] * page
    xp, xs = x_prompt, x_sample
    outs_p = [[] for _ in range(6)]
    outs_s = [[] for _ in range(6)]
    for layer in range(depth):
        if layer % 2 == 0:
            e = layer // 2
            cw = _compress_weights(cmp_pe_k[e], cmp_w1_k[e], cmp_w2_k[e], cmp_pe_v[e], cmp_w1_v[e], cmp_w2_v[e])
            ssm_p = (ssm_conv_w[e], ssm_conv_b[e], ssm_dt_bias[e], ssm_a_log[e], ssm_d[e], ssm_norm[e])
            p = (norm_mix[layer], w_in_even[e], w_out_even[e], cw, ssm_p)
            xp, o = _even_layer(xp, False, p, None)
            for lst, v in zip(outs_p[:5], o):
                lst.append(v)
            n_pool = cache_nsa_cmp_kv.shape[1]
            cache = (cache_nsa_cmp_kv[e].reshape(n_pool, page, KV_ROW), cache_nsa_slc_kv[e].reshape(n_pool, page, KV_ROW),
                     page_table, state_nsa_win_kv[e], state_ssm_conv[e], state_ssm[e])
            xs, o = _even_layer(xs, True, p, cache)
            for lst, v in zip(outs_s[:5], o):
                lst.append(v)
        else:
            o_idx = layer // 2
            p = (norm_mix[layer], w_in_odd[o_idx], w_out_odd[o_idx], swa_sinks[o_idx])
            xp, kw = _odd_layer(xp, 0, p, None)
            outs_p[5].append(kw)
            xs, kw = _odd_layer(xs, past, p, state_swa_kv[o_idx])
            outs_s[5].append(kw)
        gf = norm_final if layer == depth - 1 else None
        wgu, wd = w_gate_up[layer].astype(BF16), w_down[layer].astype(BF16)
        xp = ffn(xp.reshape(-1, xp.shape[-1]), norm_ffn[layer], wgu, wd, gf).reshape(xp.shape)
        xs = ffn(xs.reshape(-1, xs.shape[-1]), norm_ffn[layer], wgu, wd, gf).reshape(xs.shape)
    return (xp, xs) + tuple(jnp.stack(v) for v in outs_p) + tuple(jnp.stack(v) for v in outs_s)
```

```python
import functools
import math

import numpy as np
import jax
import jax.numpy as jnp
from jax import lax
from jax.experimental import pallas as pl
from jax.experimental.pallas import tpu as pltpu

F32 = jnp.float32
BF16 = jnp.bfloat16

HEAD_DIM = 64
NSA_HEADS = 8
NSA_KV_HEADS = 2
NSA_GROUP = NSA_HEADS // NSA_KV_HEADS
CMP_BLOCK = 64
SEL_TOPK = 16
NSA_WINDOW = 512
SSM_HEADS = 8
SSM_HEAD_DIM = 64
SSM_D_INNER = SSM_HEADS * SSM_HEAD_DIM
SSM_GROUPS = 2
SSM_STATE = 128
SSM_CONV = 4
SSM_CONV_DIM = SSM_D_INNER + 2 * SSM_GROUPS * SSM_STATE
SWA_HEADS = 16
SWA_KV_HEADS = 2
SWA_GROUP = SWA_HEADS // SWA_KV_HEADS
SWA_WINDOW = 128
RMS_EPS = 1e-6
NSA_WIDTH = NSA_HEADS * HEAD_DIM
SWA_WIDTH = SWA_HEADS * HEAD_DIM
SCALE = HEAD_DIM ** -0.5

LANE = 128
SUBLANE = 8
KV_ROW = 2 * NSA_KV_HEADS * HEAD_DIM
QPAD = KV_ROW
TQ = 128
NEG = -1e30
VMEM_LIMIT = 56 * 1024 * 1024
DT_LANE = 3 * NSA_HEADS


def _cparams(*sem):
    return pltpu.CompilerParams(dimension_semantics=sem, vmem_limit_bytes=VMEM_LIMIT)


def _dot(a, b):
    return jnp.dot(a, b, preferred_element_type=F32)


def _dot_nt(a, b):
    return lax.dot_general(a, b, (((1,), (1,)), ((), ())), preferred_element_type=F32)


def _rms(x, g):
    return x * lax.rsqrt(jnp.mean(x * x, axis=-1, keepdims=True) + RMS_EPS) * g


def _sigmoid(x):
    return 1.0 / (1.0 + jnp.exp(-x))


def _alibi(n_heads):
    return [float(2.0 ** (-8.0 * i / n_heads)) for i in range(1, n_heads + 1)]


def _col(vals, rows):
    return jnp.concatenate([jnp.full((rows, 1), v, F32) for v in vals], axis=0)


def _row(vals, cols):
    return jnp.concatenate([jnp.full((1, cols), v, F32) for v in vals], axis=1)


def _pair_lanes(c0, c1):
    lane = lax.broadcasted_iota(jnp.int32, (c0.shape[0], LANE), 1)
    return jnp.where(lane < HEAD_DIM, c0, c1)


def _norm_proj_kernel(x_ref, g_ref, w_ref, *o_refs, pieces):
    hb = _rms(x_ref[...], g_ref[...]).astype(BF16)
    cache = {}
    for o_ref, (kind, off, n) in zip(o_refs, pieces):
        if (off, n) not in cache:
            cache[(off, n)] = _dot(hb, w_ref[:, off:off + n])
        y = cache[(off, n)]
        if kind == "row":
            o_ref[...] = y.astype(o_ref.dtype)
        else:
            if ("t", off, n) not in cache:
                cache[("t", off, n)] = y.T
            y_t = cache[("t", off, n)]
            if kind == "chan":
                o_ref[0] = y_t.astype(o_ref.dtype)
            else:
                for j in range(o_ref.shape[0]):
                    o_ref[j] = y_t[:, j * TQ:(j + 1) * TQ].astype(o_ref.dtype)


def norm_proj(x3d, g, w_bf, pieces, dtypes, tm=256):
    b, seq, d = x3d.shape
    m = b * seq
    n_tot = w_bf.shape[1]
    tm = min(tm, m)
    per_b = seq // tm if seq >= tm else 1
    out_specs, out_shape = [], []
    for (kind, _, n), dt in zip(pieces, dtypes):
        if kind == "row":
            out_specs.append(pl.BlockSpec((tm, n), lambda i: (i, 0)))
            out_shape.append(jax.ShapeDtypeStruct((m, n), dt))
        elif kind == "chan":
            out_specs.append(pl.BlockSpec((1, n, tm), lambda i: (i // per_b, 0, i % per_b)))
            out_shape.append(jax.ShapeDtypeStruct((b, n, seq), dt))
        else:
            out_specs.append(pl.BlockSpec((tm // TQ, n, TQ), lambda i: (i, 0, 0)))
            out_shape.append(jax.ShapeDtypeStruct((m // TQ, n, TQ), dt))
    return pl.pallas_call(
        functools.partial(_norm_proj_kernel, pieces=tuple(pieces)),
        grid=(m // tm,),
        in_specs=[pl.BlockSpec((tm, d), lambda i: (i, 0)),
                  pl.BlockSpec((1, d), lambda i: (0, 0)),
                  pl.BlockSpec((d, n_tot), lambda i: (0, 0))],
        out_specs=out_specs,
        out_shape=out_shape,
        compiler_params=_cparams("parallel"),
        name="norm_proj",
    )(x3d.reshape(m, d), g.reshape(1, d), w_bf)


def _proj_res_kernel(*refs, n_in):
    a_refs, w_refs = refs[:n_in], refs[n_in:2 * n_in]
    res_ref, o_ref = refs[2 * n_in], refs[2 * n_in + 1]
    acc = res_ref[...]
    for a_ref, w_ref in zip(a_refs, w_refs):
        acc = acc + _dot(a_ref[...].astype(BF16), w_ref[...])
    o_ref[...] = acc


def proj_res(a_list, w_list, res, tm=512):
    m, d = res.shape
    tm = min(tm, m)
    n_in = len(a_list)
    in_specs = ([pl.BlockSpec((tm, a.shape[1]), lambda i: (i, 0)) for a in a_list]
                + [pl.BlockSpec(w.shape, lambda i: (0, 0)) for w in w_list]
                + [pl.BlockSpec((tm, d), lambda i: (i, 0))])
    return pl.pallas_call(
        functools.partial(_proj_res_kernel, n_in=n_in),
        grid=(m // tm,),
        in_specs=in_specs,
        out_specs=pl.BlockSpec((tm, d), lambda i: (i, 0)),
        out_shape=jax.ShapeDtypeStruct((m, d), F32),
        compiler_params=_cparams("parallel"),
        name="proj_res",
    )(*a_list, *w_list, res)


def _ffn_kernel(x_ref, g_ref, wg_ref, wu_ref, wd_ref, *rest, final):
    if final:
        gf_ref, o_ref, h_scr, acc_scr = rest
    else:
        o_ref, h_scr, acc_scr = rest
    j = pl.program_id(1)

    @pl.when(j == 0)
    def _():
        x = x_ref[...]
        h_scr[...] = _rms(x, g_ref[...]).astype(BF16)
        acc_scr[...] = x

    hb = h_scr[...]
    gate = _dot(hb, wg_ref[...])
    up = _dot(hb, wu_ref[...])
    act = gate * _sigmoid(gate) * up
    acc_scr[...] += _dot(act.astype(BF16), wd_ref[...])

    @pl.when(j == pl.num_programs(1) - 1)
    def _():
        y = acc_scr[...]
        if final:
            y = _rms(y, gf_ref[...])
        o_ref[...] = y


def ffn(x2d, g, w_gu_bf, w_down_bf, g_final=None, tm=512):
    m, d = x2d.shape
    f = w_down_bf.shape[0]
    tm = min(tm, m)
    nf = 2 if (f // 2) % LANE == 0 else 1
    tf = f // nf
    final = g_final is not None
    in_specs = [pl.BlockSpec((tm, d), lambda i, j: (i, 0)),
                pl.BlockSpec((1, d), lambda i, j: (0, 0)),
                pl.BlockSpec((d, tf), lambda i, j: (0, j)),
                pl.BlockSpec((d, tf), lambda i, j: (0, j + nf)),
                pl.BlockSpec((tf, d), lambda i, j: (j, 0))]
    args = [x2d, g.reshape(1, d), w_gu_bf, w_gu_bf, w_down_bf]
    if final:
        in_specs.append(pl.BlockSpec((1, d), lambda i, j: (0, 0)))
        args.append(g_final.reshape(1, d))
    return pl.pallas_call(
        functools.partial(_ffn_kernel, final=final),
        grid=(m // tm, nf),
        in_specs=in_specs,
        out_specs=pl.BlockSpec((tm, d), lambda i, j: (i, 0)),
        out_shape=jax.ShapeDtypeStruct((m, d), F32),
        scratch_shapes=[pltpu.VMEM((tm, d), BF16), pltpu.VMEM((tm, d), F32)],
        compiler_params=_cparams("parallel", "arbitrary"),
        name="ffn",
    )(*args)


def _gelu_tanh(x):
    c = math.sqrt(2.0 / math.pi)
    return x * (0.5 * (1.0 + jnp.tanh(c * (x + 0.044715 * (x * x * x)))))


def _compress_kernel(x_ref, pe_ref, w1k_ref, w1v_ref, w2_ref, o_ref, *, n_pages):
    hidden = []
    for c in range(2 * NSA_KV_HEADS):
        rows = [x_ref[pl.ds(c * HEAD_DIM + d, n_pages, stride=KV_ROW), :] for d in range(HEAD_DIM)]
        xc = jnp.concatenate(rows, axis=1) + pe_ref[c:c + 1, :]
        w1_ref = w1k_ref if c < NSA_KV_HEADS else w1v_ref
        hidden.append(_gelu_tanh(_dot(xc.astype(BF16), w1_ref[...])).astype(BF16))
    o_ref[...] = _dot(jnp.concatenate(hidden, axis=1), w2_ref[...]).astype(o_ref.dtype)


def compress(pages2d, cw, pages_per_step=64):
    pe4, w1k, w1v, w2 = cw
    n_pages = pages2d.shape[0] // KV_ROW
    pps = math.gcd(pages_per_step, n_pages)
    blocks_per_page = TQ // CMP_BLOCK
    const = lambda a: pl.BlockSpec(a.shape, lambda i: (0, 0))
    return pl.pallas_call(
        functools.partial(_compress_kernel, n_pages=pps),
        grid=(n_pages // pps,),
        in_specs=[pl.BlockSpec((pps * KV_ROW, TQ), lambda i: (i, 0)), const(pe4), const(w1k), const(w1v), const(w2)],
        out_specs=pl.BlockSpec((pps, blocks_per_page * KV_ROW), lambda i: (i, 0)),
        out_shape=jax.ShapeDtypeStruct((n_pages, blocks_per_page * KV_ROW), BF16),
        compiler_params=_cparams("parallel"),
        name="compress",
    )(pages2d, pe4, w1k, w1v, w2)


def _compress_weights(pe_k, w1_k, w2_k, pe_v, w1_v, w2_v):
    nj = TQ // CMP_BLOCK
    nc = 2 * NSA_KV_HEADS
    l, d, h = w1_k.shape

    def first(w1):
        w = jnp.zeros((d, nj, l, nj, h), F32)
        for j in range(nj):
            w = w.at[:, j, :, j, :].set(jnp.transpose(w1, (1, 0, 2)))
        return w.reshape(d * nj * l, nj * h).astype(BF16)

    def pe_row(pe):
        return jnp.broadcast_to(pe.T[:, None, :], (d, nj, l)).reshape(1, d * nj * l)

    w2 = jnp.zeros((nc, nj, h, nj, nc, w2_k.shape[1]), F32)
    for c in range(nc):
        for j in range(nj):
            w2 = w2.at[c, j, :, j, c, :].set(w2_k if c < NSA_KV_HEADS else w2_v)
    pe4 = jnp.concatenate([pe_row(pe_k if c < NSA_KV_HEADS else pe_v) for c in range(nc)], axis=0)
    return pe4, first(w1_k), first(w1_v), w2.reshape(nc * nj * h, nj * nc * w2_k.shape[1]).astype(BF16)


def _flash_update_t(m_scr, l_scr, acc_scr, s, valid, v_t):
    s = jnp.where(valid, s, NEG)
    m_old = m_scr[...]
    m_new = jnp.maximum(m_old, jnp.max(s, axis=0, keepdims=True))
    alpha = jnp.exp(m_old - m_new)
    p = jnp.where(valid, jnp.exp(s - m_new), 0.0)
    l_scr[...] = alpha * l_scr[...] + jnp.sum(p, axis=0, keepdims=True)
    acc_scr[...] = alpha * acc_scr[...] + _dot(v_t, p.astype(BF16))
    m_scr[...] = m_new


def _flash_update(m_scr, l_scr, acc_scr, s, valid, x_t):
    s = jnp.where(valid, s, NEG)
    m_old = m_scr[...]
    m_new = jnp.maximum(m_old, jnp.max(s, axis=-1, keepdims=True))
    alpha = jnp.exp(m_old - m_new)
    p = jnp.where(valid, jnp.exp(s - m_new), 0.0)
    l_scr[...] = alpha * l_scr[...] + jnp.sum(p, axis=-1, keepdims=True)
    acc_scr[...] = alpha * acc_scr[...] + _dot_nt(p.astype(BF16), x_t)
    m_scr[...] = m_new


def _heads_to_rows(o_t, n_rep, pair):
    r0 = 2 * pair
    two = jnp.concatenate([o_t[:, r0 * TQ:(r0 + 1) * TQ], o_t[:, (r0 + 1) * TQ:(r0 + 2) * TQ]], axis=0)
    return two.T


def _banded_kernel(*refs, n_groups, n_rep, window, slopes, has_sink, kv_block):
    if has_sink:
        sink_ref, q_ref, x_ref, xt_ref, o_ref, m_scr, l_scr, acc_scr = refs
    else:
        q_ref, x_ref, xt_ref, o_ref, m_scr, l_scr, acc_scr = refs
    qt = pl.program_id(1)
    nrow = n_rep * TQ
    ki = lax.broadcasted_iota(jnp.int32, (TQ, nrow), 0)
    qi = lax.broadcasted_iota(jnp.int32, (TQ, nrow), 1) % TQ
    diff = qi - ki
    for g in range(n_groups):
        heads = [g * n_rep + r for r in range(n_rep)]
        qrows = jnp.concatenate([q_ref[0, :, h * QPAD:(h + 1) * QPAD] for h in heads], axis=0)
        slope_row = _row([slopes[h] for h in heads], TQ)
        bias0 = slope_row * diff.astype(F32)
        if has_sink:
            m_scr[...] = jnp.concatenate([jnp.full((1, TQ), sink_ref[h], F32) for h in heads], axis=1)
            l_scr[...] = jnp.ones_like(l_scr)
        else:
            m_scr[...] = jnp.full_like(m_scr, NEG)
            l_scr[...] = jnp.zeros_like(l_scr)
        acc_scr[...] = jnp.zeros_like(acc_scr)
        v0 = kv_block * KV_ROW + (NSA_KV_HEADS + g) * HEAD_DIM

        def body(kt, carry):
            x = x_ref[0, pl.ds(pl.multiple_of(kt * TQ, TQ), TQ), :]
            off = (qt - kt) * TQ
            s = _dot_nt(x, qrows) * SCALE - (bias0 + slope_row * off.astype(F32))
            dist = diff + off
            valid = (dist >= 0) & (dist <= window)
            _flash_update_t(m_scr, l_scr, acc_scr, s, valid, xt_ref[0, kt, v0:v0 + HEAD_DIM, :])
            return carry

        lax.fori_loop(jnp.maximum(qt - window // TQ, 0), qt + 1, body, 0)
        o_t = acc_scr[...] / jnp.maximum(l_scr[...], 1e-30)
        for pair in range(n_rep // 2):
            h0 = heads[2 * pair]
            o_ref[0, :, h0 * HEAD_DIM:(h0 + 2) * HEAD_DIM] = _heads_to_rows(o_t, n_rep, pair)


def banded_attention(q_pad, x_rows, x_pages, kv_block, n_groups, n_rep, window, sinks=None):
    b, seq = q_pad.shape[:2]
    n_heads = n_groups * n_rep
    has_sink = sinks is not None
    in_specs = [pl.BlockSpec((1, TQ, n_heads * QPAD), lambda i, j: (i, j, 0)),
                pl.BlockSpec((1, seq, KV_ROW), lambda i, j: (i, 0, kv_block)),
                pl.BlockSpec((1,) + x_pages.shape[1:], lambda i, j: (i, 0, 0, 0))]
    args = [q_pad, x_rows, x_pages]
    if has_sink:
        in_specs.insert(0, pl.BlockSpec(memory_space=pltpu.SMEM))
        args.insert(0, sinks)
    nrow = n_rep * TQ
    return pl.pallas_call(
        functools.partial(_banded_kernel, n_groups=n_groups, n_rep=n_rep, window=window,
                          slopes=_alibi(n_heads), has_sink=has_sink, kv_block=kv_block),
        grid=(b, seq // TQ),
        in_specs=in_specs,
        out_specs=pl.BlockSpec((1, TQ, n_heads * HEAD_DIM), lambda i, j: (i, j, 0)),
        out_shape=jax.ShapeDtypeStruct((b, seq, n_heads * HEAD_DIM), F32),
        scratch_shapes=[pltpu.VMEM((1, nrow), F32), pltpu.VMEM((1, nrow), F32),
                        pltpu.VMEM((HEAD_DIM, nrow), F32)],
        compiler_params=_cparams("parallel", "arbitrary"),
        name="banded_attention",
    )(*args)


def _banded_decode_kernel(*refs, n_groups, n_rep, window, slopes, has_sink, pos0):
    if has_sink:
        sink_ref, q_ref, st_ref, new_ref, o_ref = refs
    else:
        q_ref, st_ref, new_ref, o_ref = refs
    nq = q_ref.shape[1]
    npast = st_ref.shape[2]
    nrow = n_rep * nq
    xs_t = st_ref[0].astype(BF16)
    xn = jnp.concatenate([new_ref[0], jnp.zeros((TQ - nq, KV_ROW), F32)], axis=0).astype(BF16)
    qi_s = jnp.concatenate([lax.broadcasted_iota(jnp.int32, (nq, npast), 0)] * n_rep, axis=0)
    kj_s = lax.broadcasted_iota(jnp.int32, (nrow, npast), 1)
    dist_s = npast + qi_s - kj_s
    valid_s = (dist_s >= 0) & (dist_s <= window) & (pos0 - npast + kj_s >= 0)
    qi_n = jnp.concatenate([lax.broadcasted_iota(jnp.int32, (nq, TQ), 0)] * n_rep, axis=0)
    kj_n = lax.broadcasted_iota(jnp.int32, (nrow, TQ), 1)
    dist_n = qi_n - kj_n
    valid_n = (dist_n >= 0) & (dist_n <= window) & (kj_n < nq)
    for g in range(n_groups):
        heads = [g * n_rep + r for r in range(n_rep)]
        qrows = jnp.concatenate([q_ref[0, :, h * QPAD:(h + 1) * QPAD] for h in heads], axis=0).astype(BF16)
        slope_col = _col([slopes[h] for h in heads], nq)
        s_s = jnp.where(valid_s, _dot(qrows, xs_t) * SCALE - slope_col * dist_s.astype(F32), NEG)
        s_n = jnp.where(valid_n, _dot_nt(qrows, xn) * SCALE - slope_col * dist_n.astype(F32), NEG)
        m = jnp.maximum(jnp.max(s_s, axis=-1, keepdims=True), jnp.max(s_n, axis=-1, keepdims=True))
        if has_sink:
            sink_col = jnp.concatenate([jnp.full((nq, 1), sink_ref[h], F32) for h in heads], axis=0)
            m = jnp.maximum(m, sink_col)
        p_s = jnp.where(valid_s, jnp.exp(s_s - m), 0.0)
        p_n = jnp.where(valid_n, jnp.exp(s_n - m), 0.0)
        den = jnp.sum(p_s, axis=-1, keepdims=True) + jnp.sum(p_n, axis=-1, keepdims=True)
        if has_sink:
            den = den + jnp.exp(sink_col - m)
        o = (_dot_nt(p_s.astype(BF16), xs_t) + _dot(p_n.astype(BF16), xn)) / jnp.maximum(den, 1e-30)
        v0 = (NSA_KV_HEADS + g) * HEAD_DIM
        for r, h in enumerate(heads):
            o_ref[0, :, h * HEAD_DIM:(h + 1) * HEAD_DIM] = o[r * nq:(r + 1) * nq, v0:v0 + HEAD_DIM]


def banded_decode(q_pad, state_t, new_kv, n_groups, n_rep, window, pos0, sinks=None):
    b, nq = q_pad.shape[:2]
    npast = state_t.shape[2]
    n_heads = n_groups * n_rep
    has_sink = sinks is not None
    in_specs = [pl.BlockSpec((1, nq, n_heads * QPAD), lambda i: (i, 0, 0)),
                pl.BlockSpec((1, KV_ROW, npast), lambda i: (i, 0, 0)),
                pl.BlockSpec((1, nq, KV_ROW), lambda i: (i, 0, 0))]
    args = [q_pad, state_t, new_kv]
    if has_sink:
        in_specs.insert(0, pl.BlockSpec(memory_space=pltpu.SMEM))
        args.insert(0, sinks)
    return pl.pallas_call(
        functools.partial(_banded_decode_kernel, n_groups=n_groups, n_rep=n_rep, window=window,
                          slopes=_alibi(n_heads), has_sink=has_sink, pos0=pos0),
        grid=(b,),
        in_specs=in_specs,
        out_specs=pl.BlockSpec((1, nq, n_heads * HEAD_DIM), lambda i: (i, 0, 0)),
        out_shape=jax.ShapeDtypeStruct((b, nq, n_heads * HEAD_DIM), F32),
        compiler_params=_cparams("parallel"),
        name="banded_decode",
    )(*args)


def _select_blocks_t(imp, cur, n_sel):
    nb = imp.shape[0]
    nrow = lax.broadcasted_iota(jnp.int32, imp.shape, 0)
    count = jnp.zeros(imp.shape, F32)
    for i in range(nb):
        ci = imp[i:i + 1, :]
        ahead = jnp.where(ci > imp, 1.0, jnp.where(ci == imp, jnp.where(nrow > i, 1.0, 0.0), 0.0))
        count = count + jnp.where(cur > i, ahead, 0.0)
    return jnp.where(nrow < cur, jnp.where(count < n_sel - 1, 1.0, 0.0), jnp.where(nrow == cur, 1.0, 0.0))


def _nsa_prompt_kernel(q_ref, gd_ref, kc_ref, kct_ref, x_ref, xt_ref, ow_ref, o_ref, m_scr, l_scr, acc_scr,
                       *, n_sel, kv_block):
    qt = pl.program_id(1)
    n_rep = NSA_GROUP
    nrow = n_rep * TQ
    slopes = _alibi(NSA_HEADS)
    nb = kc_ref.shape[1]
    kcb = kc_ref[0]
    gates = _sigmoid(gd_ref[0])
    ki = lax.broadcasted_iota(jnp.int32, (TQ, nrow), 0)
    qi = lax.broadcasted_iota(jnp.int32, (TQ, nrow), 1) % TQ
    diff = qi - ki
    tpos = qt * TQ + lax.broadcasted_iota(jnp.int32, (nb, nrow), 1) % TQ
    blk = lax.broadcasted_iota(jnp.int32, (nb, nrow), 0)
    dist_c = tpos - ((blk + 1) * CMP_BLOCK - 1)
    valid_c = dist_c >= 0
    cur = (qt * TQ + lax.broadcasted_iota(jnp.int32, (nb, TQ), 1)) // CMP_BLOCK
    key_blk = lax.broadcasted_iota(jnp.int32, (TQ, nb), 0) // CMP_BLOCK
    blk_col = lax.broadcasted_iota(jnp.int32, (TQ, nb), 1)
    for g in range(NSA_KV_HEADS):
        heads = [g * n_rep + r for r in range(n_rep)]
        qrows = jnp.concatenate([q_ref[0, :, h * QPAD:(h + 1) * QPAD] for h in heads], axis=0)
        slope_row = _row([slopes[h] for h in heads], TQ)
        vrow = (NSA_KV_HEADS + g) * HEAD_DIM
        s_c = jnp.where(valid_c, _dot_nt(kcb, qrows) * SCALE - slope_row * dist_c.astype(F32), NEG)
        e_c = jnp.where(valid_c, jnp.exp(s_c - jnp.max(s_c, axis=0, keepdims=True)), 0.0)
        p_c = e_c / jnp.maximum(jnp.sum(e_c, axis=0, keepdims=True), 1e-30)
        ocmp_t = _dot(kct_ref[0, vrow:vrow + HEAD_DIM, :], p_c.astype(BF16))
        imp = p_c[:, 0:TQ]
        for r in range(1, n_rep):
            imp = imp + p_c[:, r * TQ:(r + 1) * TQ]
        taken = _select_blocks_t(imp, cur, n_sel).astype(BF16)
        bias0 = slope_row * diff.astype(F32)
        m_scr[...] = jnp.full_like(m_scr, NEG)
        l_scr[...] = jnp.zeros_like(l_scr)
        acc_scr[...] = jnp.zeros_like(acc_scr)
        v0 = kv_block * KV_ROW + vrow

        def body(kt, carry):
            x = x_ref[0, pl.ds(pl.multiple_of(kt * TQ, TQ), TQ), :]
            off = (qt - kt) * TQ
            s = _dot_nt(x, qrows) * SCALE - (bias0 + slope_row * off.astype(F32))
            expand = jnp.where(blk_col == key_blk + kt * (TQ // CMP_BLOCK), 1.0, 0.0).astype(BF16)
            hit = _dot(expand, taken)
            hit = jnp.concatenate([hit] * n_rep, axis=1)
            valid = jnp.where(diff + off >= 0, hit, 0.0) > 0.5
            _flash_update_t(m_scr, l_scr, acc_scr, s, valid, xt_ref[0, kt, v0:v0 + HEAD_DIM, :])
            return carry

        lax.fori_loop(0, qt + 1, body, 0)
        oslc_t = acc_scr[...] / jnp.maximum(l_scr[...], 1e-30)
        for pair in range(n_rep // 2):
            h0 = heads[2 * pair]
            hh = slice(h0 * HEAD_DIM, (h0 + 2) * HEAD_DIM)
            gate = lambda j: _pair_lanes(gates[:, 3 * h0 + j:3 * h0 + j + 1], gates[:, 3 * h0 + 3 + j:3 * h0 + 4 + j])
            o_ref[0, :, hh] = (gate(0) * _heads_to_rows(ocmp_t, n_rep, pair)
                               + gate(1) * _heads_to_rows(oslc_t, n_rep, pair)
                               + gate(2) * ow_ref[0, :, hh])


def nsa_prompt(q_pad, gd, kc, x_rows, x_pages, kv_block, o_win):
    b, seq = q_pad.shape[:2]
    nb = kc.shape[1]
    nrow = NSA_GROUP * TQ
    kct = jnp.swapaxes(kc, 1, 2)
    return pl.pallas_call(
        functools.partial(_nsa_prompt_kernel, n_sel=min(SEL_TOPK, nb), kv_block=kv_block),
        grid=(b, seq // TQ),
        in_specs=[pl.BlockSpec((1, TQ, NSA_HEADS * QPAD), lambda i, j: (i, j, 0)),
                  pl.BlockSpec((1, TQ, LANE), lambda i, j: (i, j, 0)),
                  pl.BlockSpec((1, nb, KV_ROW), lambda i, j: (i, 0, 0)),
                  pl.BlockSpec((1, KV_ROW, nb), lambda i, j: (i, 0, 0)),
                  pl.BlockSpec((1, seq, KV_ROW), lambda i, j: (i, 0, kv_block)),
                  pl.BlockSpec((1,) + x_pages.shape[1:], lambda i, j: (i, 0, 0, 0)),
                  pl.BlockSpec((1, TQ, NSA_WIDTH), lambda i, j: (i, j, 0))],
        out_specs=pl.BlockSpec((1, TQ, NSA_WIDTH), lambda i, j: (i, j, 0)),
        out_shape=jax.ShapeDtypeStruct((b, seq, NSA_WIDTH), F32),
        scratch_shapes=[pltpu.VMEM((1, nrow), F32), pltpu.VMEM((1, nrow), F32),
                        pltpu.VMEM((HEAD_DIM, nrow), F32)],
        compiler_params=_cparams("parallel", "arbitrary"),
        name="nsa_prompt",
    )(q_pad, gd, kc, kct, x_rows, x_pages, o_win)


def _select_blocks(imp, n_sel):
    nb = imp.shape[1]
    ncol = lax.broadcasted_iota(jnp.int32, imp.shape, 1)
    count = jnp.zeros(imp.shape, F32)
    for i in range(nb):
        ci = imp[:, i:i + 1]
        count = count + jnp.where(ci > imp, 1.0, jnp.where(ci == imp, jnp.where(ncol > i, 1.0, 0.0), 0.0))
    return jnp.where(count < n_sel - 1, 1.0, 0.0)


def _nsa_decode_kernel(pt_ref, q_ref, gd_ref, kc_ref, new_ref, ow_ref, *rest, pages_per_step, pos0, n_sel):
    page_refs = rest[:pages_per_step]
    o_ref, m_scr, l_scr, acc_scr, taken_scr, ocmp_scr = rest[pages_per_step:]
    p = pl.program_id(1)
    n_rep = NSA_GROUP
    nq = q_ref.shape[1]
    ngrp = n_rep * nq
    nrow = NSA_KV_HEADS * ngrp
    nb = kc_ref.shape[1]
    page = page_refs[0].shape[1]
    nkey = pages_per_step * page
    slopes = _alibi(NSA_HEADS)
    qrows = jnp.concatenate([q_ref[0, :, h * QPAD:(h + 1) * QPAD] for h in range(NSA_HEADS)], axis=0).astype(BF16)
    slope_col = _col(slopes, nq)

    def per_query(a):
        return jnp.concatenate([a[g] for g in range(NSA_KV_HEADS) for _ in range(n_rep)], axis=0)

    @pl.when(p == 0)
    def _():
        kcb = kc_ref[0]
        tpos = pos0 + jnp.concatenate([lax.broadcasted_iota(jnp.int32, (nq, nb), 0)] * (nrow // nq), axis=0)
        ncol = lax.broadcasted_iota(jnp.int32, (nrow, nb), 1)
        dist = tpos - ((ncol + 1) * CMP_BLOCK - 1)
        valid = dist >= 0
        s = jnp.where(valid, _dot_nt(qrows, kcb) * SCALE - slope_col * dist.astype(F32), NEG)
        e = jnp.where(valid, jnp.exp(s - jnp.max(s, axis=-1, keepdims=True)), 0.0)
        pc = e / jnp.maximum(jnp.sum(e, axis=-1, keepdims=True), 1e-30)
        ocmp_scr[...] = _dot(pc.astype(BF16), kcb)
        for g in range(NSA_KV_HEADS):
            imp = pc[g * ngrp:g * ngrp + nq]
            for r in range(1, n_rep):
                imp = imp + pc[g * ngrp + r * nq:g * ngrp + (r + 1) * nq]
            taken_scr[g * nq:(g + 1) * nq, :] = _select_blocks(imp, n_sel)
        m_scr[...] = jnp.full_like(m_scr, NEG)
        l_scr[...] = jnp.zeros_like(l_scr)
        acc_scr[...] = jnp.zeros_like(acc_scr)

    x_t = jnp.concatenate([r[...] for r in page_refs], axis=1).astype(BF16)
    key0 = p * nkey
    qi = jnp.concatenate([lax.broadcasted_iota(jnp.int32, (nq, nkey), 0)] * (nrow // nq), axis=0)
    kj = lax.broadcasted_iota(jnp.int32, (nrow, nkey), 1)
    dist = pos0 + qi - (key0 + kj)
    blk_row = lax.broadcasted_iota(jnp.int32, (nb, nkey), 0)
    key_blk = (key0 + lax.broadcasted_iota(jnp.int32, (nb, nkey), 1)) // CMP_BLOCK
    expand = jnp.where(blk_row == key_blk, 1.0, 0.0).astype(BF16)
    hit = _dot(taken_scr[...].astype(BF16), expand)
    hit = per_query([hit[g * nq:(g + 1) * nq] for g in range(NSA_KV_HEADS)])
    s = _dot(qrows, x_t) * SCALE - slope_col * dist.astype(F32)
    valid = jnp.where(dist >= 0, hit, 0.0) > 0.5
    _flash_update(m_scr, l_scr, acc_scr, s, valid, x_t)

    @pl.when(p == pl.num_programs(1) - 1)
    def _():
        xn = jnp.concatenate([new_ref[0], jnp.zeros((TQ - nq, KV_ROW), F32)], axis=0).astype(BF16)
        qi_n = jnp.concatenate([lax.broadcasted_iota(jnp.int32, (nq, TQ), 0)] * (nrow // nq), axis=0)
        kj_n = lax.broadcasted_iota(jnp.int32, (nrow, TQ), 1)
        dist_n = qi_n - kj_n
        valid_n = (dist_n >= 0) & (kj_n < nq)
        s_n = jnp.where(valid_n, _dot_nt(qrows, xn) * SCALE - slope_col * dist_n.astype(F32), NEG)
        m_old = m_scr[...]
        m_new = jnp.maximum(m_old, jnp.max(s_n, axis=-1, keepdims=True))
        alpha = jnp.exp(m_old - m_new)
        p_n = jnp.where(valid_n, jnp.exp(s_n - m_new), 0.0)
        den = alpha * l_scr[...] + jnp.sum(p_n, axis=-1, keepdims=True)
        o_slc = (alpha * acc_scr[...] + _dot(p_n.astype(BF16), xn)) / jnp.maximum(den, 1e-30)
        o_cmp = ocmp_scr[...]
        gates = _sigmoid(gd_ref[0])
        for h in range(NSA_HEADS):
            g = h // n_rep
            v0 = (NSA_KV_HEADS + g) * HEAD_DIM
            rr = slice(h * nq, (h + 1) * nq)
            hh = slice(h * HEAD_DIM, (h + 1) * HEAD_DIM)
            o_ref[0, :, hh] = (gates[:, 3 * h:3 * h + 1] * o_cmp[rr, v0:v0 + HEAD_DIM]
                               + gates[:, 3 * h + 1:3 * h + 2] * o_slc[rr, v0:v0 + HEAD_DIM]
                               + gates[:, 3 * h + 2:3 * h + 3] * ow_ref[0, :, hh])


def nsa_decode(page_table, q_pad, gd, kc, new_kv, o_win, pool_t, pos0, pages_per_step=8):
    b, nq = q_pad.shape[:2]
    nb = kc.shape[1]
    n_pages = page_table.shape[1]
    page = pool_t.shape[2]
    pps = math.gcd(pages_per_step, n_pages)
    nrow = NSA_HEADS * nq

    def page_spec(j):
        return pl.BlockSpec((None, KV_ROW, page), lambda i, p, pt: (pt[i, p * pps + j], 0, 0))

    grid_spec = pltpu.PrefetchScalarGridSpec(
        num_scalar_prefetch=1,
        grid=(b, n_pages // pps),
        in_specs=[pl.BlockSpec((1, nq, NSA_HEADS * QPAD), lambda i, p, pt: (i, 0, 0)),
                  pl.BlockSpec((1, nq, LANE), lambda i, p, pt: (i, 0, 0)),
                  pl.BlockSpec((1, nb, KV_ROW), lambda i, p, pt: (i, 0, 0)),
                  pl.BlockSpec((1, nq, KV_ROW), lambda i, p, pt: (i, 0, 0)),
                  pl.BlockSpec((1, nq, NSA_WIDTH), lambda i, p, pt: (i, 0, 0))]
                 + [page_spec(j) for j in range(pps)],
        out_specs=pl.BlockSpec((1, nq, NSA_WIDTH), lambda i, p, pt: (i, 0, 0)),
        scratch_shapes=[pltpu.VMEM((nrow, 1), F32), pltpu.VMEM((nrow, 1), F32),
                        pltpu.VMEM((nrow, KV_ROW), F32),
                        pltpu.VMEM((NSA_KV_HEADS * nq, nb), F32),
                        pltpu.VMEM((nrow, KV_ROW), F32)])
    return pl.pallas_call(
        functools.partial(_nsa_decode_kernel, pages_per_step=pps, pos0=pos0, n_sel=min(SEL_TOPK, nb + 1)),
        grid_spec=grid_spec,
        out_shape=jax.ShapeDtypeStruct((b, nq, NSA_WIDTH), F32),
        compiler_params=_cparams("parallel", "arbitrary"),
        name="nsa_decode",
    )(page_table, q_pad, gd, kc, new_kv, o_win, *([pool_t] * pps))


def _cumsum_rows(v):
    n = v.shape[0]
    ri = lax.broadcasted_iota(jnp.int32, v.shape, 0)
    sh = 1
    while sh < n:
        v = v + jnp.where(ri >= sh, pltpu.roll(v, sh, 0), 0.0)
        sh *= 2
    return v


def _ssd_kernel(z_ref, xbc_ref, gd_ref, tail_ref, s0_ref, cw_ref, cb_ref, dtb_ref, alog_ref, dsk_ref, ng_ref,
                y_ref, sfin_ref, xp_scr, st_scr, y_scr, *, n_valid):
    c = pl.program_id(1)
    q = xbc_ref.shape[1]

    @pl.when(c == 0)
    def _():
        xp_scr[0:SUBLANE, :] = tail_ref[0]
        st_scr[...] = s0_ref[0]

    xp_scr[SUBLANE:SUBLANE + q, :] = xbc_ref[0]
    acc = cb_ref[...]
    for k in range(SSM_CONV):
        lo = SUBLANE - (SSM_CONV - 1) + k
        acc = acc + xp_scr[lo:lo + q, :] * cw_ref[k:k + 1, :]
    nxt = xp_scr[q:q + SUBLANE, :]
    xp_scr[0:SUBLANE, :] = nxt
    act = acc * _sigmoid(acc)

    raw = gd_ref[0] + dtb_ref[...]
    dt = jnp.maximum(raw, 0.0) + jnp.log1p(jnp.exp(-jnp.abs(raw)))
    if n_valid < q:
        dt = jnp.where(lax.broadcasted_iota(jnp.int32, dt.shape, 0) < n_valid, dt, 0.0)
    acum = _cumsum_rows(dt * (-jnp.exp(alog_ref[...])))
    acum_t = acum.T
    ri = lax.broadcasted_iota(jnp.int32, (q, q), 0)
    ci = lax.broadcasted_iota(jnp.int32, (q, q), 1)
    causal = ri >= ci
    half = lax.broadcasted_iota(jnp.int32, (2 * SSM_HEAD_DIM, 1), 0) < SSM_HEAD_DIM
    hpg = SSM_HEADS // SSM_GROUPS
    for pair in range(SSM_HEADS // 2):
        grp = (2 * pair) // hpg
        lanes = slice(pair * LANE, (pair + 1) * LANE)
        bm = act[:, SSM_D_INNER + grp * SSM_STATE:SSM_D_INNER + (grp + 1) * SSM_STATE].astype(BF16)
        cm_lo = SSM_D_INNER + SSM_GROUPS * SSM_STATE + grp * SSM_STATE
        cm = act[:, cm_lo:cm_lo + SSM_STATE].astype(BF16)
        cb = _dot_nt(cm, bm)
        xs = act[:, lanes]
        cols = []
        for h in (2 * pair, 2 * pair + 1):
            ln = DT_LANE + h
            cols.append((acum[:, ln:ln + 1], acum_t[ln:ln + 1, :], acum[q - 1:q, ln:ln + 1], dt[:, ln:ln + 1]))
        xdt = xs * _pair_lanes(cols[0][3], cols[1][3])
        xdt_bf = xdt.astype(BF16)
        y_parts = []
        for a_col, a_row, _, _ in cols:
            seg = a_col - a_row
            lmat = jnp.where(causal, jnp.exp(jnp.where(causal, seg, 0.0)), 0.0)
            y_parts.append(_dot((cb * lmat).astype(BF16), xdt_bf))
        lane = lax.broadcasted_iota(jnp.int32, (q, LANE), 1)
        y_diag = jnp.where(lane < SSM_HEAD_DIM, y_parts[0], y_parts[1])
        st = st_scr[pair * LANE:(pair + 1) * LANE, :]
        y_off = _dot_nt(cm, st.astype(BF16)) * _pair_lanes(jnp.exp(cols[0][0]), jnp.exp(cols[1][0]))
        dec_end = _pair_lanes(jnp.exp(cols[0][2] - cols[0][0]), jnp.exp(cols[1][2] - cols[1][0]))
        cs = _dot((xdt * dec_end).T.astype(BF16), bm)
        st_scr[pair * LANE:(pair + 1) * LANE, :] = st * jnp.where(half, jnp.exp(cols[0][2]), jnp.exp(cols[1][2])) + cs
        dsk = _pair_lanes(jnp.zeros((q, 1), F32) + dsk_ref[:, 2 * pair:2 * pair + 1],
                          jnp.zeros((q, 1), F32) + dsk_ref[:, 2 * pair + 1:2 * pair + 2])
        y_scr[:, lanes] = y_diag + y_off + dsk * xs

    z = z_ref[0]
    y_ref[0] = _rms(y_scr[...] * (z * _sigmoid(z)), ng_ref[...])

    @pl.when(c == pl.num_programs(1) - 1)
    def _():
        sfin_ref[0] = st_scr[...]


def ssd_mixer(z, xbc, gd, tail8, s0, conv_w, conv_b, dt_bias, a_log, d_skip, norm_g, n_valid):
    b, seq = z.shape[:2]
    nst = SSM_HEADS * SSM_HEAD_DIM

    def lane_piece(v):
        return jnp.zeros((1, LANE), F32).at[0, DT_LANE:DT_LANE + SSM_HEADS].set(v)

    alog_piece = jnp.full((1, LANE), -100.0, F32).at[0, DT_LANE:DT_LANE + SSM_HEADS].set(a_log)
    full2 = lambda shape: pl.BlockSpec(shape, lambda i, c: (0, 0))
    return pl.pallas_call(
        functools.partial(_ssd_kernel, n_valid=n_valid),
        grid=(b, seq // TQ),
        in_specs=[pl.BlockSpec((1, TQ, SSM_D_INNER), lambda i, c: (i, c, 0)),
                  pl.BlockSpec((1, TQ, SSM_CONV_DIM), lambda i, c: (i, c, 0)),
                  pl.BlockSpec((1, TQ, LANE), lambda i, c: (i, c, 0)),
                  pl.BlockSpec((1, SUBLANE, SSM_CONV_DIM), lambda i, c: (i, 0, 0)),
                  pl.BlockSpec((1, nst, SSM_STATE), lambda i, c: (i, 0, 0)),
                  full2((SSM_CONV, SSM_CONV_DIM)), full2((1, SSM_CONV_DIM)), full2((1, LANE)),
                  full2((1, LANE)), full2((1, SSM_HEADS)), full2((1, SSM_D_INNER))],
        out_specs=[pl.BlockSpec((1, TQ, SSM_D_INNER), lambda i, c: (i, c, 0)),
                   pl.BlockSpec((1, nst, SSM_STATE), lambda i, c: (i, 0, 0))],
        out_shape=[jax.ShapeDtypeStruct((b, seq, SSM_D_INNER), F32),
                   jax.ShapeDtypeStruct((b, nst, SSM_STATE), F32)],
        scratch_shapes=[pltpu.VMEM((SUBLANE + TQ, SSM_CONV_DIM), F32),
                        pltpu.VMEM((nst, SSM_STATE), F32),
                        pltpu.VMEM((TQ, SSM_D_INNER), F32)],
        compiler_params=_cparams("parallel", "arbitrary"),
        name="ssd_mixer",
    )(z, xbc, gd, tail8, s0, conv_w, conv_b.reshape(1, -1), lane_piece(dt_bias), alog_piece,
      d_skip.reshape(1, -1), norm_g.reshape(1, -1))


def _pad_q_cols(w_q, n_heads, n_rep):
    d = w_q.shape[0]
    out = jnp.zeros((d, n_heads, QPAD), w_q.dtype)
    wq = w_q.reshape(d, n_heads, HEAD_DIM)
    for h in range(n_heads):
        g = h // n_rep
        out = out.at[:, h, g * HEAD_DIM:(g + 1) * HEAD_DIM].set(wq[:, h])
    return out.reshape(d, n_heads * QPAD)


def _channel_major(a6):
    b, rows = a6.shape[:2]
    return jnp.transpose(a6, (0, 2, 3, 4, 1)).reshape(b, KV_ROW, rows)


def _token_major(a_t):
    b, _, rows = a_t.shape
    return jnp.transpose(a_t.reshape(b, 2, NSA_KV_HEADS, HEAD_DIM, rows), (0, 4, 1, 2, 3))


def _even_weights(w_in):
    d = w_in.shape[0]
    o = np.cumsum([0, NSA_WIDTH, KV_ROW, KV_ROW, KV_ROW, 3 * NSA_HEADS, SSM_D_INNER, SSM_CONV_DIM, SSM_HEADS])
    q, kvc, kvs, kvw, gt, z, xbc, dtr = (w_in[:, o[i]:o[i + 1]] for i in range(8))
    gd = jnp.zeros((d, LANE), F32).at[:, :3 * NSA_HEADS].set(gt).at[:, DT_LANE:DT_LANE + SSM_HEADS].set(dtr)
    w = jnp.concatenate([_pad_q_cols(q, NSA_HEADS, NSA_GROUP), kvc, kvs, kvw, z, xbc, gd], axis=1).astype(BF16)
    qw = NSA_HEADS * QPAD
    off = {"q": (0, qw), "kvc": (qw, KV_ROW), "kvs": (qw + KV_ROW, KV_ROW), "kvw": (qw + 2 * KV_ROW, KV_ROW),
           "kv3": (qw, 3 * KV_ROW), "z": (qw + 3 * KV_ROW, SSM_D_INNER),
           "xbc": (qw + 3 * KV_ROW + SSM_D_INNER, SSM_CONV_DIM),
           "gd": (qw + 3 * KV_ROW + SSM_D_INNER + SSM_CONV_DIM, LANE)}
    return w, off


def _even_layer(x, past, p, cache):
    (norm_g, w_in, w_out, cw, ssm_p) = p
    b, seq, d = x.shape
    m = b * seq
    decode = cache is not None
    w, off = _even_weights(w_in)
    conv_w, conv_b, dt_bias, a_log, d_skip, ssm_norm = ssm_p
    if not decode:
        names = [("row", "q", BF16), ("chan", "kvc", F32), ("chan", "kvs", F32), ("chan", "kvw", F32),
                 ("page", "kvc", F32), ("row", "kv3", BF16), ("page", "kv3", BF16),
                 ("row", "z", F32), ("row", "xbc", F32), ("row", "gd", F32)]
        q, kvc_t, kvs_t, kvw_t, kvc_pages, x_rows, x_pages, z, xbc, gd = norm_proj(
            x, norm_g, w, [(k,) + off[n] for k, n, _ in names], [dt for _, _, dt in names])
        q = q.reshape(b, seq, -1)
        gd3 = gd.reshape(b, seq, LANE)
        x_rows = x_rows.reshape(b, seq, 3 * KV_ROW)
        x_pages = x_pages.reshape(b, seq // TQ, 3 * KV_ROW, TQ)
        kc = compress(kvc_pages.reshape(-1, TQ), cw).reshape(b, seq // CMP_BLOCK, KV_ROW)
        o_win = banded_attention(q, x_rows, x_pages, 2, NSA_KV_HEADS, NSA_GROUP, NSA_WINDOW)
        o_nsa = nsa_prompt(q, gd3, kc, x_rows, x_pages, 1, o_win)
        tail8 = jnp.zeros((b, SUBLANE, SSM_CONV_DIM), F32)
        s0 = jnp.zeros((b, SSM_D_INNER, SSM_STATE), F32)
        xbc3 = xbc.reshape(b, seq, -1)
        o_ssm, s_fin = ssd_mixer(z.reshape(b, seq, -1), xbc3, gd3, tail8, s0,
                                 conv_w, conv_b, dt_bias, a_log, d_skip, ssm_norm, TQ)
        conv_new = xbc3[:, seq - (SSM_CONV - 1):]
        kv_outs = (_token_major(kvc_t), _token_major(kvs_t), _token_major(kvw_t[:, :, seq - min(NSA_WINDOW, seq):]))
    else:
        cmp_pool_t, slc_pool_t, page_table, win_state_t, conv0, ssm0 = cache
        n_pool, _, page = cmp_pool_t.shape
        assert page == TQ and seq < CMP_BLOCK and TQ % seq == 0
        names = [("row", "q", F32), ("row", "kvc", F32), ("row", "kvs", F32), ("row", "kvw", F32),
                 ("row", "z", F32), ("row", "xbc", F32), ("row", "gd", F32)]
        q, kvc, kvs, kvw, z, xbc, gd = norm_proj(
            x, norm_g, w, [(k,) + off[n] for k, n, _ in names], [dt for _, _, dt in names])
        q = q.reshape(b, seq, -1)
        gd3 = gd.reshape(b, seq, LANE)
        bpp = page // CMP_BLOCK
        kc_pool = compress(cmp_pool_t.reshape(n_pool * KV_ROW, page), cw)
        kc = kc_pool.reshape(n_pool, bpp, KV_ROW)[page_table].reshape(b, past // CMP_BLOCK, KV_ROW)
        kvw3 = kvw.reshape(b, seq, KV_ROW)
        o_win = banded_decode(q, win_state_t, kvw3, NSA_KV_HEADS, NSA_GROUP, NSA_WINDOW, past)
        o_nsa = nsa_decode(page_table, q, gd3, kc, kvs.reshape(b, seq, KV_ROW), o_win, slc_pool_t, past)
        padr = lambda a: jnp.pad(a.reshape(b, seq, -1), ((0, 0), (0, TQ - seq), (0, 0)))
        tail8 = jnp.pad(conv0, ((0, 0), (SUBLANE - (SSM_CONV - 1), 0), (0, 0)))
        o_ssm, s_fin = ssd_mixer(padr(z), padr(xbc), padr(gd), tail8, ssm0.reshape(b, SSM_D_INNER, SSM_STATE),
                                 conv_w, conv_b, dt_bias, a_log, d_skip, ssm_norm, seq)
        o_ssm = o_ssm[:, :seq]
        conv_new = jnp.concatenate([conv0, xbc.reshape(b, seq, -1)], axis=1)[:, -(SSM_CONV - 1):]
        npast = win_state_t.shape[2]
        win_out = jnp.concatenate([win_state_t, jnp.swapaxes(kvw3, 1, 2)], axis=2)[:, :, -npast:]
        kv6 = (b, seq, 2, NSA_KV_HEADS, HEAD_DIM)
        kv_outs = (kvc.reshape(kv6), kvs.reshape(kv6), _token_major(win_out))
    x = proj_res([o_nsa.reshape(m, -1), o_ssm.reshape(m, -1)],
                 [w_out[:NSA_WIDTH].astype(BF16), w_out[NSA_WIDTH:].astype(BF16)], x.reshape(m, d))
    outs = kv_outs + (s_fin.reshape(b, SSM_HEADS, SSM_HEAD_DIM, SSM_STATE), conv_new)
    return x.reshape(b, seq, d), outs


def _odd_layer(x, past, p, swa_state_t):
    norm_g, w_in, w_out, sinks = p
    b, seq, d = x.shape
    m = b * seq
    decode = swa_state_t is not None
    w = jnp.concatenate([_pad_q_cols(w_in[:, :SWA_WIDTH], SWA_HEADS, SWA_GROUP), w_in[:, SWA_WIDTH:]], axis=1).astype(BF16)
    qw = SWA_HEADS * QPAD
    if not decode:
        q, kv_t, x_rows, x_pages = norm_proj(
            x, norm_g, w, [("row", 0, qw), ("chan", qw, KV_ROW), ("row", qw, KV_ROW), ("page", qw, KV_ROW)],
            [BF16, F32, BF16, BF16])
        o = banded_attention(q.reshape(b, seq, -1), x_rows.reshape(b, seq, KV_ROW),
                             x_pages.reshape(b, seq // TQ, KV_ROW, TQ), 0, SWA_KV_HEADS, SWA_GROUP, SWA_WINDOW, sinks)
        kv_out = _token_major(kv_t[:, :, seq - min(SWA_WINDOW, seq):])
    else:
        q, kv = norm_proj(x, norm_g, w, [("row", 0, qw), ("row", qw, KV_ROW)], [F32, F32])
        kv3 = kv.reshape(b, seq, KV_ROW)
        o = banded_decode(q.reshape(b, seq, -1), swa_state_t, kv3, SWA_KV_HEADS, SWA_GROUP, SWA_WINDOW, past, sinks)
        npast = swa_state_t.shape[2]
        kv_out = _token_major(jnp.concatenate([swa_state_t, jnp.swapaxes(kv3, 1, 2)], axis=2)[:, :, -npast:])
    x = proj_res([o.reshape(m, -1)], [w_out.astype(BF16)], x.reshape(m, d))
    return x.reshape(b, seq, d), kv_out


def kernel(x_prompt, x_sample, cache_nsa_cmp_kv, cache_nsa_slc_kv, state_nsa_win_kv, state_ssm, state_ssm_conv, state_swa_kv, page_table, norm_mix, norm_ffn, norm_final, w_in_even, w_out_even, cmp_pe_k, cmp_w1_k, cmp_w2_k, cmp_pe_v, cmp_w1_v, cmp_w2_v, ssm_conv_w, ssm_conv_b, ssm_dt_bias, ssm_a_log, ssm_d, ssm_norm, w_in_odd, w_out_odd, swa_sinks, w_gate_up, w_down):
    depth = norm_mix.shape[0]
    page = cache_nsa_cmp_kv.shape[2]
    past = page_table.shape[1] * page
    xp, xs = x_prompt, x_sample
    outs_p = [[] for _ in range(6)]
    outs_s = [[] for _ in range(6)]
    for layer in range(depth):
        if layer % 2 == 0:
            e = layer // 2
            cw = _compress_weights(cmp_pe_k[e], cmp_w1_k[e], cmp_w2_k[e], cmp_pe_v[e], cmp_w1_v[e], cmp_w2_v[e])
            ssm_p = (ssm_conv_w[e], ssm_conv_b[e], ssm_dt_bias[e], ssm_a_log[e], ssm_d[e], ssm_norm[e])
            p = (norm_mix[layer], w_in_even[e], w_out_even[e], cw, ssm_p)
            xp, o = _even_layer(xp, 0, p, None)
            for lst, v in zip(outs_p[:5], o):
                lst.append(v)
            cache = (_channel_major(cache_nsa_cmp_kv[e]), _channel_major(cache_nsa_slc_kv[e]), page_table,
                     _channel_major(state_nsa_win_kv[e]), state_ssm_conv[e], state_ssm[e])
            xs, o = _even_layer(xs, past, p, cache)
            for lst, v in zip(outs_s[:5], o):
                lst.append(v)
        else:
            o_idx = layer // 2
            p = (norm_mix[layer], w_in_odd[o_idx], w_out_odd[o_idx], swa_sinks[o_idx])
            xp, kw = _odd_layer(xp, 0, p, None)
            outs_p[5].append(kw)
            xs, kw = _odd_layer(xs, past, p, _channel_major(state_swa_kv[o_idx]))
            outs_s[5].append(kw)
        gf = norm_final if layer == depth - 1 else None
        wgu, wd = w_gate_up[layer].astype(BF16), w_down[layer].astype(BF16)
        xp = ffn(xp.reshape(-1, xp.shape[-1]), norm_ffn[layer], wgu, wd, gf).reshape(xp.shape)
        xs = ffn(xs.reshape(-1, xs.shape[-1]), norm_ffn[layer], wgu, wd, gf).reshape(xs.shape)
    return (xp, xs) + tuple(jnp.stack(v) for v in outs_p) + tuple(jnp.stack(v) for v in outs_s)
```

```python
import functools
import math

import numpy as np
import jax
import jax.numpy as jnp
from jax import lax
from jax.experimental import pallas as pl
from jax.experimental.pallas import tpu as pltpu

F32 = jnp.float32
BF16 = jnp.bfloat16

HEAD_DIM = 64
NSA_HEADS = 8
NSA_KV_HEADS = 2
NSA_GROUP = NSA_HEADS // NSA_KV_HEADS
CMP_BLOCK = 64
SEL_TOPK = 16
NSA_WINDOW = 512
SSM_HEADS = 8
SSM_HEAD_DIM = 64
SSM_D_INNER = SSM_HEADS * SSM_HEAD_DIM
SSM_GROUPS = 2
SSM_STATE = 128
SSM_CONV = 4
SSM_CONV_DIM = SSM_D_INNER + 2 * SSM_GROUPS * SSM_STATE
SWA_HEADS = 16
SWA_KV_HEADS = 2
SWA_GROUP = SWA_HEADS // SWA_KV_HEADS
SWA_WINDOW = 128
RMS_EPS = 1e-6
NSA_WIDTH = NSA_HEADS * HEAD_DIM
SWA_WIDTH = SWA_HEADS * HEAD_DIM
SCALE = HEAD_DIM ** -0.5

LANE = 128
SUBLANE = 8
KV_ROW = 2 * NSA_KV_HEADS * HEAD_DIM
QPAD = KV_ROW
TQ = 128
NEG = -1e30
MASK = -2e30
VMEM_LIMIT = 56 * 1024 * 1024
DT_LANE = 3 * NSA_HEADS


def _cparams(*sem):
    return pltpu.CompilerParams(dimension_semantics=sem, vmem_limit_bytes=VMEM_LIMIT)


def _dot(a, b):
    return jnp.dot(a, b, preferred_element_type=F32)


def _dot_nt(a, b):
    return lax.dot_general(a, b, (((1,), (1,)), ((), ())), preferred_element_type=F32)


def _rms(x, g):
    return x * lax.rsqrt(jnp.mean(x * x, axis=-1, keepdims=True) + RMS_EPS) * g


def _sigmoid(x):
    return 1.0 / (1.0 + jnp.exp(-x))


def _alibi(n_heads):
    return [float(2.0 ** (-8.0 * i / n_heads)) for i in range(1, n_heads + 1)]


def _col(vals, rows):
    return jnp.concatenate([jnp.full((rows, 1), v, F32) for v in vals], axis=0)


def _row(vals, cols):
    return jnp.concatenate([jnp.full((1, cols), v, F32) for v in vals], axis=1)


def _pair_lanes(c0, c1):
    lane = lax.broadcasted_iota(jnp.int32, (c0.shape[0], LANE), 1)
    return jnp.where(lane < HEAD_DIM, c0, c1)


def _norm_proj_kernel(x_ref, g_ref, w_ref, *o_refs, pieces):
    hb = _rms(x_ref[...], g_ref[...]).astype(BF16)
    cache = {}
    for o_ref, (kind, off, n) in zip(o_refs, pieces):
        if (off, n) not in cache:
            cache[(off, n)] = _dot(hb, w_ref[:, off:off + n])
        y = cache[(off, n)]
        if kind == "row":
            o_ref[...] = y.astype(o_ref.dtype)
        else:
            if ("t", off, n) not in cache:
                cache[("t", off, n)] = y.T
            y_t = cache[("t", off, n)]
            if kind == "chan":
                o_ref[0] = y_t.astype(o_ref.dtype)
            else:
                for j in range(o_ref.shape[0]):
                    o_ref[j] = y_t[:, j * TQ:(j + 1) * TQ].astype(o_ref.dtype)


def norm_proj(x3d, g, w_bf, pieces, dtypes, tm=256):
    b, seq, d = x3d.shape
    m = b * seq
    n_tot = w_bf.shape[1]
    tm = min(tm, m)
    per_b = seq // tm if seq >= tm else 1
    out_specs, out_shape = [], []
    for (kind, _, n), dt in zip(pieces, dtypes):
        if kind == "row":
            out_specs.append(pl.BlockSpec((tm, n), lambda i: (i, 0)))
            out_shape.append(jax.ShapeDtypeStruct((m, n), dt))
        elif kind == "chan":
            out_specs.append(pl.BlockSpec((1, n, tm), lambda i: (i // per_b, 0, i % per_b)))
            out_shape.append(jax.ShapeDtypeStruct((b, n, seq), dt))
        else:
            out_specs.append(pl.BlockSpec((tm // TQ, n, TQ), lambda i: (i, 0, 0)))
            out_shape.append(jax.ShapeDtypeStruct((m // TQ, n, TQ), dt))
    return pl.pallas_call(
        functools.partial(_norm_proj_kernel, pieces=tuple(pieces)),
        grid=(m // tm,),
        in_specs=[pl.BlockSpec((tm, d), lambda i: (i, 0)),
                  pl.BlockSpec((1, d), lambda i: (0, 0)),
                  pl.BlockSpec((d, n_tot), lambda i: (0, 0))],
        out_specs=out_specs,
        out_shape=out_shape,
        compiler_params=_cparams("parallel"),
        name="norm_proj",
    )(x3d.reshape(m, d), g.reshape(1, d), w_bf)


def _proj_res_kernel(*refs, n_in):
    a_refs, w_refs = refs[:n_in], refs[n_in:2 * n_in]
    res_ref, o_ref = refs[2 * n_in], refs[2 * n_in + 1]
    acc = res_ref[...]
    for a_ref, w_ref in zip(a_refs, w_refs):
        acc = acc + _dot(a_ref[...].astype(BF16), w_ref[...])
    o_ref[...] = acc


def proj_res(a_list, w_list, res, tm=512):
    m, d = res.shape
    tm = min(tm, m)
    n_in = len(a_list)
    in_specs = ([pl.BlockSpec((tm, a.shape[1]), lambda i: (i, 0)) for a in a_list]
                + [pl.BlockSpec(w.shape, lambda i: (0, 0)) for w in w_list]
                + [pl.BlockSpec((tm, d), lambda i: (i, 0))])
    return pl.pallas_call(
        functools.partial(_proj_res_kernel, n_in=n_in),
        grid=(m // tm,),
        in_specs=in_specs,
        out_specs=pl.BlockSpec((tm, d), lambda i: (i, 0)),
        out_shape=jax.ShapeDtypeStruct((m, d), F32),
        compiler_params=_cparams("parallel"),
        name="proj_res",
    )(*a_list, *w_list, res)


def _ffn_kernel(x_ref, g_ref, wg_ref, wu_ref, wd_ref, *rest, final):
    if final:
        gf_ref, o_ref, h_scr, acc_scr = rest
    else:
        o_ref, h_scr, acc_scr = rest
    j = pl.program_id(1)

    @pl.when(j == 0)
    def _():
        x = x_ref[...]
        h_scr[...] = _rms(x, g_ref[...]).astype(BF16)
        acc_scr[...] = x

    hb = h_scr[...]
    gate = _dot(hb, wg_ref[...])
    up = _dot(hb, wu_ref[...])
    act = gate * _sigmoid(gate) * up
    acc_scr[...] += _dot(act.astype(BF16), wd_ref[...])

    @pl.when(j == pl.num_programs(1) - 1)
    def _():
        y = acc_scr[...]
        if final:
            y = _rms(y, gf_ref[...])
        o_ref[...] = y


def ffn(x2d, g, w_gu_bf, w_down_bf, g_final=None, tm=512):
    m, d = x2d.shape
    f = w_down_bf.shape[0]
    tm = min(tm, m)
    nf = 2 if (f // 2) % LANE == 0 else 1
    tf = f // nf
    final = g_final is not None
    in_specs = [pl.BlockSpec((tm, d), lambda i, j: (i, 0)),
                pl.BlockSpec((1, d), lambda i, j: (0, 0)),
                pl.BlockSpec((d, tf), lambda i, j: (0, j)),
                pl.BlockSpec((d, tf), lambda i, j: (0, j + nf)),
                pl.BlockSpec((tf, d), lambda i, j: (j, 0))]
    args = [x2d, g.reshape(1, d), w_gu_bf, w_gu_bf, w_down_bf]
    if final:
        in_specs.append(pl.BlockSpec((1, d), lambda i, j: (0, 0)))
        args.append(g_final.reshape(1, d))
    return pl.pallas_call(
        functools.partial(_ffn_kernel, final=final),
        grid=(m // tm, nf),
        in_specs=in_specs,
        out_specs=pl.BlockSpec((tm, d), lambda i, j: (i, 0)),
        out_shape=jax.ShapeDtypeStruct((m, d), F32),
        scratch_shapes=[pltpu.VMEM((tm, d), BF16), pltpu.VMEM((tm, d), F32)],
        compiler_params=_cparams("parallel", "arbitrary"),
        name="ffn",
    )(*args)


def _gelu_tanh(x):
    c = math.sqrt(2.0 / math.pi)
    return x * (0.5 * (1.0 + jnp.tanh(c * (x + 0.044715 * (x * x * x)))))


def _compress_kernel(x_ref, pe_ref, w1k_ref, w1v_ref, w2_ref, o_ref, *, n_pages):
    hidden = []
    for kv, w1_ref in enumerate((w1k_ref, w1v_ref)):
        slabs = []
        for c in range(kv * NSA_KV_HEADS, (kv + 1) * NSA_KV_HEADS):
            rows = [x_ref[pl.ds(c * HEAD_DIM + d, n_pages, stride=KV_ROW), :] for d in range(HEAD_DIM)]
            slabs.append((jnp.concatenate(rows, axis=1) + pe_ref[c:c + 1, :]).astype(BF16))
        h = _gelu_tanh(_dot(jnp.concatenate(slabs, axis=0), w1_ref[...])).astype(BF16)
        hidden += [h[i * n_pages:(i + 1) * n_pages] for i in range(NSA_KV_HEADS)]
    o_ref[...] = _dot(jnp.concatenate(hidden, axis=1), w2_ref[...]).astype(o_ref.dtype)


def compress(pages2d, cw, pages_per_step=64):
    pe4, w1k, w1v, w2 = cw
    n_pages = pages2d.shape[0] // KV_ROW
    pps = math.gcd(pages_per_step, n_pages)
    blocks_per_page = TQ // CMP_BLOCK
    const = lambda a: pl.BlockSpec(a.shape, lambda i: (0, 0))
    return pl.pallas_call(
        functools.partial(_compress_kernel, n_pages=pps),
        grid=(n_pages // pps,),
        in_specs=[pl.BlockSpec((pps * KV_ROW, TQ), lambda i: (i, 0)), const(pe4), const(w1k), const(w1v), const(w2)],
        out_specs=pl.BlockSpec((pps, blocks_per_page * KV_ROW), lambda i: (i, 0)),
        out_shape=jax.ShapeDtypeStruct((n_pages, blocks_per_page * KV_ROW), BF16),
        compiler_params=_cparams("parallel"),
        name="compress",
    )(pages2d, pe4, w1k, w1v, w2)


def _compress_weights(pe_k, w1_k, w2_k, pe_v, w1_v, w2_v):
    nj = TQ // CMP_BLOCK
    nc = 2 * NSA_KV_HEADS
    l, d, h = w1_k.shape

    def first(w1):
        w = jnp.zeros((d, nj, l, nj, h), F32)
        for j in range(nj):
            w = w.at[:, j, :, j, :].set(jnp.transpose(w1, (1, 0, 2)))
        return w.reshape(d * nj * l, nj * h).astype(BF16)

    def pe_row(pe):
        return jnp.broadcast_to(pe.T[:, None, :], (d, nj, l)).reshape(1, d * nj * l)

    w2 = jnp.zeros((nc, nj, h, nj, nc, w2_k.shape[1]), F32)
    for c in range(nc):
        for j in range(nj):
            w2 = w2.at[c, j, :, j, c, :].set(w2_k if c < NSA_KV_HEADS else w2_v)
    pe4 = jnp.concatenate([pe_row(pe_k if c < NSA_KV_HEADS else pe_v) for c in range(nc)], axis=0)
    return pe4, first(w1_k), first(w1_v), w2.reshape(nc * nj * h, nj * nc * w2_k.shape[1]).astype(BF16)


def _heads_to_rows(o_t, n_rep, pair):
    r0 = 2 * pair
    two = jnp.concatenate([o_t[:, r0 * TQ:(r0 + 1) * TQ], o_t[:, (r0 + 1) * TQ:(r0 + 2) * TQ]], axis=0)
    return two.T


def _banded_kernel(*refs, n_groups, n_rep, window, slopes, has_sink, kv_block):
    if has_sink:
        sink_ref, q_ref, x_ref, xt_ref, o_ref = refs
    else:
        q_ref, x_ref, xt_ref, o_ref = refs
    qt = pl.program_id(1)
    nrow = n_rep * TQ
    n_tiles = window // TQ + 1
    ki = lax.broadcasted_iota(jnp.int32, (TQ, nrow), 0)
    qi = lax.broadcasted_iota(jnp.int32, (TQ, nrow), 1) % TQ
    diff = qi - ki
    diff_f = diff.astype(F32)
    tiles = []
    for j in range(n_tiles):
        kt = qt - (n_tiles - 1 - j)
        ktc = jnp.maximum(kt, 0)
        x = x_ref[0, pl.ds(pl.multiple_of(ktc * TQ, TQ), TQ), :]
        tiles.append((x, ktc, jnp.where(kt >= 0, 0.0, MASK)))
    for g in range(n_groups):
        heads = [g * n_rep + r for r in range(n_rep)]
        qrows = jnp.concatenate([q_ref[0, :, h * QPAD:(h + 1) * QPAD] for h in heads], axis=0)
        slope_row = _row([slopes[h] for h in heads], TQ)
        bias0 = slope_row * diff_f
        v0 = kv_block * KV_ROW + (NSA_KV_HEADS + g) * HEAD_DIM
        s_list = []
        for j, (x, _, before_start) in enumerate(tiles):
            s = _dot_nt(x, qrows) - bias0 + (before_start - slope_row * float((n_tiles - 1 - j) * TQ))
            if j == 0:
                s = jnp.where(diff <= 0, s, MASK)
            if j == n_tiles - 1:
                s = jnp.where(diff >= 0, s, MASK)
            s_list.append(s)
        m = s_list[0].max(axis=0, keepdims=True)
        for s in s_list[1:]:
            m = jnp.maximum(m, s.max(axis=0, keepdims=True))
        if has_sink:
            sink_row = jnp.concatenate([jnp.full((1, TQ), sink_ref[h], F32) for h in heads], axis=1)
            m = jnp.maximum(m, sink_row)
        p_list = [jnp.exp(s - m) for s in s_list]
        den = p_list[0].sum(axis=0, keepdims=True)
        for p in p_list[1:]:
            den = den + p.sum(axis=0, keepdims=True)
        if has_sink:
            den = den + jnp.exp(sink_row - m)
        v_all = jnp.concatenate([xt_ref[0, ktc, v0:v0 + HEAD_DIM, :] for _, ktc, _ in tiles], axis=1)
        p_all = jnp.concatenate([p.astype(BF16) for p in p_list], axis=0)
        o_t = _dot(v_all, p_all) / jnp.maximum(den, 1e-30)
        for pair in range(n_rep // 2):
            h0 = heads[2 * pair]
            o_ref[0, :, h0 * HEAD_DIM:(h0 + 2) * HEAD_DIM] = _heads_to_rows(o_t, n_rep, pair)


def banded_attention(q_pad, x_rows, x_pages, kv_block, n_groups, n_rep, window, sinks=None):
    b, seq = q_pad.shape[:2]
    n_heads = n_groups * n_rep
    has_sink = sinks is not None
    in_specs = [pl.BlockSpec((1, TQ, n_heads * QPAD), lambda i, j: (i, j, 0)),
                pl.BlockSpec((1, seq, KV_ROW), lambda i, j: (i, 0, kv_block)),
                pl.BlockSpec((1,) + x_pages.shape[1:], lambda i, j: (i, 0, 0, 0))]
    args = [q_pad, x_rows, x_pages]
    if has_sink:
        in_specs.insert(0, pl.BlockSpec(memory_space=pltpu.SMEM))
        args.insert(0, sinks)
    assert window % TQ == 0 and window >= TQ
    return pl.pallas_call(
        functools.partial(_banded_kernel, n_groups=n_groups, n_rep=n_rep, window=window,
                          slopes=_alibi(n_heads), has_sink=has_sink, kv_block=kv_block),
        grid=(b, seq // TQ),
        in_specs=in_specs,
        out_specs=pl.BlockSpec((1, TQ, n_heads * HEAD_DIM), lambda i, j: (i, j, 0)),
        out_shape=jax.ShapeDtypeStruct((b, seq, n_heads * HEAD_DIM), F32),
        compiler_params=_cparams("parallel", "parallel"),
        name="banded_attention",
    )(*args)


def _banded_decode_kernel(*refs, n_groups, n_rep, window, slopes, has_sink, pos0):
    if has_sink:
        sink_ref, q_ref, st_ref, new_ref, o_ref = refs
    else:
        q_ref, st_ref, new_ref, o_ref = refs
    nq = q_ref.shape[1]
    npast = st_ref.shape[2]
    nrow = n_rep * nq
    xs_t = st_ref[0].astype(BF16)
    xn = jnp.concatenate([new_ref[0], jnp.zeros((TQ - nq, KV_ROW), F32)], axis=0).astype(BF16)
    qi_s = jnp.concatenate([lax.broadcasted_iota(jnp.int32, (nq, npast), 0)] * n_rep, axis=0)
    kj_s = lax.broadcasted_iota(jnp.int32, (nrow, npast), 1)
    dist_s = npast + qi_s - kj_s
    valid_s = (dist_s >= 0) & (dist_s <= window) & (pos0 - npast + kj_s >= 0)
    qi_n = jnp.concatenate([lax.broadcasted_iota(jnp.int32, (nq, TQ), 0)] * n_rep, axis=0)
    kj_n = lax.broadcasted_iota(jnp.int32, (nrow, TQ), 1)
    dist_n = qi_n - kj_n
    valid_n = (dist_n >= 0) & (dist_n <= window) & (kj_n < nq)
    for g in range(n_groups):
        heads = [g * n_rep + r for r in range(n_rep)]
        qrows = jnp.concatenate([q_ref[0, :, h * QPAD:(h + 1) * QPAD] for h in heads], axis=0).astype(BF16)
        slope_col = _col([slopes[h] for h in heads], nq)
        s_s = jnp.where(valid_s, _dot(qrows, xs_t) - slope_col * dist_s.astype(F32), NEG)
        s_n = jnp.where(valid_n, _dot_nt(qrows, xn) - slope_col * dist_n.astype(F32), NEG)
        m = jnp.maximum(jnp.max(s_s, axis=-1, keepdims=True), jnp.max(s_n, axis=-1, keepdims=True))
        if has_sink:
            sink_col = jnp.concatenate([jnp.full((nq, 1), sink_ref[h], F32) for h in heads], axis=0)
            m = jnp.maximum(m, sink_col)
        p_s = jnp.where(valid_s, jnp.exp(s_s - m), 0.0)
        p_n = jnp.where(valid_n, jnp.exp(s_n - m), 0.0)
        den = jnp.sum(p_s, axis=-1, keepdims=True) + jnp.sum(p_n, axis=-1, keepdims=True)
        if has_sink:
            den = den + jnp.exp(sink_col - m)
        o = (_dot_nt(p_s.astype(BF16), xs_t) + _dot(p_n.astype(BF16), xn)) / jnp.maximum(den, 1e-30)
        v0 = (NSA_KV_HEADS + g) * HEAD_DIM
        for r, h in enumerate(heads):
            o_ref[0, :, h * HEAD_DIM:(h + 1) * HEAD_DIM] = o[r * nq:(r + 1) * nq, v0:v0 + HEAD_DIM]


def banded_decode(q_pad, state_t, new_kv, n_groups, n_rep, window, pos0, sinks=None):
    b, nq = q_pad.shape[:2]
    npast = state_t.shape[2]
    n_heads = n_groups * n_rep
    has_sink = sinks is not None
    in_specs = [pl.BlockSpec((1, nq, n_heads * QPAD), lambda i: (i, 0, 0)),
                pl.BlockSpec((1, KV_ROW, npast), lambda i: (i, 0, 0)),
                pl.BlockSpec((1, nq, KV_ROW), lambda i: (i, 0, 0))]
    args = [q_pad, state_t, new_kv]
    if has_sink:
        in_specs.insert(0, pl.BlockSpec(memory_space=pltpu.SMEM))
        args.insert(0, sinks)
    return pl.pallas_call(
        functools.partial(_banded_decode_kernel, n_groups=n_groups, n_rep=n_rep, window=window,
                          slopes=_alibi(n_heads), has_sink=has_sink, pos0=pos0),
        grid=(b,),
        in_specs=in_specs,
        out_specs=pl.BlockSpec((1, nq, n_heads * HEAD_DIM), lambda i: (i, 0, 0)),
        out_shape=jax.ShapeDtypeStruct((b, nq, n_heads * HEAD_DIM), F32),
        compiler_params=_cparams("parallel"),
        name="banded_decode",
    )(*args)


def _select_blocks_t(imp, cur, n_sel):
    nb = imp.shape[0]
    nrow = lax.broadcasted_iota(jnp.int32, imp.shape, 0)
    count = jnp.zeros(imp.shape, F32)
    for i in range(nb):
        ci = imp[i:i + 1, :]
        ahead = jnp.where(ci > imp, 1.0, jnp.where(ci == imp, jnp.where(nrow > i, 1.0, 0.0), 0.0))
        count = count + jnp.where(cur > i, ahead, 0.0)
    return jnp.where(nrow < cur, jnp.where(count < n_sel - 1, 1.0, 0.0), jnp.where(nrow == cur, 1.0, 0.0))


def _nsa_prompt_kernel(q_ref, gd_ref, kc_ref, kct_ref, x_ref, xt_ref, ow_ref, o_ref, m_scr, l_scr, acc_scr,
                       *, n_sel, kv_block):
    qt = pl.program_id(1)
    n_rep = NSA_GROUP
    nrow = n_rep * TQ
    slopes = _alibi(NSA_HEADS)
    nb = kc_ref.shape[1]
    kcb = kc_ref[0]
    gates = _sigmoid(gd_ref[0])
    tk = 2 * TQ
    n_seq_tiles = x_ref.shape[1] // TQ
    ki = lax.broadcasted_iota(jnp.int32, (tk, nrow), 0)
    qi = lax.broadcasted_iota(jnp.int32, (tk, nrow), 1) % TQ
    diff = qi - ki
    tpos = qt * TQ + lax.broadcasted_iota(jnp.int32, (nb, nrow), 1) % TQ
    blk = lax.broadcasted_iota(jnp.int32, (nb, nrow), 0)
    dist_c = tpos - ((blk + 1) * CMP_BLOCK - 1)
    valid_c = dist_c >= 0
    cur = (qt * TQ + lax.broadcasted_iota(jnp.int32, (nb, TQ), 1)) // CMP_BLOCK
    key_blk = lax.broadcasted_iota(jnp.int32, (tk, nb), 0) // CMP_BLOCK
    blk_col = lax.broadcasted_iota(jnp.int32, (tk, nb), 1)
    diff_f = diff.astype(F32)
    groups = []
    for g in range(NSA_KV_HEADS):
        heads = [g * n_rep + r for r in range(n_rep)]
        qrows = jnp.concatenate([q_ref[0, :, h * QPAD:(h + 1) * QPAD] for h in heads], axis=0)
        slope_row = _row([slopes[h] for h in heads], TQ)
        vrow = (NSA_KV_HEADS + g) * HEAD_DIM
        s_c = jnp.where(valid_c, _dot_nt(kcb, qrows) - slope_row * dist_c.astype(F32), NEG)
        e_c = jnp.where(valid_c, jnp.exp(s_c - jnp.max(s_c, axis=0, keepdims=True)), 0.0)
        p_c = e_c / jnp.maximum(jnp.sum(e_c, axis=0, keepdims=True), 1e-30)
        ocmp_t = _dot(kct_ref[0, vrow:vrow + HEAD_DIM, :], p_c.astype(BF16))
        imp = p_c[:, 0:TQ]
        for r in range(1, n_rep):
            imp = imp + p_c[:, r * TQ:(r + 1) * TQ]
        taken = _select_blocks_t(imp, cur, n_sel)
        not_taken = jnp.where(taken > 0.5, 0.0, MASK).astype(BF16)
        m_scr[g] = jnp.full((1, nrow), NEG, F32)
        l_scr[g] = jnp.zeros((1, nrow), F32)
        acc_scr[g] = jnp.zeros((HEAD_DIM, nrow), F32)
        groups.append((heads, qrows, slope_row, slope_row * diff_f, kv_block * KV_ROW + vrow, not_taken, ocmp_t))

    def chunk(kt0, causal):
        kt1 = jnp.minimum(kt0 + 1, n_seq_tiles - 1)
        x = jnp.concatenate([x_ref[0, pl.ds(pl.multiple_of(kt0 * TQ, TQ), TQ), :],
                             x_ref[0, pl.ds(pl.multiple_of(kt1 * TQ, TQ), TQ), :]], axis=0)
        off = (qt - kt0) * TQ
        expand = jnp.where(blk_col == key_blk + kt0 * (TQ // CMP_BLOCK), 1.0, 0.0).astype(BF16)
        for g, (_, qrows, slope_row, bias0, v0, not_taken, _) in enumerate(groups):
            s = _dot_nt(x, qrows) - bias0 + jnp.concatenate([_dot(expand, not_taken)] * n_rep, axis=1)
            if causal:
                s = jnp.where(diff + off >= 0, s, MASK)
            shift = slope_row * off.astype(F32)
            m_old = m_scr[g]
            m_new = jnp.maximum(m_old, jnp.max(s, axis=0, keepdims=True) - shift)
            alpha = jnp.exp(m_old - m_new)
            p = jnp.exp(s - (m_new + shift))
            l_scr[g] = alpha * l_scr[g] + jnp.sum(p, axis=0, keepdims=True)
            v_t = jnp.concatenate([xt_ref[0, kt0, v0:v0 + HEAD_DIM, :], xt_ref[0, kt1, v0:v0 + HEAD_DIM, :]], axis=1)
            acc_scr[g] = alpha * acc_scr[g] + _dot(v_t, p.astype(BF16))
            m_scr[g] = m_new

    n_full = qt // 2

    def body(j, carry):
        chunk(2 * j, False)
        return carry

    lax.fori_loop(0, n_full, body, 0)
    chunk(2 * n_full, True)
    for g, (heads, _, _, _, _, _, ocmp_t) in enumerate(groups):
        oslc_t = acc_scr[g] / jnp.maximum(l_scr[g], 1e-30)
        for pair in range(n_rep // 2):
            h0 = heads[2 * pair]
            hh = slice(h0 * HEAD_DIM, (h0 + 2) * HEAD_DIM)
            gate = lambda j: _pair_lanes(gates[:, 3 * h0 + j:3 * h0 + j + 1], gates[:, 3 * h0 + 3 + j:3 * h0 + 4 + j])
            o_ref[0, :, hh] = (gate(0) * _heads_to_rows(ocmp_t, n_rep, pair)
                               + gate(1) * _heads_to_rows(oslc_t, n_rep, pair)
                               + gate(2) * ow_ref[0, :, hh])


def nsa_prompt(q_pad, gd, kc, x_rows, x_pages, kv_block, o_win):
    b, seq = q_pad.shape[:2]
    nb = kc.shape[1]
    nrow = NSA_GROUP * TQ
    kct = jnp.swapaxes(kc, 1, 2)
    return pl.pallas_call(
        functools.partial(_nsa_prompt_kernel, n_sel=min(SEL_TOPK, nb), kv_block=kv_block),
        grid=(b, seq // TQ),
        in_specs=[pl.BlockSpec((1, TQ, NSA_HEADS * QPAD), lambda i, j: (i, j, 0)),
                  pl.BlockSpec((1, TQ, LANE), lambda i, j: (i, j, 0)),
                  pl.BlockSpec((1, nb, KV_ROW), lambda i, j: (i, 0, 0)),
                  pl.BlockSpec((1, KV_ROW, nb), lambda i, j: (i, 0, 0)),
                  pl.BlockSpec((1, seq, KV_ROW), lambda i, j: (i, 0, kv_block)),
                  pl.BlockSpec((1,) + x_pages.shape[1:], lambda i, j: (i, 0, 0, 0)),
                  pl.BlockSpec((1, TQ, NSA_WIDTH), lambda i, j: (i, j, 0))],
        out_specs=pl.BlockSpec((1, TQ, NSA_WIDTH), lambda i, j: (i, j, 0)),
        out_shape=jax.ShapeDtypeStruct((b, seq, NSA_WIDTH), F32),
        scratch_shapes=[pltpu.VMEM((NSA_KV_HEADS, 1, nrow), F32), pltpu.VMEM((NSA_KV_HEADS, 1, nrow), F32),
                        pltpu.VMEM((NSA_KV_HEADS, HEAD_DIM, nrow), F32)],
        compiler_params=_cparams("parallel", "parallel"),
        name="nsa_prompt",
    )(q_pad, gd, kc, kct, x_rows, x_pages, o_win)


def _select_blocks(imp, n_sel):
    nb = imp.shape[1]
    ncol = lax.broadcasted_iota(jnp.int32, imp.shape, 1)
    count = jnp.zeros(imp.shape, F32)
    for i in range(nb):
        ci = imp[:, i:i + 1]
        count = count + jnp.where(ci > imp, 1.0, jnp.where(ci == imp, jnp.where(ncol > i, 1.0, 0.0), 0.0))
    return jnp.where(count < n_sel - 1, 1.0, 0.0)


def _nsa_decode_kernel(pt_ref, q_ref, gd_ref, kc_ref, new_ref, ow_ref, ex_ref, *rest, pages_per_step, pos0, n_sel):
    page_refs = rest[:pages_per_step]
    o_ref, m_scr, l_scr, acc_scr, mask_scr, bias_scr, ocmp_scr = rest[pages_per_step:]
    p = pl.program_id(1)
    n_rep = NSA_GROUP
    nq = q_ref.shape[1]
    ngrp = n_rep * nq
    nrow = NSA_KV_HEADS * ngrp
    nb = kc_ref.shape[1]
    page = page_refs[0].shape[1]
    nkey = pages_per_step * page
    slopes = _alibi(NSA_HEADS)
    qrows = jnp.concatenate([q_ref[0, :, h * QPAD:(h + 1) * QPAD] for h in range(NSA_HEADS)], axis=0).astype(BF16)
    slope_col = _col(slopes, nq)

    def per_query(a):
        return jnp.concatenate([a[g] for g in range(NSA_KV_HEADS) for _ in range(n_rep)], axis=0)

    @pl.when(p == 0)
    def _():
        kcb = kc_ref[0]
        tpos = pos0 + jnp.concatenate([lax.broadcasted_iota(jnp.int32, (nq, nb), 0)] * (nrow // nq), axis=0)
        ncol = lax.broadcasted_iota(jnp.int32, (nrow, nb), 1)
        dist = tpos - ((ncol + 1) * CMP_BLOCK - 1)
        valid = dist >= 0
        s = jnp.where(valid, _dot_nt(qrows, kcb) - slope_col * dist.astype(F32), NEG)
        e = jnp.where(valid, jnp.exp(s - jnp.max(s, axis=-1, keepdims=True)), 0.0)
        pc = e / jnp.maximum(jnp.sum(e, axis=-1, keepdims=True), 1e-30)
        ocmp_scr[...] = _dot(pc.astype(BF16), kcb)
        not_taken = []
        for g in range(NSA_KV_HEADS):
            imp = pc[g * ngrp:g * ngrp + nq]
            for r in range(1, n_rep):
                imp = imp + pc[g * ngrp + r * nq:g * ngrp + (r + 1) * nq]
            not_taken.append(jnp.where(_select_blocks(imp, n_sel) > 0.5, 0.0, MASK))
        not_taken = jnp.concatenate(not_taken, axis=0).astype(BF16)
        for st in range(mask_scr.shape[0]):
            mask_scr[st] = _dot(not_taken, ex_ref[:, st * nkey:(st + 1) * nkey])
        qi = jnp.concatenate([lax.broadcasted_iota(jnp.int32, (nq, nkey), 0)] * (nrow // nq), axis=0)
        kj = lax.broadcasted_iota(jnp.int32, (nrow, nkey), 1)
        bias_scr[...] = slope_col * (pos0 + qi - kj).astype(F32)
        m_scr[...] = jnp.full_like(m_scr, NEG)
        l_scr[...] = jnp.zeros_like(l_scr)
        acc_scr[...] = jnp.zeros_like(acc_scr)

    x_t = jnp.concatenate([r[...] for r in page_refs], axis=1).astype(BF16)
    mask = mask_scr[p]
    s = _dot(qrows, x_t) - bias_scr[...] + per_query([mask[g * nq:(g + 1) * nq] for g in range(NSA_KV_HEADS)])
    shift = slope_col * (p * nkey).astype(F32)
    m_old = m_scr[...]
    m_new = jnp.maximum(m_old, jnp.max(s, axis=-1, keepdims=True) + shift)
    alpha = jnp.exp(m_old - m_new)
    pr = jnp.exp(s - (m_new - shift))
    l_scr[...] = alpha * l_scr[...] + jnp.sum(pr, axis=-1, keepdims=True)
    acc_scr[...] = alpha * acc_scr[...] + _dot_nt(pr.astype(BF16), x_t)
    m_scr[...] = m_new

    @pl.when(p == pl.num_programs(1) - 1)
    def _():
        xn = jnp.concatenate([new_ref[0], jnp.zeros((TQ - nq, KV_ROW), F32)], axis=0).astype(BF16)
        qi_n = jnp.concatenate([lax.broadcasted_iota(jnp.int32, (nq, TQ), 0)] * (nrow // nq), axis=0)
        kj_n = lax.broadcasted_iota(jnp.int32, (nrow, TQ), 1)
        dist_n = qi_n - kj_n
        valid_n = (dist_n >= 0) & (kj_n < nq)
        s_n = jnp.where(valid_n, _dot_nt(qrows, xn) - slope_col * dist_n.astype(F32), NEG)
        m_old = m_scr[...]
        m_new = jnp.maximum(m_old, jnp.max(s_n, axis=-1, keepdims=True))
        alpha = jnp.exp(m_old - m_new)
        p_n = jnp.where(valid_n, jnp.exp(s_n - m_new), 0.0)
        den = alpha * l_scr[...] + jnp.sum(p_n, axis=-1, keepdims=True)
        o_slc = (alpha * acc_scr[...] + _dot(p_n.astype(BF16), xn)) / jnp.maximum(den, 1e-30)
        o_cmp = ocmp_scr[...]
        gates = _sigmoid(gd_ref[0])
        for h in range(NSA_HEADS):
            g = h // n_rep
            v0 = (NSA_KV_HEADS + g) * HEAD_DIM
            rr = slice(h * nq, (h + 1) * nq)
            hh = slice(h * HEAD_DIM, (h + 1) * HEAD_DIM)
            o_ref[0, :, hh] = (gates[:, 3 * h:3 * h + 1] * o_cmp[rr, v0:v0 + HEAD_DIM]
                               + gates[:, 3 * h + 1:3 * h + 2] * o_slc[rr, v0:v0 + HEAD_DIM]
                               + gates[:, 3 * h + 2:3 * h + 3] * ow_ref[0, :, hh])


def nsa_decode(page_table, q_pad, gd, kc, new_kv, o_win, pool_t, pos0, pages_per_step=16):
    b, nq = q_pad.shape[:2]
    nb = kc.shape[1]
    n_pages = page_table.shape[1]
    page = pool_t.shape[2]
    pps = math.gcd(pages_per_step, n_pages)
    n_steps = n_pages // pps
    nkey = pps * page
    nrow = NSA_HEADS * nq
    expand = (jnp.arange(n_pages * page)[None, :] // CMP_BLOCK == jnp.arange(nb)[:, None]).astype(BF16)

    def page_spec(j):
        return pl.BlockSpec((None, KV_ROW, page), lambda i, p, pt: (pt[i, p * pps + j], 0, 0))

    grid_spec = pltpu.PrefetchScalarGridSpec(
        num_scalar_prefetch=1,
        grid=(b, n_steps),
        in_specs=[pl.BlockSpec((1, nq, NSA_HEADS * QPAD), lambda i, p, pt: (i, 0, 0)),
                  pl.BlockSpec((1, nq, LANE), lambda i, p, pt: (i, 0, 0)),
                  pl.BlockSpec((1, nb, KV_ROW), lambda i, p, pt: (i, 0, 0)),
                  pl.BlockSpec((1, nq, KV_ROW), lambda i, p, pt: (i, 0, 0)),
                  pl.BlockSpec((1, nq, NSA_WIDTH), lambda i, p, pt: (i, 0, 0)),
                  pl.BlockSpec(expand.shape, lambda i, p, pt: (0, 0))]
                 + [page_spec(j) for j in range(pps)],
        out_specs=pl.BlockSpec((1, nq, NSA_WIDTH), lambda i, p, pt: (i, 0, 0)),
        scratch_shapes=[pltpu.VMEM((nrow, 1), F32), pltpu.VMEM((nrow, 1), F32),
                        pltpu.VMEM((nrow, KV_ROW), F32),
                        pltpu.VMEM((n_steps, NSA_KV_HEADS * nq, nkey), F32),
                        pltpu.VMEM((nrow, nkey), F32),
                        pltpu.VMEM((nrow, KV_ROW), F32)])
    return pl.pallas_call(
        functools.partial(_nsa_decode_kernel, pages_per_step=pps, pos0=pos0, n_sel=min(SEL_TOPK, nb + 1)),
        grid_spec=grid_spec,
        out_shape=jax.ShapeDtypeStruct((b, nq, NSA_WIDTH), F32),
        compiler_params=_cparams("parallel", "arbitrary"),
        name="nsa_decode",
    )(page_table, q_pad, gd, kc, new_kv, o_win, expand, *([pool_t] * pps))


def _cumsum_rows(v):
    n = v.shape[0]
    ri = lax.broadcasted_iota(jnp.int32, v.shape, 0)
    sh = 1
    while sh < n:
        v = v + jnp.where(ri >= sh, pltpu.roll(v, sh, 0), 0.0)
        sh *= 2
    return v


def _ssd_kernel(z_ref, xbc_ref, gd_ref, tail_ref, s0_ref, cw_ref, cb_ref, dtb_ref, alog_ref, dsk_ref, ng_ref,
                y_ref, sfin_ref, xp_scr, st_scr, y_scr, *, n_valid):
    c = pl.program_id(1)
    q = xbc_ref.shape[1]

    @pl.when(c == 0)
    def _():
        xp_scr[0:SUBLANE, :] = tail_ref[0]
        st_scr[...] = s0_ref[0]

    xp_scr[SUBLANE:SUBLANE + q, :] = xbc_ref[0]
    acc = cb_ref[...]
    for k in range(SSM_CONV):
        lo = SUBLANE - (SSM_CONV - 1) + k
        acc = acc + xp_scr[lo:lo + q, :] * cw_ref[k:k + 1, :]
    nxt = xp_scr[q:q + SUBLANE, :]
    xp_scr[0:SUBLANE, :] = nxt
    act = acc * _sigmoid(acc)

    raw = gd_ref[0] + dtb_ref[...]
    dt = jnp.maximum(raw, 0.0) + jnp.log1p(jnp.exp(-jnp.abs(raw)))
    if n_valid < q:
        dt = jnp.where(lax.broadcasted_iota(jnp.int32, dt.shape, 0) < n_valid, dt, 0.0)
    acum = _cumsum_rows(dt * (-jnp.exp(alog_ref[...])))
    acum_t = acum.T
    ri = lax.broadcasted_iota(jnp.int32, (q, q), 0)
    ci = lax.broadcasted_iota(jnp.int32, (q, q), 1)
    causal = ri >= ci
    half = lax.broadcasted_iota(jnp.int32, (2 * SSM_HEAD_DIM, 1), 0) < SSM_HEAD_DIM
    hpg = SSM_HEADS // SSM_GROUPS
    for pair in range(SSM_HEADS // 2):
        grp = (2 * pair) // hpg
        lanes = slice(pair * LANE, (pair + 1) * LANE)
        bm = act[:, SSM_D_INNER + grp * SSM_STATE:SSM_D_INNER + (grp + 1) * SSM_STATE].astype(BF16)
        cm_lo = SSM_D_INNER + SSM_GROUPS * SSM_STATE + grp * SSM_STATE
        cm = act[:, cm_lo:cm_lo + SSM_STATE].astype(BF16)
        cb = _dot_nt(cm, bm)
        xs = act[:, lanes]
        cols = []
        for h in (2 * pair, 2 * pair + 1):
            ln = DT_LANE + h
            cols.append((acum[:, ln:ln + 1], acum_t[ln:ln + 1, :], acum[q - 1:q, ln:ln + 1], dt[:, ln:ln + 1]))
        xdt = xs * _pair_lanes(cols[0][3], cols[1][3])
        xdt_bf = xdt.astype(BF16)
        y_parts = []
        for a_col, a_row, _, _ in cols:
            seg = a_col - a_row
            lmat = jnp.where(causal, jnp.exp(jnp.where(causal, seg, 0.0)), 0.0)
            y_parts.append(_dot((cb * lmat).astype(BF16), xdt_bf))
        lane = lax.broadcasted_iota(jnp.int32, (q, LANE), 1)
        y_diag = jnp.where(lane < SSM_HEAD_DIM, y_parts[0], y_parts[1])
        st = st_scr[pair * LANE:(pair + 1) * LANE, :]
        y_off = _dot_nt(cm, st.astype(BF16)) * _pair_lanes(jnp.exp(cols[0][0]), jnp.exp(cols[1][0]))
        dec_end = _pair_lanes(jnp.exp(cols[0][2] - cols[0][0]), jnp.exp(cols[1][2] - cols[1][0]))
        cs = _dot((xdt * dec_end).T.astype(BF16), bm)
        st_scr[pair * LANE:(pair + 1) * LANE, :] = st * jnp.where(half, jnp.exp(cols[0][2]), jnp.exp(cols[1][2])) + cs
        dsk = _pair_lanes(jnp.zeros((q, 1), F32) + dsk_ref[:, 2 * pair:2 * pair + 1],
                          jnp.zeros((q, 1), F32) + dsk_ref[:, 2 * pair + 1:2 * pair + 2])
        y_scr[:, lanes] = y_diag + y_off + dsk * xs

    z = z_ref[0]
    y_ref[0] = _rms(y_scr[...] * (z * _sigmoid(z)), ng_ref[...])

    @pl.when(c == pl.num_programs(1) - 1)
    def _():
        sfin_ref[0] = st_scr[...]


def ssd_mixer(z, xbc, gd, tail8, s0, conv_w, conv_b, dt_bias, a_log, d_skip, norm_g, n_valid):
    b, seq = z.shape[:2]
    nst = SSM_HEADS * SSM_HEAD_DIM

    def lane_piece(v):
        return jnp.zeros((1, LANE), F32).at[0, DT_LANE:DT_LANE + SSM_HEADS].set(v)

    alog_piece = jnp.full((1, LANE), -100.0, F32).at[0, DT_LANE:DT_LANE + SSM_HEADS].set(a_log)
    full2 = lambda shape: pl.BlockSpec(shape, lambda i, c: (0, 0))
    return pl.pallas_call(
        functools.partial(_ssd_kernel, n_valid=n_valid),
        grid=(b, seq // TQ),
        in_specs=[pl.BlockSpec((1, TQ, SSM_D_INNER), lambda i, c: (i, c, 0)),
                  pl.BlockSpec((1, TQ, SSM_CONV_DIM), lambda i, c: (i, c, 0)),
                  pl.BlockSpec((1, TQ, LANE), lambda i, c: (i, c, 0)),
                  pl.BlockSpec((1, SUBLANE, SSM_CONV_DIM), lambda i, c: (i, 0, 0)),
                  pl.BlockSpec((1, nst, SSM_STATE), lambda i, c: (i, 0, 0)),
                  full2((SSM_CONV, SSM_CONV_DIM)), full2((1, SSM_CONV_DIM)), full2((1, LANE)),
                  full2((1, LANE)), full2((1, SSM_HEADS)), full2((1, SSM_D_INNER))],
        out_specs=[pl.BlockSpec((1, TQ, SSM_D_INNER), lambda i, c: (i, c, 0)),
                   pl.BlockSpec((1, nst, SSM_STATE), lambda i, c: (i, 0, 0))],
        out_shape=[jax.ShapeDtypeStruct((b, seq, SSM_D_INNER), F32),
                   jax.ShapeDtypeStruct((b, nst, SSM_STATE), F32)],
        scratch_shapes=[pltpu.VMEM((SUBLANE + TQ, SSM_CONV_DIM), F32),
                        pltpu.VMEM((nst, SSM_STATE), F32),
                        pltpu.VMEM((TQ, SSM_D_INNER), F32)],
        compiler_params=_cparams("parallel", "arbitrary"),
        name="ssd_mixer",
    )(z, xbc, gd, tail8, s0, conv_w, conv_b.reshape(1, -1), lane_piece(dt_bias), alog_piece,
      d_skip.reshape(1, -1), norm_g.reshape(1, -1))


def _pad_q_cols(w_q, n_heads, n_rep):
    d = w_q.shape[0]
    out = jnp.zeros((d, n_heads, QPAD), w_q.dtype)
    wq = w_q.reshape(d, n_heads, HEAD_DIM)
    for h in range(n_heads):
        g = h // n_rep
        out = out.at[:, h, g * HEAD_DIM:(g + 1) * HEAD_DIM].set(wq[:, h] * SCALE)
    return out.reshape(d, n_heads * QPAD)


def _channel_major(a6):
    b, rows = a6.shape[:2]
    return jnp.transpose(a6, (0, 2, 3, 4, 1)).reshape(b, KV_ROW, rows)


def _token_major(a_t):
    b, _, rows = a_t.shape
    return jnp.transpose(a_t.reshape(b, 2, NSA_KV_HEADS, HEAD_DIM, rows), (0, 4, 1, 2, 3))


def _even_weights(w_in):
    d = w_in.shape[0]
    o = np.cumsum([0, NSA_WIDTH, KV_ROW, KV_ROW, KV_ROW, 3 * NSA_HEADS, SSM_D_INNER, SSM_CONV_DIM, SSM_HEADS])
    q, kvc, kvs, kvw, gt, z, xbc, dtr = (w_in[:, o[i]:o[i + 1]] for i in range(8))
    gd = jnp.zeros((d, LANE), F32).at[:, :3 * NSA_HEADS].set(gt).at[:, DT_LANE:DT_LANE + SSM_HEADS].set(dtr)
    w = jnp.concatenate([_pad_q_cols(q, NSA_HEADS, NSA_GROUP), kvc, kvs, kvw, z, xbc, gd], axis=1).astype(BF16)
    qw = NSA_HEADS * QPAD
    off = {"q": (0, qw), "kvc": (qw, KV_ROW), "kvs": (qw + KV_ROW, KV_ROW), "kvw": (qw + 2 * KV_ROW, KV_ROW),
           "kv3": (qw, 3 * KV_ROW), "z": (qw + 3 * KV_ROW, SSM_D_INNER),
           "xbc": (qw + 3 * KV_ROW + SSM_D_INNER, SSM_CONV_DIM),
           "gd": (qw + 3 * KV_ROW + SSM_D_INNER + SSM_CONV_DIM, LANE)}
    return w, off


def _even_layer(x, past, p, cache):
    (norm_g, w_in, w_out, cw, ssm_p) = p
    b, seq, d = x.shape
    m = b * seq
    decode = cache is not None
    w, off = _even_weights(w_in)
    conv_w, conv_b, dt_bias, a_log, d_skip, ssm_norm = ssm_p
    if not decode:
        names = [("row", "q", BF16), ("chan", "kvc", F32), ("chan", "kvs", F32), ("chan", "kvw", F32),
                 ("page", "kvc", F32), ("row", "kv3", BF16), ("page", "kv3", BF16),
                 ("row", "z", F32), ("row", "xbc", F32), ("row", "gd", F32)]
        q, kvc_t, kvs_t, kvw_t, kvc_pages, x_rows, x_pages, z, xbc, gd = norm_proj(
            x, norm_g, w, [(k,) + off[n] for k, n, _ in names], [dt for _, _, dt in names])
        q = q.reshape(b, seq, -1)
        gd3 = gd.reshape(b, seq, LANE)
        x_rows = x_rows.reshape(b, seq, 3 * KV_ROW)
        x_pages = x_pages.reshape(b, seq // TQ, 3 * KV_ROW, TQ)
        kc = compress(kvc_pages.reshape(-1, TQ), cw).reshape(b, seq // CMP_BLOCK, KV_ROW)
        o_win = banded_attention(q, x_rows, x_pages, 2, NSA_KV_HEADS, NSA_GROUP, NSA_WINDOW)
        o_nsa = nsa_prompt(q, gd3, kc, x_rows, x_pages, 1, o_win)
        tail8 = jnp.zeros((b, SUBLANE, SSM_CONV_DIM), F32)
        s0 = jnp.zeros((b, SSM_D_INNER, SSM_STATE), F32)
        xbc3 = xbc.reshape(b, seq, -1)
        o_ssm, s_fin = ssd_mixer(z.reshape(b, seq, -1), xbc3, gd3, tail8, s0,
                                 conv_w, conv_b, dt_bias, a_log, d_skip, ssm_norm, TQ)
        conv_new = xbc3[:, seq - (SSM_CONV - 1):]
        kv_outs = (_token_major(kvc_t), _token_major(kvs_t), _token_major(kvw_t[:, :, seq - min(NSA_WINDOW, seq):]))
    else:
        cmp_pool_t, slc_pool_t, page_table, win_state_t, conv0, ssm0 = cache
        n_pool, _, page = cmp_pool_t.shape
        assert page == TQ and seq < CMP_BLOCK and TQ % seq == 0
        names = [("row", "q", F32), ("row", "kvc", F32), ("row", "kvs", F32), ("row", "kvw", F32),
                 ("row", "z", F32), ("row", "xbc", F32), ("row", "gd", F32)]
        q, kvc, kvs, kvw, z, xbc, gd = norm_proj(
            x, norm_g, w, [(k,) + off[n] for k, n, _ in names], [dt for _, _, dt in names])
        q = q.reshape(b, seq, -1)
        gd3 = gd.reshape(b, seq, LANE)
        bpp = page // CMP_BLOCK
        kc_pool = compress(cmp_pool_t.reshape(n_pool * KV_ROW, page), cw)
        kc = kc_pool.reshape(n_pool, bpp, KV_ROW)[page_table].reshape(b, past // CMP_BLOCK, KV_ROW)
        kvw3 = kvw.reshape(b, seq, KV_ROW)
        o_win = banded_decode(q, win_state_t, kvw3, NSA_KV_HEADS, NSA_GROUP, NSA_WINDOW, past)
        o_nsa = nsa_decode(page_table, q, gd3, kc, kvs.reshape(b, seq, KV_ROW), o_win, slc_pool_t, past)
        padr = lambda a: jnp.pad(a.reshape(b, seq, -1), ((0, 0), (0, TQ - seq), (0, 0)))
        tail8 = jnp.pad(conv0, ((0, 0), (SUBLANE - (SSM_CONV - 1), 0), (0, 0)))
        o_ssm, s_fin = ssd_mixer(padr(z), padr(xbc), padr(gd), tail8, ssm0.reshape(b, SSM_D_INNER, SSM_STATE),
                                 conv_w, conv_b, dt_bias, a_log, d_skip, ssm_norm, seq)
        o_ssm = o_ssm[:, :seq]
        conv_new = jnp.concatenate([conv0, xbc.reshape(b, seq, -1)], axis=1)[:, -(SSM_CONV - 1):]
        npast = win_state_t.shape[2]
        win_out = jnp.concatenate([win_state_t, jnp.swapaxes(kvw3, 1, 2)], axis=2)[:, :, -npast:]
        kv6 = (b, seq, 2, NSA_KV_HEADS, HEAD_DIM)
        kv_outs = (kvc.reshape(kv6), kvs.reshape(kv6), _token_major(win_out))
    x = proj_res([o_nsa.reshape(m, -1), o_ssm.reshape(m, -1)],
                 [w_out[:NSA_WIDTH].astype(BF16), w_out[NSA_WIDTH:].astype(BF16)], x.reshape(m, d))
    outs = kv_outs + (s_fin.reshape(b, SSM_HEADS, SSM_HEAD_DIM, SSM_STATE), conv_new)
    return x.reshape(b, seq, d), outs


def _odd_layer(x, past, p, swa_state_t):
    norm_g, w_in, w_out, sinks = p
    b, seq, d = x.shape
    m = b * seq
    decode = swa_state_t is not None
    w = jnp.concatenate([_pad_q_cols(w_in[:, :SWA_WIDTH], SWA_HEADS, SWA_GROUP), w_in[:, SWA_WIDTH:]], axis=1).astype(BF16)
    qw = SWA_HEADS * QPAD
    if not decode:
        q, kv_t, x_rows, x_pages = norm_proj(
            x, norm_g, w, [("row", 0, qw), ("chan", qw, KV_ROW), ("row", qw, KV_ROW), ("page", qw, KV_ROW)],
            [BF16, F32, BF16, BF16])
        o = banded_attention(q.reshape(b, seq, -1), x_rows.reshape(b, seq, KV_ROW),
                             x_pages.reshape(b, seq // TQ, KV_ROW, TQ), 0, SWA_KV_HEADS, SWA_GROUP, SWA_WINDOW, sinks)
        kv_out = _token_major(kv_t[:, :, seq - min(SWA_WINDOW, seq):])
    else:
        q, kv = norm_proj(x, norm_g, w, [("row", 0, qw), ("row", qw, KV_ROW)], [F32, F32])
        kv3 = kv.reshape(b, seq, KV_ROW)
        o = banded_decode(q.reshape(b, seq, -1), swa_state_t, kv3, SWA_KV_HEADS, SWA_GROUP, SWA_WINDOW, past, sinks)
        npast = swa_state_t.shape[2]
        kv_out = _token_major(jnp.concatenate([swa_state_t, jnp.swapaxes(kv3, 1, 2)], axis=2)[:, :, -npast:])
    x = proj_res([o.reshape(m, -1)], [w_out.astype(BF16)], x.reshape(m, d))
    return x.reshape(b, seq, d), kv_out


def kernel(x_prompt, x_sample, cache_nsa_cmp_kv, cache_nsa_slc_kv, state_nsa_win_kv, state_ssm, state_ssm_conv, state_swa_kv, page_table, norm_mix, norm_ffn, norm_final, w_in_even, w_out_even, cmp_pe_k, cmp_w1_k, cmp_w2_k, cmp_pe_v, cmp_w1_v, cmp_w2_v, ssm_conv_w, ssm_conv_b, ssm_dt_bias, ssm_a_log, ssm_d, ssm_norm, w_in_odd, w_out_odd, swa_sinks, w_gate_up, w_down):
    depth = norm_mix.shape[0]
    page = cache_nsa_cmp_kv.shape[2]
    past = page_table.shape[1] * page
    xp, xs = x_prompt, x_sample
    outs_p = [[] for _ in range(6)]
    outs_s = [[] for _ in range(6)]
    for layer in range(depth):
        if layer % 2 == 0:
            e = layer // 2
            cw = _compress_weights(cmp_pe_k[e], cmp_w1_k[e], cmp_w2_k[e], cmp_pe_v[e], cmp_w1_v[e], cmp_w2_v[e])
            ssm_p = (ssm_conv_w[e], ssm_conv_b[e], ssm_dt_bias[e], ssm_a_log[e], ssm_d[e], ssm_norm[e])
            p = (norm_mix[layer], w_in_even[e], w_out_even[e], cw, ssm_p)
            xp, o = _even_layer(xp, 0, p, None)
            for lst, v in zip(outs_p[:5], o):
                lst.append(v)
            cache = (_channel_major(cache_nsa_cmp_kv[e]), _channel_major(cache_nsa_slc_kv[e]), page_table,
                     _channel_major(state_nsa_win_kv[e]), state_ssm_conv[e], state_ssm[e])
            xs, o = _even_layer(xs, past, p, cache)
            for lst, v in zip(outs_s[:5], o):
                lst.append(v)
        else:
            o_idx = layer // 2
            p = (norm_mix[layer], w_in_odd[o_idx], w_out_odd[o_idx], swa_sinks[o_idx])
            xp, kw = _odd_layer(xp, 0, p, None)
            outs_p[5].append(kw)
            xs, kw = _odd_layer(xs, past, p, _channel_major(state_swa_kv[o_idx]))
            outs_s[5].append(kw)
        gf = norm_final if layer == depth - 1 else None
        wgu, wd = w_gate_up[layer].astype(BF16), w_down[layer].astype(BF16)
        xp = ffn(xp.reshape(-1, xp.shape[-1]), norm_ffn[layer], wgu, wd, gf).reshape(xp.shape)
        xs = ffn(xs.reshape(-1, xs.shape[-1]), norm_ffn[layer], wgu, wd, gf).reshape(xs.shape)
    return (xp, xs) + tuple(jnp.stack(v) for v in outs_p) + tuple(jnp.stack(v) for v in outs_s)
```

```python
import functools
import math

import numpy as np
import jax
import jax.numpy as jnp
from jax import lax
from jax.experimental import pallas as pl
from jax.experimental.pallas import tpu as pltpu

F32 = jnp.float32
BF16 = jnp.bfloat16

HEAD_DIM = 64
NSA_HEADS = 8
NSA_KV_HEADS = 2
NSA_GROUP = NSA_HEADS // NSA_KV_HEADS
CMP_BLOCK = 64
SEL_TOPK = 16
NSA_WINDOW = 512
SSM_HEADS = 8
SSM_HEAD_DIM = 64
SSM_D_INNER = SSM_HEADS * SSM_HEAD_DIM
SSM_GROUPS = 2
SSM_STATE = 128
SSM_CONV = 4
SSM_CONV_DIM = SSM_D_INNER + 2 * SSM_GROUPS * SSM_STATE
SWA_HEADS = 16
SWA_KV_HEADS = 2
SWA_GROUP = SWA_HEADS // SWA_KV_HEADS
SWA_WINDOW = 128
RMS_EPS = 1e-6
NSA_WIDTH = NSA_HEADS * HEAD_DIM
SWA_WIDTH = SWA_HEADS * HEAD_DIM
SCALE = HEAD_DIM ** -0.5

LANE = 128
SUBLANE = 8
KV_ROW = 2 * NSA_KV_HEADS * HEAD_DIM
TQ = 128
NEG = -1e30
MASK = -2e30
VMEM_LIMIT = 56 * 1024 * 1024
DT_LANE = 3 * NSA_HEADS
N_STREAMS = 1


def _cparams(*sem):
    return pltpu.CompilerParams(dimension_semantics=sem, vmem_limit_bytes=VMEM_LIMIT)


def _dot(a, b):
    return jnp.dot(a, b, preferred_element_type=F32)


def _dot_nt(a, b):
    return lax.dot_general(a, b, (((1,), (1,)), ((), ())), preferred_element_type=F32)


def _rms(x, g):
    return x * lax.rsqrt(jnp.mean(x * x, axis=-1, keepdims=True) + RMS_EPS) * g


def _sigmoid(x):
    return 1.0 / (1.0 + jnp.exp(-x))


def _alibi(n_heads):
    return [float(2.0 ** (-8.0 * i / n_heads)) for i in range(1, n_heads + 1)]


def _col(vals, rows):
    return jnp.concatenate([jnp.full((rows, 1), v, F32) for v in vals], axis=0)


def _row(vals, cols):
    return jnp.concatenate([jnp.full((1, cols), v, F32) for v in vals], axis=1)


def _pair_lanes(c0, c1):
    lane = lax.broadcasted_iota(jnp.int32, (c0.shape[0], LANE), 1)
    return jnp.where(lane < HEAD_DIM, c0, c1)


def _norm_proj_kernel(x_ref, g_ref, w_ref, *o_refs, pieces):
    hb = _rms(x_ref[...], g_ref[...]).astype(BF16)
    cache = {}
    spans = sorted({(off, n) for _, off, n in pieces}, key=lambda s: -s[1])
    for o_ref, (kind, off, n) in zip(o_refs, pieces):
        p_off, p_n = next((o2, n2) for o2, n2 in spans if o2 <= off and off + n <= o2 + n2)
        if (p_off, p_n) not in cache:
            cache[(p_off, p_n)] = _dot(hb, w_ref[:, p_off:p_off + p_n])
        if (off, n) not in cache:
            cache[(off, n)] = cache[(p_off, p_n)][:, off - p_off:off - p_off + n]
        y = cache[(off, n)]
        if kind == "row":
            o_ref[...] = y.astype(o_ref.dtype)
        else:
            if ("t", off, n) not in cache:
                cache[("t", off, n)] = y.T
            y_t = cache[("t", off, n)]
            if kind == "chan":
                o_ref[0] = y_t.astype(o_ref.dtype)
            else:
                for j in range(o_ref.shape[0]):
                    o_ref[j] = y_t[:, j * TQ:(j + 1) * TQ].astype(o_ref.dtype)


def norm_proj(x3d, g, w_bf, pieces, dtypes, tm=256):
    b, seq, d = x3d.shape
    m = b * seq
    n_tot = w_bf.shape[1]
    tm = min(tm, m)
    per_b = seq // tm if seq >= tm else 1
    out_specs, out_shape = [], []
    for (kind, _, n), dt in zip(pieces, dtypes):
        if kind == "row":
            out_specs.append(pl.BlockSpec((tm, n), lambda i: (i, 0)))
            out_shape.append(jax.ShapeDtypeStruct((m, n), dt))
        elif kind == "chan":
            out_specs.append(pl.BlockSpec((1, n, tm), lambda i: (i // per_b, 0, i % per_b)))
            out_shape.append(jax.ShapeDtypeStruct((b, n, seq), dt))
        else:
            out_specs.append(pl.BlockSpec((tm // TQ, n, TQ), lambda i: (i, 0, 0)))
            out_shape.append(jax.ShapeDtypeStruct((m // TQ, n, TQ), dt))
    return pl.pallas_call(
        functools.partial(_norm_proj_kernel, pieces=tuple(pieces)),
        grid=(m // tm,),
        in_specs=[pl.BlockSpec((tm, d), lambda i: (i, 0)),
                  pl.BlockSpec((1, d), lambda i: (0, 0)),
                  pl.BlockSpec((d, n_tot), lambda i: (0, 0))],
        out_specs=out_specs,
        out_shape=out_shape,
        compiler_params=_cparams("parallel"),
        name="norm_proj",
    )(x3d.reshape(m, d), g.reshape(1, d), w_bf)


def _proj_res_kernel(*refs, n_in):
    a_refs, w_refs = refs[:n_in], refs[n_in:2 * n_in]
    res_ref, o_ref = refs[2 * n_in], refs[2 * n_in + 1]
    acc = res_ref[...]
    for a_ref, w_ref in zip(a_refs, w_refs):
        acc = acc + _dot(a_ref[...].astype(BF16), w_ref[...])
    o_ref[...] = acc


def proj_res(a_list, w_list, res, tm=512):
    m, d = res.shape
    tm = min(tm, m)
    n_in = len(a_list)
    in_specs = ([pl.BlockSpec((tm, a.shape[1]), lambda i: (i, 0)) for a in a_list]
                + [pl.BlockSpec(w.shape, lambda i: (0, 0)) for w in w_list]
                + [pl.BlockSpec((tm, d), lambda i: (i, 0))])
    return pl.pallas_call(
        functools.partial(_proj_res_kernel, n_in=n_in),
        grid=(m // tm,),
        in_specs=in_specs,
        out_specs=pl.BlockSpec((tm, d), lambda i: (i, 0)),
        out_shape=jax.ShapeDtypeStruct((m, d), F32),
        compiler_params=_cparams("parallel"),
        name="proj_res",
    )(*a_list, *w_list, res)


def _ffn_kernel(x_ref, g_ref, wg_ref, wu_ref, wd_ref, *rest, final):
    if final:
        gf_ref, o_ref, h_scr, acc_scr = rest
    else:
        o_ref, h_scr, acc_scr = rest
    j = pl.program_id(1)

    @pl.when(j == 0)
    def _():
        x = x_ref[...]
        h_scr[...] = _rms(x, g_ref[...]).astype(BF16)
        acc_scr[...] = x

    hb = h_scr[...]
    gate = _dot(hb, wg_ref[...])
    up = _dot(hb, wu_ref[...])
    act = gate * _sigmoid(gate) * up
    acc_scr[...] += _dot(act.astype(BF16), wd_ref[...])

    @pl.when(j == pl.num_programs(1) - 1)
    def _():
        y = acc_scr[...]
        if final:
            y = _rms(y, gf_ref[...])
        o_ref[...] = y


def ffn(x2d, g, w_gu_bf, w_down_bf, g_final=None, tm=512):
    m, d = x2d.shape
    f = w_down_bf.shape[0]
    tm = min(tm, m)
    nf = 2 if (f // 2) % LANE == 0 else 1
    tf = f // nf
    final = g_final is not None
    in_specs = [pl.BlockSpec((tm, d), lambda i, j: (i, 0)),
                pl.BlockSpec((1, d), lambda i, j: (0, 0)),
                pl.BlockSpec((d, tf), lambda i, j: (0, j)),
                pl.BlockSpec((d, tf), lambda i, j: (0, j + nf)),
                pl.BlockSpec((tf, d), lambda i, j: (j, 0))]
    args = [x2d, g.reshape(1, d), w_gu_bf, w_gu_bf, w_down_bf]
    if final:
        in_specs.append(pl.BlockSpec((1, d), lambda i, j: (0, 0)))
        args.append(g_final.reshape(1, d))
    return pl.pallas_call(
        functools.partial(_ffn_kernel, final=final),
        grid=(m // tm, nf),
        in_specs=in_specs,
        out_specs=pl.BlockSpec((tm, d), lambda i, j: (i, 0)),
        out_shape=jax.ShapeDtypeStruct((m, d), F32),
        scratch_shapes=[pltpu.VMEM((tm, d), BF16), pltpu.VMEM((tm, d), F32)],
        compiler_params=_cparams("parallel", "arbitrary"),
        name="ffn",
    )(*args)


def _gelu_tanh(x):
    c = math.sqrt(2.0 / math.pi)
    return x * (0.5 * (1.0 + jnp.tanh(c * (x + 0.044715 * (x * x * x)))))


def _transpose8(v):
    row = lax.broadcasted_iota(jnp.int32, (SUBLANE, LANE), 0)
    for s in (4, 2, 1):
        low = (row & s) == 0
        out = list(v)
        for i in range(SUBLANE):
            if i & s == 0:
                out[i] = jnp.where(low, v[i], pltpu.roll(v[i + s], s, 0))
                out[i + s] = jnp.where(low, pltpu.roll(v[i], SUBLANE - s, 0), v[i + s])
        v = out
    return v


def _compress_kernel(x_ref, pe_ref, w1k_ref, w1v_ref, w2_ref, o_ref, *, n_pages):
    hidden = []
    for kv, w1_ref in enumerate((w1k_ref, w1v_ref)):
        slabs = []
        for c in range(kv * NSA_KV_HEADS, (kv + 1) * NSA_KV_HEADS):
            per_d = [[] for _ in range(HEAD_DIM)]
            for p0 in range(0, n_pages, SUBLANE):
                for d0 in range(0, HEAD_DIM, SUBLANE):
                    tiles = [x_ref[pl.ds((p0 + p) * KV_ROW + c * HEAD_DIM + d0, SUBLANE), :] for p in range(SUBLANE)]
                    for r, u in enumerate(_transpose8(tiles)):
                        per_d[d0 + r].append(u)
            rows = [jnp.concatenate(per_d[d], axis=0) for d in range(HEAD_DIM)]
            slabs.append((jnp.concatenate(rows, axis=1) + pe_ref[c:c + 1, :]).astype(BF16))
        h = _gelu_tanh(_dot(jnp.concatenate(slabs, axis=0), w1_ref[...])).astype(BF16)
        hidden += [h[i * n_pages:(i + 1) * n_pages] for i in range(NSA_KV_HEADS)]
    o_ref[...] = _dot(jnp.concatenate(hidden, axis=1), w2_ref[...]).astype(o_ref.dtype)


def compress(pages2d, cw, pages_per_step=64):
    pe4, w1k, w1v, w2 = cw
    n_pages = pages2d.shape[0] // KV_ROW
    pps = math.gcd(pages_per_step, n_pages)
    blocks_per_page = TQ // CMP_BLOCK
    const = lambda a: pl.BlockSpec(a.shape, lambda i: (0, 0))
    return pl.pallas_call(
        functools.partial(_compress_kernel, n_pages=pps),
        grid=(n_pages // pps,),
        in_specs=[pl.BlockSpec((pps * KV_ROW, TQ), lambda i: (i, 0)), const(pe4), const(w1k), const(w1v), const(w2)],
        out_specs=pl.BlockSpec((pps, blocks_per_page * KV_ROW), lambda i: (i, 0)),
        out_shape=jax.ShapeDtypeStruct((n_pages, blocks_per_page * KV_ROW), BF16),
        compiler_params=_cparams("parallel"),
        name="compress",
    )(pages2d, pe4, w1k, w1v, w2)


def _compress_weights(pe_k, w1_k, w2_k, pe_v, w1_v, w2_v):
    nj = TQ // CMP_BLOCK
    nc = 2 * NSA_KV_HEADS
    l, d, h = w1_k.shape

    def first(w1):
        w = jnp.zeros((d, nj, l, nj, h), F32)
        for j in range(nj):
            w = w.at[:, j, :, j, :].set(jnp.transpose(w1, (1, 0, 2)))
        return w.reshape(d * nj * l, nj * h).astype(BF16)

    def pe_row(pe):
        return jnp.broadcast_to(pe.T[:, None, :], (d, nj, l)).reshape(1, d * nj * l)

    w2 = jnp.zeros((nc, nj, h, nj, nc, w2_k.shape[1]), F32)
    for c in range(nc):
        for j in range(nj):
            w2 = w2.at[c, j, :, j, c, :].set(w2_k if c < NSA_KV_HEADS else w2_v)
    pe4 = jnp.concatenate([pe_row(pe_k if c < NSA_KV_HEADS else pe_v) for c in range(nc)], axis=0)
    return pe4, first(w1_k), first(w1_v), w2.reshape(nc * nj * h, nj * nc * w2_k.shape[1]).astype(BF16)


def _q_rows(q_ref, heads, n_rep):
    assert NSA_KV_HEADS * HEAD_DIM == LANE
    nq = q_ref.shape[1]
    lane_half = lax.broadcasted_iota(jnp.int32, (nq, LANE), 1) // HEAD_DIM
    rows = []
    for h in heads:
        g = h // n_rep
        tile = q_ref[0, :, (h // 2) * LANE:(h // 2 + 1) * LANE]
        if h % 2 != g:
            tile = pltpu.roll(tile, HEAD_DIM, 1)
        rows.append(jnp.where(lane_half == g, tile, 0.0))
    q_keys = jnp.concatenate(rows, axis=0)
    return jnp.concatenate([q_keys, jnp.zeros_like(q_keys)], axis=1).astype(BF16)


def _heads_to_rows(o_t, n_rep, pair):
    r0 = 2 * pair
    two = jnp.concatenate([o_t[:, r0 * TQ:(r0 + 1) * TQ], o_t[:, (r0 + 1) * TQ:(r0 + 2) * TQ]], axis=0)
    return two.T


def _banded_kernel(*refs, n_groups, n_rep, window, slopes, has_sink, kv_block):
    if has_sink:
        sink_ref, q_ref, x_ref, xt_ref, o_ref = refs
    else:
        q_ref, x_ref, xt_ref, o_ref = refs
    qt = pl.program_id(1)
    nrow = n_rep * TQ
    n_tiles = window // TQ + 1
    ki = lax.broadcasted_iota(jnp.int32, (TQ, nrow), 0)
    qi = lax.broadcasted_iota(jnp.int32, (TQ, nrow), 1) % TQ
    diff = qi - ki
    diff_f = diff.astype(F32)
    tiles = []
    for j in range(n_tiles):
        kt = qt - (n_tiles - 1 - j)
        ktc = jnp.maximum(kt, 0)
        x = x_ref[0, pl.ds(pl.multiple_of(ktc * TQ, TQ), TQ), :]
        tiles.append((x, ktc, jnp.where(kt >= 0, 0.0, MASK)))
    for g in range(n_groups):
        heads = [g * n_rep + r for r in range(n_rep)]
        qrows = _q_rows(q_ref, heads, n_rep)
        slope_row = _row([slopes[h] for h in heads], TQ)
        bias0 = slope_row * diff_f
        v0 = kv_block * KV_ROW + (NSA_KV_HEADS + g) * HEAD_DIM
        s_list = []
        for j, (x, _, before_start) in enumerate(tiles):
            s = _dot_nt(x, qrows) - bias0 + (before_start - slope_row * float((n_tiles - 1 - j) * TQ))
            if j == 0:
                s = jnp.where(diff <= 0, s, MASK)
            if j == n_tiles - 1:
                s = jnp.where(diff >= 0, s, MASK)
            s_list.append(s)
        m = s_list[0].max(axis=0, keepdims=True)
        for s in s_list[1:]:
            m = jnp.maximum(m, s.max(axis=0, keepdims=True))
        if has_sink:
            sink_row = jnp.concatenate([jnp.full((1, TQ), sink_ref[h], F32) for h in heads], axis=1)
            m = jnp.maximum(m, sink_row)
        p_list = [jnp.exp(s - m) for s in s_list]
        den = p_list[0].sum(axis=0, keepdims=True)
        for p in p_list[1:]:
            den = den + p.sum(axis=0, keepdims=True)
        if has_sink:
            den = den + jnp.exp(sink_row - m)
        v_all = jnp.concatenate([xt_ref[0, ktc, v0:v0 + HEAD_DIM, :] for _, ktc, _ in tiles], axis=1)
        p_all = jnp.concatenate([p.astype(BF16) for p in p_list], axis=0)
        o_t = _dot(v_all, p_all) / jnp.maximum(den, 1e-30)
        for pair in range(n_rep // 2):
            h0 = heads[2 * pair]
            o_ref[0, :, h0 * HEAD_DIM:(h0 + 2) * HEAD_DIM] = _heads_to_rows(o_t, n_rep, pair)


def banded_attention(q_pad, x_rows, x_pages, kv_block, n_groups, n_rep, window, sinks=None):
    b, seq = q_pad.shape[:2]
    n_heads = n_groups * n_rep
    has_sink = sinks is not None
    in_specs = [pl.BlockSpec((1, TQ, n_heads * HEAD_DIM), lambda i, j: (i, j, 0)),
                pl.BlockSpec((1, seq, KV_ROW), lambda i, j: (i, 0, kv_block)),
                pl.BlockSpec((1,) + x_pages.shape[1:], lambda i, j: (i, 0, 0, 0))]
    args = [q_pad, x_rows, x_pages]
    if has_sink:
        in_specs.insert(0, pl.BlockSpec(memory_space=pltpu.SMEM))
        args.insert(0, sinks)
    assert window % TQ == 0 and window >= TQ
    return pl.pallas_call(
        functools.partial(_banded_kernel, n_groups=n_groups, n_rep=n_rep, window=window,
                          slopes=_alibi(n_heads), has_sink=has_sink, kv_block=kv_block),
        grid=(b, seq // TQ),
        in_specs=in_specs,
        out_specs=pl.BlockSpec((1, TQ, n_heads * HEAD_DIM), lambda i, j: (i, j, 0)),
        out_shape=jax.ShapeDtypeStruct((b, seq, n_heads * HEAD_DIM), F32),
        compiler_params=_cparams("parallel", "parallel"),
        name="banded_attention",
    )(*args)


def _banded_decode_kernel(*refs, n_groups, n_rep, window, slopes, has_sink, pos0):
    if has_sink:
        sink_ref, q_ref, st_ref, new_ref, o_ref = refs
    else:
        q_ref, st_ref, new_ref, o_ref = refs
    nq = q_ref.shape[1]
    npast = st_ref.shape[2]
    nrow = n_rep * nq
    xs_t = st_ref[0].astype(BF16)
    xn = jnp.concatenate([new_ref[0], jnp.zeros((TQ - nq, KV_ROW), F32)], axis=0).astype(BF16)
    qi_s = jnp.concatenate([lax.broadcasted_iota(jnp.int32, (nq, npast), 0)] * n_rep, axis=0)
    kj_s = lax.broadcasted_iota(jnp.int32, (nrow, npast), 1)
    dist_s = npast + qi_s - kj_s
    valid_s = (dist_s >= 0) & (dist_s <= window) & (pos0 - npast + kj_s >= 0)
    qi_n = jnp.concatenate([lax.broadcasted_iota(jnp.int32, (nq, TQ), 0)] * n_rep, axis=0)
    kj_n = lax.broadcasted_iota(jnp.int32, (nrow, TQ), 1)
    dist_n = qi_n - kj_n
    valid_n = (dist_n >= 0) & (dist_n <= window) & (kj_n < nq)
    for g in range(n_groups):
        heads = [g * n_rep + r for r in range(n_rep)]
        qrows = _q_rows(q_ref, heads, n_rep)
        slope_col = _col([slopes[h] for h in heads], nq)
        s_s = jnp.where(valid_s, _dot(qrows, xs_t) - slope_col * dist_s.astype(F32), NEG)
        s_n = jnp.where(valid_n, _dot_nt(qrows, xn) - slope_col * dist_n.astype(F32), NEG)
        m = jnp.maximum(jnp.max(s_s, axis=-1, keepdims=True), jnp.max(s_n, axis=-1, keepdims=True))
        if has_sink:
            sink_col = jnp.concatenate([jnp.full((nq, 1), sink_ref[h], F32) for h in heads], axis=0)
            m = jnp.maximum(m, sink_col)
        p_s = jnp.where(valid_s, jnp.exp(s_s - m), 0.0)
        p_n = jnp.where(valid_n, jnp.exp(s_n - m), 0.0)
        den = jnp.sum(p_s, axis=-1, keepdims=True) + jnp.sum(p_n, axis=-1, keepdims=True)
        if has_sink:
            den = den + jnp.exp(sink_col - m)
        o = (_dot_nt(p_s.astype(BF16), xs_t) + _dot(p_n.astype(BF16), xn)) / jnp.maximum(den, 1e-30)
        v0 = (NSA_KV_HEADS + g) * HEAD_DIM
        for r, h in enumerate(heads):
            o_ref[0, :, h * HEAD_DIM:(h + 1) * HEAD_DIM] = o[r * nq:(r + 1) * nq, v0:v0 + HEAD_DIM]


def banded_decode(q_pad, state_t, new_kv, n_groups, n_rep, window, pos0, sinks=None):
    b, nq = q_pad.shape[:2]
    npast = state_t.shape[2]
    n_heads = n_groups * n_rep
    has_sink = sinks is not None
    in_specs = [pl.BlockSpec((1, nq, n_heads * HEAD_DIM), lambda i: (i, 0, 0)),
                pl.BlockSpec((1, KV_ROW, npast), lambda i: (i, 0, 0)),
                pl.BlockSpec((1, nq, KV_ROW), lambda i: (i, 0, 0))]
    args = [q_pad, state_t, new_kv]
    if has_sink:
        in_specs.insert(0, pl.BlockSpec(memory_space=pltpu.SMEM))
        args.insert(0, sinks)
    return pl.pallas_call(
        functools.partial(_banded_decode_kernel, n_groups=n_groups, n_rep=n_rep, window=window,
                          slopes=_alibi(n_heads), has_sink=has_sink, pos0=pos0),
        grid=(b,),
        in_specs=in_specs,
        out_specs=pl.BlockSpec((1, nq, n_heads * HEAD_DIM), lambda i: (i, 0, 0)),
        out_shape=jax.ShapeDtypeStruct((b, nq, n_heads * HEAD_DIM), F32),
        compiler_params=_cparams("parallel"),
        name="banded_decode",
    )(*args)


def _select_blocks_t(imp, cur, n_sel):
    nb = imp.shape[0]
    nrow = lax.broadcasted_iota(jnp.int32, imp.shape, 0)
    count = jnp.zeros(imp.shape, F32)
    for i in range(nb):
        ci = imp[i:i + 1, :]
        ahead = jnp.where(ci > imp, 1.0, jnp.where(ci == imp, jnp.where(nrow > i, 1.0, 0.0), 0.0))
        count = count + jnp.where(cur > i, ahead, 0.0)
    return jnp.where(nrow < cur, jnp.where(count < n_sel - 1, 1.0, 0.0), jnp.where(nrow == cur, 1.0, 0.0))


def _nsa_prompt_kernel(q_ref, gd_ref, kc_ref, kct_ref, x_ref, xt_ref, ow_ref, o_ref, m_scr, l_scr, acc_scr,
                       *, n_sel, kv_block):
    qt = pl.program_id(1)
    n_rep = NSA_GROUP
    nrow = n_rep * TQ
    slopes = _alibi(NSA_HEADS)
    nb = kc_ref.shape[1]
    kcb = kc_ref[0]
    gates = _sigmoid(gd_ref[0])
    tk = 2 * TQ
    n_seq_tiles = x_ref.shape[1] // TQ
    ki = lax.broadcasted_iota(jnp.int32, (tk, nrow), 0)
    qi = lax.broadcasted_iota(jnp.int32, (tk, nrow), 1) % TQ
    diff = qi - ki
    tpos = qt * TQ + lax.broadcasted_iota(jnp.int32, (nb, nrow), 1) % TQ
    blk = lax.broadcasted_iota(jnp.int32, (nb, nrow), 0)
    dist_c = tpos - ((blk + 1) * CMP_BLOCK - 1)
    valid_c = dist_c >= 0
    cur = (qt * TQ + lax.broadcasted_iota(jnp.int32, (nb, TQ), 1)) // CMP_BLOCK
    key_blk = lax.broadcasted_iota(jnp.int32, (tk, nb), 0) // CMP_BLOCK
    blk_col = lax.broadcasted_iota(jnp.int32, (tk, nb), 1)
    diff_f = diff.astype(F32)
    groups = []
    for g in range(NSA_KV_HEADS):
        heads = [g * n_rep + r for r in range(n_rep)]
        qrows = _q_rows(q_ref, heads, n_rep)
        slope_row = _row([slopes[h] for h in heads], TQ)
        vrow = (NSA_KV_HEADS + g) * HEAD_DIM
        s_c = jnp.where(valid_c, _dot_nt(kcb, qrows) - slope_row * dist_c.astype(F32), NEG)
        e_c = jnp.where(valid_c, jnp.exp(s_c - jnp.max(s_c, axis=0, keepdims=True)), 0.0)
        p_c = e_c / jnp.maximum(jnp.sum(e_c, axis=0, keepdims=True), 1e-30)
        ocmp_t = _dot(kct_ref[0, vrow:vrow + HEAD_DIM, :], p_c.astype(BF16))
        imp = p_c[:, 0:TQ]
        for r in range(1, n_rep):
            imp = imp + p_c[:, r * TQ:(r + 1) * TQ]
        taken = _select_blocks_t(imp, cur, n_sel)
        not_taken = jnp.where(taken > 0.5, 0.0, MASK).astype(BF16)
        m_scr[g] = jnp.full((1, nrow), NEG, F32)
        l_scr[g] = jnp.zeros((1, nrow), F32)
        acc_scr[g] = jnp.zeros((HEAD_DIM, nrow), F32)
        groups.append((heads, qrows, slope_row, slope_row * diff_f, kv_block * KV_ROW + vrow, not_taken, ocmp_t))

    def chunk(kt0, causal):
        kt1 = jnp.minimum(kt0 + 1, n_seq_tiles - 1)
        x = jnp.concatenate([x_ref[0, pl.ds(pl.multiple_of(kt0 * TQ, TQ), TQ), :],
                             x_ref[0, pl.ds(pl.multiple_of(kt1 * TQ, TQ), TQ), :]], axis=0)
        off = (qt - kt0) * TQ
        expand = jnp.where(blk_col == key_blk + kt0 * (TQ // CMP_BLOCK), 1.0, 0.0).astype(BF16)
        for g, (_, qrows, slope_row, bias0, v0, not_taken, _) in enumerate(groups):
            s = _dot_nt(x, qrows) - bias0 + jnp.concatenate([_dot(expand, not_taken)] * n_rep, axis=1)
            if causal:
                s = jnp.where(diff + off >= 0, s, MASK)
            shift = slope_row * off.astype(F32)
            m_old = m_scr[g]
            m_new = jnp.maximum(m_old, jnp.max(s, axis=0, keepdims=True) - shift)
            alpha = jnp.exp(m_old - m_new)
            p = jnp.exp(s - (m_new + shift))
            l_scr[g] = alpha * l_scr[g] + jnp.sum(p, axis=0, keepdims=True)
            v_t = jnp.concatenate([xt_ref[0, kt0, v0:v0 + HEAD_DIM, :], xt_ref[0, kt1, v0:v0 + HEAD_DIM, :]], axis=1)
            acc_scr[g] = alpha * acc_scr[g] + _dot(v_t, p.astype(BF16))
            m_scr[g] = m_new

    n_full = qt // 2

    def body(j, carry):
        chunk(2 * j, False)
        return carry

    lax.fori_loop(0, n_full, body, 0)
    chunk(2 * n_full, True)
    for g, (heads, _, _, _, _, _, ocmp_t) in enumerate(groups):
        oslc_t = acc_scr[g] / jnp.maximum(l_scr[g], 1e-30)
        for pair in range(n_rep // 2):
            h0 = heads[2 * pair]
            hh = slice(h0 * HEAD_DIM, (h0 + 2) * HEAD_DIM)
            gate = lambda j: _pair_lanes(gates[:, 3 * h0 + j:3 * h0 + j + 1], gates[:, 3 * h0 + 3 + j:3 * h0 + 4 + j])
            o_ref[0, :, hh] = (gate(0) * _heads_to_rows(ocmp_t, n_rep, pair)
                               + gate(1) * _heads_to_rows(oslc_t, n_rep, pair)
                               + gate(2) * ow_ref[0, :, hh])


def nsa_prompt(q_pad, gd, kc, x_rows, x_pages, kv_block, o_win):
    b, seq = q_pad.shape[:2]
    nb = kc.shape[1]
    nrow = NSA_GROUP * TQ
    kct = jnp.swapaxes(kc, 1, 2)
    return pl.pallas_call(
        functools.partial(_nsa_prompt_kernel, n_sel=min(SEL_TOPK, nb), kv_block=kv_block),
        grid=(b, seq // TQ),
        in_specs=[pl.BlockSpec((1, TQ, NSA_WIDTH), lambda i, j: (i, j, 0)),
                  pl.BlockSpec((1, TQ, LANE), lambda i, j: (i, j, 0)),
                  pl.BlockSpec((1, nb, KV_ROW), lambda i, j: (i, 0, 0)),
                  pl.BlockSpec((1, KV_ROW, nb), lambda i, j: (i, 0, 0)),
                  pl.BlockSpec((1, seq, KV_ROW), lambda i, j: (i, 0, kv_block)),
                  pl.BlockSpec((1,) + x_pages.shape[1:], lambda i, j: (i, 0, 0, 0)),
                  pl.BlockSpec((1, TQ, NSA_WIDTH), lambda i, j: (i, j, 0))],
        out_specs=pl.BlockSpec((1, TQ, NSA_WIDTH), lambda i, j: (i, j, 0)),
        out_shape=jax.ShapeDtypeStruct((b, seq, NSA_WIDTH), F32),
        scratch_shapes=[pltpu.VMEM((NSA_KV_HEADS, 1, nrow), F32), pltpu.VMEM((NSA_KV_HEADS, 1, nrow), F32),
                        pltpu.VMEM((NSA_KV_HEADS, HEAD_DIM, nrow), F32)],
        compiler_params=_cparams("parallel", "parallel"),
        name="nsa_prompt",
    )(q_pad, gd, kc, kct, x_rows, x_pages, o_win)


def _select_blocks(imp, n_sel):
    nb = imp.shape[1]
    ncol = lax.broadcasted_iota(jnp.int32, imp.shape, 1)
    count = jnp.zeros(imp.shape, F32)
    for i in range(nb):
        ci = imp[:, i:i + 1]
        count = count + jnp.where(ci > imp, 1.0, jnp.where(ci == imp, jnp.where(ncol > i, 1.0, 0.0), 0.0))
    return jnp.where(count < n_sel - 1, 1.0, 0.0)


def _nsa_decode_kernel(pt_ref, q_ref, gd_ref, kc_ref, new_ref, ow_ref, ex_ref, *rest, pages_per_step, pos0, n_sel):
    page_refs = rest[:pages_per_step]
    o_ref, m_scr, l_scr, acc_scr, mask_scr, bias_scr, ocmp_scr, q_scr = rest[pages_per_step:]
    p = pl.program_id(1)
    n_rep = NSA_GROUP
    nq = q_ref.shape[1]
    ngrp = n_rep * nq
    nrow = NSA_KV_HEADS * ngrp
    nb = kc_ref.shape[1]
    page = page_refs[0].shape[1]
    nkey = pages_per_step * page
    slopes = _alibi(NSA_HEADS)
    slope_col = _col(slopes, nq)

    def per_query(a):
        return jnp.concatenate([a[g] for g in range(NSA_KV_HEADS) for _ in range(n_rep)], axis=0)

    @pl.when(p == 0)
    def _():
        q_scr[...] = jnp.concatenate([_q_rows(q_ref, [g * n_rep + r for r in range(n_rep)], n_rep)
                                      for g in range(NSA_KV_HEADS)], axis=0)

    qrows = q_scr[...]

    @pl.when(p == 0)
    def _():
        kcb = kc_ref[0]
        tpos = pos0 + jnp.concatenate([lax.broadcasted_iota(jnp.int32, (nq, nb), 0)] * (nrow // nq), axis=0)
        ncol = lax.broadcasted_iota(jnp.int32, (nrow, nb), 1)
        dist = tpos - ((ncol + 1) * CMP_BLOCK - 1)
        valid = dist >= 0
        s = jnp.where(valid, _dot_nt(qrows, kcb) - slope_col * dist.astype(F32), NEG)
        e = jnp.where(valid, jnp.exp(s - jnp.max(s, axis=-1, keepdims=True)), 0.0)
        pc = e / jnp.maximum(jnp.sum(e, axis=-1, keepdims=True), 1e-30)
        ocmp_scr[...] = _dot(pc.astype(BF16), kcb)
        not_taken = []
        for g in range(NSA_KV_HEADS):
            imp = pc[g * ngrp:g * ngrp + nq]
            for r in range(1, n_rep):
                imp = imp + pc[g * ngrp + r * nq:g * ngrp + (r + 1) * nq]
            not_taken.append(jnp.where(_select_blocks(imp, n_sel) > 0.5, 0.0, MASK))
        not_taken = jnp.concatenate(not_taken, axis=0).astype(BF16)
        for st in range(mask_scr.shape[0]):
            mask_scr[st] = _dot(not_taken, ex_ref[:, st * nkey:(st + 1) * nkey])
        qi = jnp.concatenate([lax.broadcasted_iota(jnp.int32, (nq, nkey), 0)] * (nrow // nq), axis=0)
        kj = lax.broadcasted_iota(jnp.int32, (nrow, nkey), 1)
        bias_scr[...] = slope_col * (pos0 + qi - kj).astype(F32)
        m_scr[...] = jnp.full_like(m_scr, NEG)
        l_scr[...] = jnp.zeros_like(l_scr)
        acc_scr[...] = jnp.zeros_like(acc_scr)

    mask = mask_scr[p]
    shift = slope_col * (p * nkey).astype(F32)
    per = pages_per_step // N_STREAMS
    for st in range(N_STREAMS):
        ks = slice(st * per * page, (st + 1) * per * page)
        x_t = jnp.concatenate([r[...] for r in page_refs[st * per:(st + 1) * per]], axis=1).astype(BF16)
        s = (_dot(qrows, x_t) - bias_scr[:, ks]
             + per_query([mask[g * nq:(g + 1) * nq, ks] for g in range(NSA_KV_HEADS)]))
        m_old = m_scr[st]
        m_new = jnp.maximum(m_old, jnp.max(s, axis=-1, keepdims=True) + shift)
        alpha = jnp.exp(m_old - m_new)
        pr = jnp.exp(s - (m_new - shift))
        l_scr[st] = alpha * l_scr[st] + jnp.sum(pr, axis=-1, keepdims=True)
        acc_scr[st] = alpha * acc_scr[st] + _dot_nt(pr.astype(BF16), x_t)
        m_scr[st] = m_new

    @pl.when(p == pl.num_programs(1) - 1)
    def _():
        xn = jnp.concatenate([new_ref[0], jnp.zeros((TQ - nq, KV_ROW), F32)], axis=0).astype(BF16)
        qi_n = jnp.concatenate([lax.broadcasted_iota(jnp.int32, (nq, TQ), 0)] * (nrow // nq), axis=0)
        kj_n = lax.broadcasted_iota(jnp.int32, (nrow, TQ), 1)
        dist_n = qi_n - kj_n
        valid_n = (dist_n >= 0) & (kj_n < nq)
        s_n = jnp.where(valid_n, _dot_nt(qrows, xn) - slope_col * dist_n.astype(F32), NEG)
        m_new = jnp.max(s_n, axis=-1, keepdims=True)
        for st in range(N_STREAMS):
            m_new = jnp.maximum(m_new, m_scr[st])
        p_n = jnp.where(valid_n, jnp.exp(s_n - m_new), 0.0)
        den = jnp.sum(p_n, axis=-1, keepdims=True)
        num = _dot(p_n.astype(BF16), xn)
        for st in range(N_STREAMS):
            alpha = jnp.exp(m_scr[st] - m_new)
            den = den + alpha * l_scr[st]
            num = num + alpha * acc_scr[st]
        o_slc = num / jnp.maximum(den, 1e-30)
        o_cmp = ocmp_scr[...]
        gates = _sigmoid(gd_ref[0])
        for h in range(NSA_HEADS):
            g = h // n_rep
            v0 = (NSA_KV_HEADS + g) * HEAD_DIM
            rr = slice(h * nq, (h + 1) * nq)
            hh = slice(h * HEAD_DIM, (h + 1) * HEAD_DIM)
            o_ref[0, :, hh] = (gates[:, 3 * h:3 * h + 1] * o_cmp[rr, v0:v0 + HEAD_DIM]
                               + gates[:, 3 * h + 1:3 * h + 2] * o_slc[rr, v0:v0 + HEAD_DIM]
                               + gates[:, 3 * h + 2:3 * h + 3] * ow_ref[0, :, hh])


def nsa_decode(page_table, q_pad, gd, kc, new_kv, o_win, pool_t, pos0, pages_per_step=16):
    b, nq = q_pad.shape[:2]
    nb = kc.shape[1]
    n_pages = page_table.shape[1]
    page = pool_t.shape[2]
    pps = math.gcd(pages_per_step, n_pages)
    assert pps % N_STREAMS == 0
    n_steps = n_pages // pps
    nkey = pps * page
    nrow = NSA_HEADS * nq
    expand = (jnp.arange(n_pages * page)[None, :] // CMP_BLOCK == jnp.arange(nb)[:, None]).astype(BF16)

    def page_spec(j):
        return pl.BlockSpec((None, KV_ROW, page), lambda i, p, pt: (pt[i, p * pps + j], 0, 0))

    grid_spec = pltpu.PrefetchScalarGridSpec(
        num_scalar_prefetch=1,
        grid=(b, n_steps),
        in_specs=[pl.BlockSpec((1, nq, NSA_WIDTH), lambda i, p, pt: (i, 0, 0)),
                  pl.BlockSpec((1, nq, LANE), lambda i, p, pt: (i, 0, 0)),
                  pl.BlockSpec((1, nb, KV_ROW), lambda i, p, pt: (i, 0, 0)),
                  pl.BlockSpec((1, nq, KV_ROW), lambda i, p, pt: (i, 0, 0)),
                  pl.BlockSpec((1, nq, NSA_WIDTH), lambda i, p, pt: (i, 0, 0)),
                  pl.BlockSpec(expand.shape, lambda i, p, pt: (0, 0))]
                 + [page_spec(j) for j in range(pps)],
        out_specs=pl.BlockSpec((1, nq, NSA_WIDTH), lambda i, p, pt: (i, 0, 0)),
        scratch_shapes=[pltpu.VMEM((N_STREAMS, nrow, 1), F32), pltpu.VMEM((N_STREAMS, nrow, 1), F32),
                        pltpu.VMEM((N_STREAMS, nrow, KV_ROW), F32),
                        pltpu.VMEM((n_steps, NSA_KV_HEADS * nq, nkey), F32),
                        pltpu.VMEM((nrow, nkey), F32),
                        pltpu.VMEM((nrow, KV_ROW), F32),
                        pltpu.VMEM((nrow, KV_ROW), BF16)])
    return pl.pallas_call(
        functools.partial(_nsa_decode_kernel, pages_per_step=pps, pos0=pos0, n_sel=min(SEL_TOPK, nb + 1)),
        grid_spec=grid_spec,
        out_shape=jax.ShapeDtypeStruct((b, nq, NSA_WIDTH), F32),
        compiler_params=_cparams("parallel", "arbitrary"),
        name="nsa_decode",
    )(page_table, q_pad, gd, kc, new_kv, o_win, expand, *([pool_t] * pps))


def _cumsum_rows(v):
    n = v.shape[0]
    ri = lax.broadcasted_iota(jnp.int32, v.shape, 0)
    sh = 1
    while sh < n:
        v = v + jnp.where(ri >= sh, pltpu.roll(v, sh, 0), 0.0)
        sh *= 2
    return v


def _ssd_kernel(z_ref, xbc_ref, gd_ref, tail_ref, s0_ref, cw_ref, cb_ref, dtb_ref, alog_ref, dsk_ref, ng_ref,
                y_ref, sfin_ref, xp_scr, st_scr, y_scr, *, n_valid):
    c = pl.program_id(1)
    q = xbc_ref.shape[1]

    @pl.when(c == 0)
    def _():
        xp_scr[0:SUBLANE, :] = tail_ref[0]
        st_scr[...] = s0_ref[0]

    xp_scr[SUBLANE:SUBLANE + q, :] = xbc_ref[0]
    acc = cb_ref[...]
    for k in range(SSM_CONV):
        lo = SUBLANE - (SSM_CONV - 1) + k
        acc = acc + xp_scr[lo:lo + q, :] * cw_ref[k:k + 1, :]
    nxt = xp_scr[q:q + SUBLANE, :]
    xp_scr[0:SUBLANE, :] = nxt
    act = acc * _sigmoid(acc)

    raw = gd_ref[0] + dtb_ref[...]
    dt = jnp.maximum(raw, 0.0) + jnp.log1p(jnp.exp(-jnp.abs(raw)))
    if n_valid < q:
        dt = jnp.where(lax.broadcasted_iota(jnp.int32, dt.shape, 0) < n_valid, dt, 0.0)
    acum = _cumsum_rows(dt * (-jnp.exp(alog_ref[...])))
    acum_t = acum.T
    ri = lax.broadcasted_iota(jnp.int32, (q, q), 0)
    ci = lax.broadcasted_iota(jnp.int32, (q, q), 1)
    causal = ri >= ci
    half = lax.broadcasted_iota(jnp.int32, (2 * SSM_HEAD_DIM, 1), 0) < SSM_HEAD_DIM
    hpg = SSM_HEADS // SSM_GROUPS
    for pair in range(SSM_HEADS // 2):
        grp = (2 * pair) // hpg
        lanes = slice(pair * LANE, (pair + 1) * LANE)
        bm = act[:, SSM_D_INNER + grp * SSM_STATE:SSM_D_INNER + (grp + 1) * SSM_STATE].astype(BF16)
        cm_lo = SSM_D_INNER + SSM_GROUPS * SSM_STATE + grp * SSM_STATE
        cm = act[:, cm_lo:cm_lo + SSM_STATE].astype(BF16)
        cb = _dot_nt(cm, bm)
        xs = act[:, lanes]
        cols = []
        for h in (2 * pair, 2 * pair + 1):
            ln = DT_LANE + h
            cols.append((acum[:, ln:ln + 1], acum_t[ln:ln + 1, :], acum[q - 1:q, ln:ln + 1], dt[:, ln:ln + 1]))
        xdt = xs * _pair_lanes(cols[0][3], cols[1][3])
        xdt_bf = xdt.astype(BF16)
        y_parts = []
        for a_col, a_row, _, _ in cols:
            seg = a_col - a_row
            lmat = jnp.where(causal, jnp.exp(jnp.where(causal, seg, 0.0)), 0.0)
            y_parts.append(_dot((cb * lmat).astype(BF16), xdt_bf))
        lane = lax.broadcasted_iota(jnp.int32, (q, LANE), 1)
        y_diag = jnp.where(lane < SSM_HEAD_DIM, y_parts[0], y_parts[1])
        st = st_scr[pair * LANE:(pair + 1) * LANE, :]
        y_off = _dot_nt(cm, st.astype(BF16)) * _pair_lanes(jnp.exp(cols[0][0]), jnp.exp(cols[1][0]))
        dec_end = _pair_lanes(jnp.exp(cols[0][2] - cols[0][0]), jnp.exp(cols[1][2] - cols[1][0]))
        cs = _dot((xdt * dec_end).T.astype(BF16), bm)
        st_scr[pair * LANE:(pair + 1) * LANE, :] = st * jnp.where(half, jnp.exp(cols[0][2]), jnp.exp(cols[1][2])) + cs
        dsk = _pair_lanes(jnp.zeros((q, 1), F32) + dsk_ref[:, 2 * pair:2 * pair + 1],
                          jnp.zeros((q, 1), F32) + dsk_ref[:, 2 * pair + 1:2 * pair + 2])
        y_scr[:, lanes] = y_diag + y_off + dsk * xs

    z = z_ref[0]
    y_ref[0] = _rms(y_scr[...] * (z * _sigmoid(z)), ng_ref[...])

    @pl.when(c == pl.num_programs(1) - 1)
    def _():
        sfin_ref[0] = st_scr[...]


def ssd_mixer(z, xbc, gd, tail8, s0, conv_w, conv_b, dt_bias, a_log, d_skip, norm_g, n_valid):
    b, seq = z.shape[:2]
    nst = SSM_HEADS * SSM_HEAD_DIM

    def lane_piece(v):
        return jnp.zeros((1, LANE), F32).at[0, DT_LANE:DT_LANE + SSM_HEADS].set(v)

    alog_piece = jnp.full((1, LANE), -100.0, F32).at[0, DT_LANE:DT_LANE + SSM_HEADS].set(a_log)
    full2 = lambda shape: pl.BlockSpec(shape, lambda i, c: (0, 0))
    return pl.pallas_call(
        functools.partial(_ssd_kernel, n_valid=n_valid),
        grid=(b, seq // TQ),
        in_specs=[pl.BlockSpec((1, TQ, SSM_D_INNER), lambda i, c: (i, c, 0)),
                  pl.BlockSpec((1, TQ, SSM_CONV_DIM), lambda i, c: (i, c, 0)),
                  pl.BlockSpec((1, TQ, LANE), lambda i, c: (i, c, 0)),
                  pl.BlockSpec((1, SUBLANE, SSM_CONV_DIM), lambda i, c: (i, 0, 0)),
                  pl.BlockSpec((1, nst, SSM_STATE), lambda i, c: (i, 0, 0)),
                  full2((SSM_CONV, SSM_CONV_DIM)), full2((1, SSM_CONV_DIM)), full2((1, LANE)),
                  full2((1, LANE)), full2((1, SSM_HEADS)), full2((1, SSM_D_INNER))],
        out_specs=[pl.BlockSpec((1, TQ, SSM_D_INNER), lambda i, c: (i, c, 0)),
                   pl.BlockSpec((1, nst, SSM_STATE), lambda i, c: (i, 0, 0))],
        out_shape=[jax.ShapeDtypeStruct((b, seq, SSM_D_INNER), F32),
                   jax.ShapeDtypeStruct((b, nst, SSM_STATE), F32)],
        scratch_shapes=[pltpu.VMEM((SUBLANE + TQ, SSM_CONV_DIM), F32),
                        pltpu.VMEM((nst, SSM_STATE), F32),
                        pltpu.VMEM((TQ, SSM_D_INNER), F32)],
        compiler_params=_cparams("parallel", "arbitrary"),
        name="ssd_mixer",
    )(z, xbc, gd, tail8, s0, conv_w, conv_b.reshape(1, -1), lane_piece(dt_bias), alog_piece,
      d_skip.reshape(1, -1), norm_g.reshape(1, -1))


def _channel_major(a6):
    b, rows = a6.shape[:2]
    return jnp.transpose(a6, (0, 2, 3, 4, 1)).reshape(b, KV_ROW, rows)


def _token_major(a_t):
    b, _, rows = a_t.shape
    return jnp.transpose(a_t.reshape(b, 2, NSA_KV_HEADS, HEAD_DIM, rows), (0, 4, 1, 2, 3))


def _even_weights(w_in):
    d = w_in.shape[0]
    o = np.cumsum([0, NSA_WIDTH, KV_ROW, KV_ROW, KV_ROW, 3 * NSA_HEADS, SSM_D_INNER, SSM_CONV_DIM, SSM_HEADS])
    q, kvc, kvs, kvw, gt, z, xbc, dtr = (w_in[:, o[i]:o[i + 1]] for i in range(8))
    gd = jnp.zeros((d, LANE), F32).at[:, :3 * NSA_HEADS].set(gt).at[:, DT_LANE:DT_LANE + SSM_HEADS].set(dtr)
    w = jnp.concatenate([q * SCALE, kvc, kvs, kvw, z, xbc, gd], axis=1).astype(BF16)
    qw = NSA_WIDTH
    off = {"q": (0, qw), "kvc": (qw, KV_ROW), "kvs": (qw + KV_ROW, KV_ROW), "kvw": (qw + 2 * KV_ROW, KV_ROW),
           "kv3": (qw, 3 * KV_ROW), "z": (qw + 3 * KV_ROW, SSM_D_INNER),
           "xbc": (qw + 3 * KV_ROW + SSM_D_INNER, SSM_CONV_DIM),
           "gd": (qw + 3 * KV_ROW + SSM_D_INNER + SSM_CONV_DIM, LANE)}
    return w, off


def _even_layer(x, past, p, cache):
    (norm_g, w_in, w_out, cw, ssm_p) = p
    b, seq, d = x.shape
    m = b * seq
    decode = cache is not None
    w, off = _even_weights(w_in)
    conv_w, conv_b, dt_bias, a_log, d_skip, ssm_norm = ssm_p
    if not decode:
        names = [("row", "q", F32), ("chan", "kvc", F32), ("chan", "kvs", F32), ("chan", "kvw", F32),
                 ("page", "kvc", F32), ("row", "kv3", BF16), ("page", "kv3", BF16),
                 ("row", "z", F32), ("row", "xbc", F32), ("row", "gd", F32)]
        q, kvc_t, kvs_t, kvw_t, kvc_pages, x_rows, x_pages, z, xbc, gd = norm_proj(
            x, norm_g, w, [(k,) + off[n] for k, n, _ in names], [dt for _, _, dt in names])
        q = q.reshape(b, seq, -1)
        gd3 = gd.reshape(b, seq, LANE)
        x_rows = x_rows.reshape(b, seq, 3 * KV_ROW)
        x_pages = x_pages.reshape(b, seq // TQ, 3 * KV_ROW, TQ)
        kc = compress(kvc_pages.reshape(-1, TQ), cw).reshape(b, seq // CMP_BLOCK, KV_ROW)
        o_win = banded_attention(q, x_rows, x_pages, 2, NSA_KV_HEADS, NSA_GROUP, NSA_WINDOW)
        o_nsa = nsa_prompt(q, gd3, kc, x_rows, x_pages, 1, o_win)
        tail8 = jnp.zeros((b, SUBLANE, SSM_CONV_DIM), F32)
        s0 = jnp.zeros((b, SSM_D_INNER, SSM_STATE), F32)
        xbc3 = xbc.reshape(b, seq, -1)
        o_ssm, s_fin = ssd_mixer(z.reshape(b, seq, -1), xbc3, gd3, tail8, s0,
                                 conv_w, conv_b, dt_bias, a_log, d_skip, ssm_norm, TQ)
        conv_new = xbc3[:, seq - (SSM_CONV - 1):]
        kv_outs = (_token_major(kvc_t), _token_major(kvs_t), _token_major(kvw_t[:, :, seq - min(NSA_WINDOW, seq):]))
    else:
        cmp_pool_t, slc_pool_t, page_table, win_state_t, conv0, ssm0 = cache
        n_pool, _, page = cmp_pool_t.shape
        assert page == TQ and seq < CMP_BLOCK and TQ % seq == 0
        names = [("row", "q", F32), ("row", "kvc", F32), ("row", "kvs", F32), ("row", "kvw", F32),
                 ("row", "z", F32), ("row", "xbc", F32), ("row", "gd", F32)]
        q, kvc, kvs, kvw, z, xbc, gd = norm_proj(
            x, norm_g, w, [(k,) + off[n] for k, n, _ in names], [dt for _, _, dt in names])
        q = q.reshape(b, seq, -1)
        gd3 = gd.reshape(b, seq, LANE)
        bpp = page // CMP_BLOCK
        kc_pool = compress(cmp_pool_t.reshape(n_pool * KV_ROW, page), cw)
        kc = kc_pool.reshape(n_pool, bpp, KV_ROW)[page_table].reshape(b, past // CMP_BLOCK, KV_ROW)
        kvw3 = kvw.reshape(b, seq, KV_ROW)
        o_win = banded_decode(q, win_state_t, kvw3, NSA_KV_HEADS, NSA_GROUP, NSA_WINDOW, past)
        o_nsa = nsa_decode(page_table, q, gd3, kc, kvs.reshape(b, seq, KV_ROW), o_win, slc_pool_t, past)
        padr = lambda a: jnp.pad(a.reshape(b, seq, -1), ((0, 0), (0, TQ - seq), (0, 0)))
        tail8 = jnp.pad(conv0, ((0, 0), (SUBLANE - (SSM_CONV - 1), 0), (0, 0)))
        o_ssm, s_fin = ssd_mixer(padr(z), padr(xbc), padr(gd), tail8, ssm0.reshape(b, SSM_D_INNER, SSM_STATE),
                                 conv_w, conv_b, dt_bias, a_log, d_skip, ssm_norm, seq)
        o_ssm = o_ssm[:, :seq]
        conv_new = jnp.concatenate([conv0, xbc.reshape(b, seq, -1)], axis=1)[:, -(SSM_CONV - 1):]
        npast = win_state_t.shape[2]
        win_out = jnp.concatenate([win_state_t, jnp.swapaxes(kvw3, 1, 2)], axis=2)[:, :, -npast:]
        kv6 = (b, seq, 2, NSA_KV_HEADS, HEAD_DIM)
        kv_outs = (kvc.reshape(kv6), kvs.reshape(kv6), _token_major(win_out))
    x = proj_res([o_nsa.reshape(m, -1), o_ssm.reshape(m, -1)],
                 [w_out[:NSA_WIDTH].astype(BF16), w_out[NSA_WIDTH:].astype(BF16)], x.reshape(m, d))
    outs = kv_outs + (s_fin.reshape(b, SSM_HEADS, SSM_HEAD_DIM, SSM_STATE), conv_new)
    return x.reshape(b, seq, d), outs


def _odd_layer(x, past, p, swa_state_t):
    norm_g, w_in, w_out, sinks = p
    b, seq, d = x.shape
    m = b * seq
    decode = swa_state_t is not None
    w = jnp.concatenate([w_in[:, :SWA_WIDTH] * SCALE, w_in[:, SWA_WIDTH:]], axis=1).astype(BF16)
    qw = SWA_WIDTH
    if not decode:
        q, kv_t, x_rows, x_pages = norm_proj(
            x, norm_g, w, [("row", 0, qw), ("chan", qw, KV_ROW), ("row", qw, KV_ROW), ("page", qw, KV_ROW)],
            [F32, F32, BF16, BF16])
        o = banded_attention(q.reshape(b, seq, -1), x_rows.reshape(b, seq, KV_ROW),
                             x_pages.reshape(b, seq // TQ, KV_ROW, TQ), 0, SWA_KV_HEADS, SWA_GROUP, SWA_WINDOW, sinks)
        kv_out = _token_major(kv_t[:, :, seq - min(SWA_WINDOW, seq):])
    else:
        q, kv = norm_proj(x, norm_g, w, [("row", 0, qw), ("row", qw, KV_ROW)], [F32, F32])
        kv3 = kv.reshape(b, seq, KV_ROW)
        o = banded_decode(q.reshape(b, seq, -1), swa_state_t, kv3, SWA_KV_HEADS, SWA_GROUP, SWA_WINDOW, past, sinks)
        npast = swa_state_t.shape[2]
        kv_out = _token_major(jnp.concatenate([swa_state_t, jnp.swapaxes(kv3, 1, 2)], axis=2)[:, :, -npast:])
    x = proj_res([o.reshape(m, -1)], [w_out.astype(BF16)], x.reshape(m, d))
    return x.reshape(b, seq, d), kv_out


def kernel(x_prompt, x_sample, cache_nsa_cmp_kv, cache_nsa_slc_kv, state_nsa_win_kv, state_ssm, state_ssm_conv, state_swa_kv, page_table, norm_mix, norm_ffn, norm_final, w_in_even, w_out_even, cmp_pe_k, cmp_w1_k, cmp_w2_k, cmp_pe_v, cmp_w1_v, cmp_w2_v, ssm_conv_w, ssm_conv_b, ssm_dt_bias, ssm_a_log, ssm_d, ssm_norm, w_in_odd, w_out_odd, swa_sinks, w_gate_up, w_down):
    depth = norm_mix.shape[0]
    page = cache_nsa_cmp_kv.shape[2]
    past = page_table.shape[1] * page
    xp, xs = x_prompt, x_sample
    outs_p = [[] for _ in range(6)]
    outs_s = [[] for _ in range(6)]
    for layer in range(depth):
        if layer % 2 == 0:
            e = layer // 2
            cw = _compress_weights(cmp_pe_k[e], cmp_w1_k[e], cmp_w2_k[e], cmp_pe_v[e], cmp_w1_v[e], cmp_w2_v[e])
            ssm_p = (ssm_conv_w[e], ssm_conv_b[e], ssm_dt_bias[e], ssm_a_log[e], ssm_d[e], ssm_norm[e])
            p = (norm_mix[layer], w_in_even[e], w_out_even[e], cw, ssm_p)
            xp, o = _even_layer(xp, 0, p, None)
            for lst, v in zip(outs_p[:5], o):
                lst.append(v)
            cache = (_channel_major(cache_nsa_cmp_kv[e]), _channel_major(cache_nsa_slc_kv[e]), page_table,
                     _channel_major(state_nsa_win_kv[e]), state_ssm_conv[e], state_ssm[e])
            xs, o = _even_layer(xs, past, p, cache)
            for lst, v in zip(outs_s[:5], o):
                lst.append(v)
        else:
            o_idx = layer // 2
            p = (norm_mix[layer], w_in_odd[o_idx], w_out_odd[o_idx], swa_sinks[o_idx])
            xp, kw = _odd_layer(xp, 0, p, None)
            outs_p[5].append(kw)
            xs, kw = _odd_layer(xs, past, p, _channel_major(state_swa_kv[o_idx]))
            outs_s[5].append(kw)
        gf = norm_final if layer == depth - 1 else None
        wgu, wd = w_gate_up[layer].astype(BF16), w_down[layer].astype(BF16)
        xp = ffn(xp.reshape(-1, xp.shape[-1]), norm_ffn[layer], wgu, wd, gf).reshape(xp.shape)
        xs = ffn(xs.reshape(-1, xs.shape[-1]), norm_ffn[layer], wgu, wd, gf).reshape(xs.shape)
    return (xp, xs) + tuple(jnp.stack(v) for v in outs_p) + tuple(jnp.stack(v) for v in outs_s)
```

```python
import functools
import math

import numpy as np
import jax
import jax.numpy as jnp
from jax import lax
from jax.experimental import pallas as pl
from jax.experimental.pallas import tpu as pltpu

F32 = jnp.float32
BF16 = jnp.bfloat16

HEAD_DIM = 64
NSA_HEADS = 8
NSA_KV_HEADS = 2
NSA_GROUP = NSA_HEADS // NSA_KV_HEADS
CMP_BLOCK = 64
SEL_TOPK = 16
NSA_WINDOW = 512
SSM_HEADS = 8
SSM_HEAD_DIM = 64
SSM_D_INNER = SSM_HEADS * SSM_HEAD_DIM
SSM_GROUPS = 2
SSM_STATE = 128
SSM_CONV = 4
SSM_CONV_DIM = SSM_D_INNER + 2 * SSM_GROUPS * SSM_STATE
SWA_HEADS = 16
SWA_KV_HEADS = 2
SWA_GROUP = SWA_HEADS // SWA_KV_HEADS
SWA_WINDOW = 128
RMS_EPS = 1e-6
NSA_WIDTH = NSA_HEADS * HEAD_DIM
SWA_WIDTH = SWA_HEADS * HEAD_DIM
SCALE = HEAD_DIM ** -0.5

LANE = 128
SUBLANE = 8
KV_ROW = 2 * NSA_KV_HEADS * HEAD_DIM
TQ = 128
NEG = -1e30
MASK = -2e30
VMEM_LIMIT = 56 * 1024 * 1024
DT_LANE = 3 * NSA_HEADS
N_STREAMS = 1
NSA_CHUNK_TILES = 4


def _cparams(*sem):
    return pltpu.CompilerParams(dimension_semantics=sem, vmem_limit_bytes=VMEM_LIMIT)


def _dot(a, b):
    return jnp.dot(a, b, preferred_element_type=F32)


def _dot_nt(a, b):
    return lax.dot_general(a, b, (((1,), (1,)), ((), ())), preferred_element_type=F32)


def _rms(x, g):
    return x * lax.rsqrt(jnp.mean(x * x, axis=-1, keepdims=True) + RMS_EPS) * g


def _sigmoid(x):
    return 1.0 / (1.0 + jnp.exp(-x))


def _alibi(n_heads):
    return [float(2.0 ** (-8.0 * i / n_heads)) for i in range(1, n_heads + 1)]


def _col(vals, rows):
    return jnp.concatenate([jnp.full((rows, 1), v, F32) for v in vals], axis=0)


def _row(vals, cols):
    return jnp.concatenate([jnp.full((1, cols), v, F32) for v in vals], axis=1)


def _pair_lanes(c0, c1):
    lane = lax.broadcasted_iota(jnp.int32, (c0.shape[0], LANE), 1)
    return jnp.where(lane < HEAD_DIM, c0, c1)


def _norm_proj_kernel(x_ref, g_ref, w_ref, *o_refs, pieces):
    hb = _rms(x_ref[...], g_ref[...]).astype(BF16)
    cache = {}
    spans = sorted({(off, n) for _, off, n in pieces}, key=lambda s: -s[1])
    for o_ref, (kind, off, n) in zip(o_refs, pieces):
        p_off, p_n = next((o2, n2) for o2, n2 in spans if o2 <= off and off + n <= o2 + n2)
        if (p_off, p_n) not in cache:
            cache[(p_off, p_n)] = _dot(hb, w_ref[:, p_off:p_off + p_n])
        if (off, n) not in cache:
            cache[(off, n)] = cache[(p_off, p_n)][:, off - p_off:off - p_off + n]
        y = cache[(off, n)]
        if kind == "row":
            o_ref[...] = y.astype(o_ref.dtype)
        else:
            if ("t", off, n) not in cache:
                cache[("t", off, n)] = y.T
            y_t = cache[("t", off, n)]
            if kind == "chan":
                o_ref[0] = y_t.astype(o_ref.dtype)
            else:
                for j in range(o_ref.shape[0]):
                    o_ref[j] = y_t[:, j * TQ:(j + 1) * TQ].astype(o_ref.dtype)


def norm_proj(x3d, g, w_bf, pieces, dtypes, tm=256):
    b, seq, d = x3d.shape
    m = b * seq
    n_tot = w_bf.shape[1]
    tm = min(tm, m)
    per_b = seq // tm if seq >= tm else 1
    out_specs, out_shape = [], []
    for (kind, _, n), dt in zip(pieces, dtypes):
        if kind == "row":
            out_specs.append(pl.BlockSpec((tm, n), lambda i: (i, 0)))
            out_shape.append(jax.ShapeDtypeStruct((m, n), dt))
        elif kind == "chan":
            out_specs.append(pl.BlockSpec((1, n, tm), lambda i: (i // per_b, 0, i % per_b)))
            out_shape.append(jax.ShapeDtypeStruct((b, n, seq), dt))
        else:
            out_specs.append(pl.BlockSpec((tm // TQ, n, TQ), lambda i: (i, 0, 0)))
            out_shape.append(jax.ShapeDtypeStruct((m // TQ, n, TQ), dt))
    return pl.pallas_call(
        functools.partial(_norm_proj_kernel, pieces=tuple(pieces)),
        grid=(m // tm,),
        in_specs=[pl.BlockSpec((tm, d), lambda i: (i, 0)),
                  pl.BlockSpec((1, d), lambda i: (0, 0)),
                  pl.BlockSpec((d, n_tot), lambda i: (0, 0))],
        out_specs=out_specs,
        out_shape=out_shape,
        compiler_params=_cparams("parallel"),
        name="norm_proj",
    )(x3d.reshape(m, d), g.reshape(1, d), w_bf)


def _proj_res_kernel(*refs, n_in):
    a_refs, w_refs = refs[:n_in], refs[n_in:2 * n_in]
    res_ref, o_ref = refs[2 * n_in], refs[2 * n_in + 1]
    acc = res_ref[...]
    for a_ref, w_ref in zip(a_refs, w_refs):
        acc = acc + _dot(a_ref[...].astype(BF16), w_ref[...])
    o_ref[...] = acc


def proj_res(a_list, w_list, res, tm=512):
    m, d = res.shape
    tm = min(tm, m)
    n_in = len(a_list)
    in_specs = ([pl.BlockSpec((tm, a.shape[1]), lambda i: (i, 0)) for a in a_list]
                + [pl.BlockSpec(w.shape, lambda i: (0, 0)) for w in w_list]
                + [pl.BlockSpec((tm, d), lambda i: (i, 0))])
    return pl.pallas_call(
        functools.partial(_proj_res_kernel, n_in=n_in),
        grid=(m // tm,),
        in_specs=in_specs,
        out_specs=pl.BlockSpec((tm, d), lambda i: (i, 0)),
        out_shape=jax.ShapeDtypeStruct((m, d), F32),
        compiler_params=_cparams("parallel"),
        name="proj_res",
    )(*a_list, *w_list, res)


def _ffn_kernel(x_ref, g_ref, wg_ref, wu_ref, wd_ref, *rest, final):
    if final:
        gf_ref, o_ref, h_scr, acc_scr = rest
    else:
        o_ref, h_scr, acc_scr = rest
    j = pl.program_id(1)

    @pl.when(j == 0)
    def _():
        x = x_ref[...]
        h_scr[...] = _rms(x, g_ref[...]).astype(BF16)
        acc_scr[...] = x

    hb = h_scr[...]
    gate = _dot(hb, wg_ref[...])
    up = _dot(hb, wu_ref[...])
    act = gate * _sigmoid(gate) * up
    acc_scr[...] += _dot(act.astype(BF16), wd_ref[...])

    @pl.when(j == pl.num_programs(1) - 1)
    def _():
        y = acc_scr[...]
        if final:
            y = _rms(y, gf_ref[...])
        o_ref[...] = y


def ffn(x2d, g, w_gu_bf, w_down_bf, g_final=None, tm=512):
    m, d = x2d.shape
    f = w_down_bf.shape[0]
    tm = min(tm, m)
    nf = 2 if (f // 2) % LANE == 0 else 1
    tf = f // nf
    final = g_final is not None
    in_specs = [pl.BlockSpec((tm, d), lambda i, j: (i, 0)),
                pl.BlockSpec((1, d), lambda i, j: (0, 0)),
                pl.BlockSpec((d, tf), lambda i, j: (0, j)),
                pl.BlockSpec((d, tf), lambda i, j: (0, j + nf)),
                pl.BlockSpec((tf, d), lambda i, j: (j, 0))]
    args = [x2d, g.reshape(1, d), w_gu_bf, w_gu_bf, w_down_bf]
    if final:
        in_specs.append(pl.BlockSpec((1, d), lambda i, j: (0, 0)))
        args.append(g_final.reshape(1, d))
    return pl.pallas_call(
        functools.partial(_ffn_kernel, final=final),
        grid=(m // tm, nf),
        in_specs=in_specs,
        out_specs=pl.BlockSpec((tm, d), lambda i, j: (i, 0)),
        out_shape=jax.ShapeDtypeStruct((m, d), F32),
        scratch_shapes=[pltpu.VMEM((tm, d), BF16), pltpu.VMEM((tm, d), F32)],
        compiler_params=_cparams("parallel", "arbitrary"),
        name="ffn",
    )(*args)


def _gelu_tanh(x):
    c = math.sqrt(2.0 / math.pi)
    return x * (0.5 * (1.0 + jnp.tanh(c * (x + 0.044715 * (x * x * x)))))


def _transpose8(v):
    row = lax.broadcasted_iota(jnp.int32, (SUBLANE, LANE), 0)
    for s in (4, 2, 1):
        low = (row & s) == 0
        out = list(v)
        for i in range(SUBLANE):
            if i & s == 0:
                out[i] = jnp.where(low, v[i], pltpu.roll(v[i + s], s, 0))
                out[i + s] = jnp.where(low, pltpu.roll(v[i], SUBLANE - s, 0), v[i + s])
        v = out
    return v


def _compress_kernel(x_ref, pe_ref, w1k_ref, w1v_ref, w2_ref, o_ref, *, n_pages):
    hidden = []
    for kv, w1_ref in enumerate((w1k_ref, w1v_ref)):
        slabs = []
        for c in range(kv * NSA_KV_HEADS, (kv + 1) * NSA_KV_HEADS):
            per_d = [[] for _ in range(HEAD_DIM)]
            for p0 in range(0, n_pages, SUBLANE):
                for d0 in range(0, HEAD_DIM, SUBLANE):
                    tiles = [x_ref[pl.ds((p0 + p) * KV_ROW + c * HEAD_DIM + d0, SUBLANE), :] for p in range(SUBLANE)]
                    for r, u in enumerate(_transpose8(tiles)):
                        per_d[d0 + r].append(u)
            rows = [jnp.concatenate(per_d[d], axis=0) for d in range(HEAD_DIM)]
            slabs.append((jnp.concatenate(rows, axis=1) + pe_ref[c:c + 1, :]).astype(BF16))
        h = _gelu_tanh(_dot(jnp.concatenate(slabs, axis=0), w1_ref[...])).astype(BF16)
        hidden += [h[i * n_pages:(i + 1) * n_pages] for i in range(NSA_KV_HEADS)]
    o_ref[...] = _dot(jnp.concatenate(hidden, axis=1), w2_ref[...]).astype(o_ref.dtype)


def compress(pages2d, cw, pages_per_step=64):
    pe4, w1k, w1v, w2 = cw
    n_pages = pages2d.shape[0] // KV_ROW
    pps = math.gcd(pages_per_step, n_pages)
    blocks_per_page = TQ // CMP_BLOCK
    const = lambda a: pl.BlockSpec(a.shape, lambda i: (0, 0))
    return pl.pallas_call(
        functools.partial(_compress_kernel, n_pages=pps),
        grid=(n_pages // pps,),
        in_specs=[pl.BlockSpec((pps * KV_ROW, TQ), lambda i: (i, 0)), const(pe4), const(w1k), const(w1v), const(w2)],
        out_specs=pl.BlockSpec((pps, blocks_per_page * KV_ROW), lambda i: (i, 0)),
        out_shape=jax.ShapeDtypeStruct((n_pages, blocks_per_page * KV_ROW), BF16),
        compiler_params=_cparams("parallel"),
        name="compress",
    )(pages2d, pe4, w1k, w1v, w2)


def _compress_weights(pe_k, w1_k, w2_k, pe_v, w1_v, w2_v):
    nj = TQ // CMP_BLOCK
    nc = 2 * NSA_KV_HEADS
    l, d, h = w1_k.shape

    def first(w1):
        w = jnp.zeros((d, nj, l, nj, h), F32)
        for j in range(nj):
            w = w.at[:, j, :, j, :].set(jnp.transpose(w1, (1, 0, 2)))
        return w.reshape(d * nj * l, nj * h).astype(BF16)

    def pe_row(pe):
        return jnp.broadcast_to(pe.T[:, None, :], (d, nj, l)).reshape(1, d * nj * l)

    w2 = jnp.zeros((nc, nj, h, nj, nc, w2_k.shape[1]), F32)
    for c in range(nc):
        for j in range(nj):
            w2 = w2.at[c, j, :, j, c, :].set(w2_k if c < NSA_KV_HEADS else w2_v)
    pe4 = jnp.concatenate([pe_row(pe_k if c < NSA_KV_HEADS else pe_v) for c in range(nc)], axis=0)
    return pe4, first(w1_k), first(w1_v), w2.reshape(nc * nj * h, nj * nc * w2_k.shape[1]).astype(BF16)


def _q_rows(q_ref, heads, n_rep, bi=0):
    assert NSA_KV_HEADS * HEAD_DIM == LANE
    nq = q_ref.shape[1]
    lane_half = lax.broadcasted_iota(jnp.int32, (nq, LANE), 1) // HEAD_DIM
    rows = []
    for h in heads:
        g = h // n_rep
        tile = q_ref[bi, :, (h // 2) * LANE:(h // 2 + 1) * LANE]
        if h % 2 != g:
            tile = pltpu.roll(tile, HEAD_DIM, 1)
        rows.append(jnp.where(lane_half == g, tile, 0.0))
    q_keys = jnp.concatenate(rows, axis=0)
    return jnp.concatenate([q_keys, jnp.zeros_like(q_keys)], axis=1).astype(BF16)


def _heads_to_rows(o_t, n_rep, pair):
    r0 = 2 * pair
    two = jnp.concatenate([o_t[:, r0 * TQ:(r0 + 1) * TQ], o_t[:, (r0 + 1) * TQ:(r0 + 2) * TQ]], axis=0)
    return two.T


def _banded_kernel(*refs, n_groups, n_rep, window, slopes, has_sink, kv_block):
    if has_sink:
        sink_ref, q_ref, x_ref, xt_ref, o_ref = refs
    else:
        q_ref, x_ref, xt_ref, o_ref = refs
    qt = pl.program_id(1)
    nrow = n_rep * TQ
    n_tiles = window // TQ + 1
    ki = lax.broadcasted_iota(jnp.int32, (TQ, nrow), 0)
    qi = lax.broadcasted_iota(jnp.int32, (TQ, nrow), 1) % TQ
    diff = qi - ki
    diff_f = diff.astype(F32)
    tiles = []
    for j in range(n_tiles):
        kt = qt - (n_tiles - 1 - j)
        ktc = jnp.maximum(kt, 0)
        x = x_ref[0, pl.ds(pl.multiple_of(ktc * TQ, TQ), TQ), :]
        tiles.append((x, ktc, jnp.where(kt >= 0, 0.0, MASK)))
    for g in range(n_groups):
        heads = [g * n_rep + r for r in range(n_rep)]
        qrows = _q_rows(q_ref, heads, n_rep)
        slope_row = _row([slopes[h] for h in heads], TQ)
        bias0 = slope_row * diff_f
        v0 = kv_block * KV_ROW + (NSA_KV_HEADS + g) * HEAD_DIM
        s_list = []
        for j, (x, _, before_start) in enumerate(tiles):
            s = _dot_nt(x, qrows) - bias0 + (before_start - slope_row * float((n_tiles - 1 - j) * TQ))
            if j == 0:
                s = jnp.where(diff <= 0, s, MASK)
            if j == n_tiles - 1:
                s = jnp.where(diff >= 0, s, MASK)
            s_list.append(s)
        m = s_list[0].max(axis=0, keepdims=True)
        for s in s_list[1:]:
            m = jnp.maximum(m, s.max(axis=0, keepdims=True))
        if has_sink:
            sink_row = jnp.concatenate([jnp.full((1, TQ), sink_ref[h], F32) for h in heads], axis=1)
            m = jnp.maximum(m, sink_row)
        p_list = [jnp.exp(s - m) for s in s_list]
        den = p_list[0].sum(axis=0, keepdims=True)
        for p in p_list[1:]:
            den = den + p.sum(axis=0, keepdims=True)
        if has_sink:
            den = den + jnp.exp(sink_row - m)
        v_all = jnp.concatenate([xt_ref[0, ktc, v0:v0 + HEAD_DIM, :] for _, ktc, _ in tiles], axis=1)
        p_all = jnp.concatenate([p.astype(BF16) for p in p_list], axis=0)
        o_t = _dot(v_all, p_all) / jnp.maximum(den, 1e-30)
        for pair in range(n_rep // 2):
            h0 = heads[2 * pair]
            o_ref[0, :, h0 * HEAD_DIM:(h0 + 2) * HEAD_DIM] = _heads_to_rows(o_t, n_rep, pair)


def banded_attention(q_pad, x_rows, x_pages, kv_block, n_groups, n_rep, window, sinks=None):
    b, seq = q_pad.shape[:2]
    n_heads = n_groups * n_rep
    has_sink = sinks is not None
    in_specs = [pl.BlockSpec((1, TQ, n_heads * HEAD_DIM), lambda i, j: (i, j, 0)),
                pl.BlockSpec((1, seq, KV_ROW), lambda i, j: (i, 0, kv_block)),
                pl.BlockSpec((1,) + x_pages.shape[1:], lambda i, j: (i, 0, 0, 0))]
    args = [q_pad, x_rows, x_pages]
    if has_sink:
        in_specs.insert(0, pl.BlockSpec(memory_space=pltpu.SMEM))
        args.insert(0, sinks)
    assert window % TQ == 0 and window >= TQ
    return pl.pallas_call(
        functools.partial(_banded_kernel, n_groups=n_groups, n_rep=n_rep, window=window,
                          slopes=_alibi(n_heads), has_sink=has_sink, kv_block=kv_block),
        grid=(b, seq // TQ),
        in_specs=in_specs,
        out_specs=pl.BlockSpec((1, TQ, n_heads * HEAD_DIM), lambda i, j: (i, j, 0)),
        out_shape=jax.ShapeDtypeStruct((b, seq, n_heads * HEAD_DIM), F32),
        compiler_params=_cparams("parallel", "parallel"),
        name="banded_attention",
    )(*args)


def _banded_decode_kernel(*refs, n_groups, n_rep, window, slopes, has_sink, pos0):
    if has_sink:
        sink_ref, q_ref, st_ref, new_ref, o_ref = refs
    else:
        q_ref, st_ref, new_ref, o_ref = refs
    nq = q_ref.shape[1]
    npast = st_ref.shape[2]
    nrow = n_rep * nq
    qi_s = jnp.concatenate([lax.broadcasted_iota(jnp.int32, (nq, npast), 0)] * n_rep, axis=0)
    kj_s = lax.broadcasted_iota(jnp.int32, (nrow, npast), 1)
    dist_s = npast + qi_s - kj_s
    valid_s = (dist_s >= 0) & (dist_s <= window) & (pos0 - npast + kj_s >= 0)
    qi_n = jnp.concatenate([lax.broadcasted_iota(jnp.int32, (nq, TQ), 0)] * n_rep, axis=0)
    kj_n = lax.broadcasted_iota(jnp.int32, (nrow, TQ), 1)
    dist_n = qi_n - kj_n
    valid_n = (dist_n >= 0) & (dist_n <= window) & (kj_n < nq)
    for bi, g in [(bi, g) for bi in range(q_ref.shape[0]) for g in range(n_groups)]:
        xs_t = st_ref[bi].astype(BF16)
        xn = jnp.concatenate([new_ref[bi], jnp.zeros((TQ - nq, KV_ROW), F32)], axis=0).astype(BF16)
        heads = [g * n_rep + r for r in range(n_rep)]
        qrows = _q_rows(q_ref, heads, n_rep, bi)
        slope_col = _col([slopes[h] for h in heads], nq)
        s_s = jnp.where(valid_s, _dot(qrows, xs_t) - slope_col * dist_s.astype(F32), NEG)
        s_n = jnp.where(valid_n, _dot_nt(qrows, xn) - slope_col * dist_n.astype(F32), NEG)
        m = jnp.maximum(jnp.max(s_s, axis=-1, keepdims=True), jnp.max(s_n, axis=-1, keepdims=True))
        if has_sink:
            sink_col = jnp.concatenate([jnp.full((nq, 1), sink_ref[h], F32) for h in heads], axis=0)
            m = jnp.maximum(m, sink_col)
        p_s = jnp.where(valid_s, jnp.exp(s_s - m), 0.0)
        p_n = jnp.where(valid_n, jnp.exp(s_n - m), 0.0)
        den = jnp.sum(p_s, axis=-1, keepdims=True) + jnp.sum(p_n, axis=-1, keepdims=True)
        if has_sink:
            den = den + jnp.exp(sink_col - m)
        o = (_dot_nt(p_s.astype(BF16), xs_t) + _dot(p_n.astype(BF16), xn)) / jnp.maximum(den, 1e-30)
        v0 = (NSA_KV_HEADS + g) * HEAD_DIM
        for r, h in enumerate(heads):
            o_ref[bi, :, h * HEAD_DIM:(h + 1) * HEAD_DIM] = o[r * nq:(r + 1) * nq, v0:v0 + HEAD_DIM]


def banded_decode(q_pad, state_t, new_kv, n_groups, n_rep, window, pos0, sinks=None, rows_per_step=4):
    b, nq = q_pad.shape[:2]
    npast = state_t.shape[2]
    n_heads = n_groups * n_rep
    has_sink = sinks is not None
    bb = math.gcd(rows_per_step, b)
    in_specs = [pl.BlockSpec((bb, nq, n_heads * HEAD_DIM), lambda i: (i, 0, 0)),
                pl.BlockSpec((bb, KV_ROW, npast), lambda i: (i, 0, 0)),
                pl.BlockSpec((bb, nq, KV_ROW), lambda i: (i, 0, 0))]
    args = [q_pad, state_t, new_kv]
    if has_sink:
        in_specs.insert(0, pl.BlockSpec(memory_space=pltpu.SMEM))
        args.insert(0, sinks)
    return pl.pallas_call(
        functools.partial(_banded_decode_kernel, n_groups=n_groups, n_rep=n_rep, window=window,
                          slopes=_alibi(n_heads), has_sink=has_sink, pos0=pos0),
        grid=(b // bb,),
        in_specs=in_specs,
        out_specs=pl.BlockSpec((bb, nq, n_heads * HEAD_DIM), lambda i: (i, 0, 0)),
        out_shape=jax.ShapeDtypeStruct((b, nq, n_heads * HEAD_DIM), F32),
        compiler_params=_cparams("parallel"),
        name="banded_decode",
    )(*args)


def _select_blocks_t(imp, cur, n_sel):
    nb = imp.shape[0]
    nrow = lax.broadcasted_iota(jnp.int32, imp.shape, 0)
    count = jnp.zeros(imp.shape, F32)
    for i in range(nb):
        ci = imp[i:i + 1, :]
        ahead = jnp.where(ci > imp, 1.0, jnp.where(ci == imp, jnp.where(nrow > i, 1.0, 0.0), 0.0))
        count = count + jnp.where(cur > i, ahead, 0.0)
    return jnp.where(nrow < cur, jnp.where(count < n_sel - 1, 1.0, 0.0), jnp.where(nrow == cur, 1.0, 0.0))


def _nsa_prompt_kernel(q_ref, gd_ref, kc_ref, kct_ref, x_ref, xt_ref, ow_ref, o_ref, m_scr, acc_scr,
                       *, n_sel, kv_block):
    qt = pl.program_id(1)
    n_rep = NSA_GROUP
    nrow = n_rep * TQ
    slopes = _alibi(NSA_HEADS)
    nb = kc_ref.shape[1]
    kcb = kc_ref[0]
    gates = _sigmoid(gd_ref[0])
    n_ct = NSA_CHUNK_TILES
    tk = n_ct * TQ
    n_seq_tiles = x_ref.shape[1] // TQ
    ki = lax.broadcasted_iota(jnp.int32, (tk, nrow), 0)
    qi = lax.broadcasted_iota(jnp.int32, (tk, nrow), 1) % TQ
    diff = qi - ki
    tpos = qt * TQ + lax.broadcasted_iota(jnp.int32, (nb, nrow), 1) % TQ
    blk = lax.broadcasted_iota(jnp.int32, (nb, nrow), 0)
    dist_c = tpos - ((blk + 1) * CMP_BLOCK - 1)
    valid_c = dist_c >= 0
    cur = (qt * TQ + lax.broadcasted_iota(jnp.int32, (nb, TQ), 1)) // CMP_BLOCK
    assert nb + 1 < LANE and all(math.frexp(sl)[0] == 0.5 for sl in slopes)
    lane_x = lax.broadcasted_iota(jnp.int32, (tk, LANE), 1)
    key_in_chunk = lax.broadcasted_iota(jnp.int32, (tk, LANE), 0)
    key_blk = key_in_chunk // CMP_BLOCK
    key_lo = (key_in_chunk % TQ).astype(F32)
    key_hi = (key_in_chunk // TQ).astype(F32)
    lane_q = lax.broadcasted_iota(jnp.int32, (TQ, LANE), 1)
    ones_rows = jnp.ones((2 * SUBLANE, tk), BF16)
    groups = []
    for g in range(NSA_KV_HEADS):
        heads = [g * n_rep + r for r in range(n_rep)]
        qrows = _q_rows(q_ref, heads, n_rep)
        slope_row = _row([slopes[h] for h in heads], TQ)
        vrow = (NSA_KV_HEADS + g) * HEAD_DIM
        s_c = jnp.where(valid_c, _dot_nt(kcb, qrows) - slope_row * dist_c.astype(F32), NEG)
        e_c = jnp.where(valid_c, jnp.exp(s_c - jnp.max(s_c, axis=0, keepdims=True)), 0.0)
        p_c = e_c / jnp.maximum(jnp.sum(e_c, axis=0, keepdims=True), 1e-30)
        ocmp_t = _dot(kct_ref[0, vrow:vrow + HEAD_DIM, :], p_c.astype(BF16))
        imp = p_c[:, 0:TQ]
        for r in range(1, n_rep):
            imp = imp + p_c[:, r * TQ:(r + 1) * TQ]
        taken = _select_blocks_t(imp, cur, n_sel)
        taken_q = jnp.concatenate([taken, jnp.zeros((LANE - nb, TQ), F32)], axis=0).T
        block_mask = jnp.where(taken_q > 0.5, 0.0, MASK)
        q_extra = jnp.concatenate([jnp.where(lane_q == nb, slopes[h], jnp.where(lane_q == nb + 1, slopes[h] * TQ,
                                                                                 block_mask)) for h in heads], axis=0)
        q_aug = jnp.concatenate([qrows, q_extra.astype(BF16)], axis=1)
        m_scr[g] = jnp.full((1, nrow), NEG, F32)
        acc_scr[g] = jnp.zeros(acc_scr.shape[1:], F32)
        groups.append((heads, q_aug, slope_row, kv_block * KV_ROW + vrow, ocmp_t))

    def chunk(kt0, causal):
        kts = [jnp.minimum(kt0 + i, n_seq_tiles - 1) for i in range(n_ct)]
        x_extra = jnp.where(lane_x == nb, key_lo, jnp.where(lane_x == nb + 1, key_hi,
                            jnp.where(lane_x == key_blk + kt0 * (TQ // CMP_BLOCK), 1.0, 0.0))).astype(BF16)
        x = jnp.concatenate([x_ref[0, pl.ds(pl.multiple_of(kt * TQ, TQ), TQ), :] for kt in kts], axis=0)
        x_aug = jnp.concatenate([x, x_extra], axis=1)
        off = (qt - kt0) * TQ
        for g, (_, q_aug, slope_row, v0, _) in enumerate(groups):
            s = _dot_nt(x_aug, q_aug)
            if causal:
                s = jnp.where(diff + off >= 0, s, MASK)
            shift = slope_row * off.astype(F32)
            m_old = m_scr[g]
            m_new = jnp.maximum(m_old, jnp.max(s, axis=0, keepdims=True) - shift)
            p = jnp.exp(s - (m_new + shift)).astype(BF16)
            v_t = jnp.concatenate([xt_ref[0, kt, v0:v0 + HEAD_DIM, :] for kt in kts], axis=1)
            acc_scr[g] = jnp.exp(m_old - m_new) * acc_scr[g] + _dot(jnp.concatenate([v_t, ones_rows], axis=0), p)
            m_scr[g] = m_new

    n_full = qt // n_ct

    def body(j, carry):
        chunk(n_ct * j, False)
        return carry

    lax.fori_loop(0, n_full, body, 0)
    chunk(n_ct * n_full, True)
    for g, (heads, _, _, _, ocmp_t) in enumerate(groups):
        oslc_t = acc_scr[g, 0:HEAD_DIM, :] / jnp.maximum(acc_scr[g, HEAD_DIM:HEAD_DIM + 1, :], 1e-30)
        for pair in range(n_rep // 2):
            h0 = heads[2 * pair]
            hh = slice(h0 * HEAD_DIM, (h0 + 2) * HEAD_DIM)
            gate = lambda j: _pair_lanes(gates[:, 3 * h0 + j:3 * h0 + j + 1], gates[:, 3 * h0 + 3 + j:3 * h0 + 4 + j])
            o_ref[0, :, hh] = (gate(0) * _heads_to_rows(ocmp_t, n_rep, pair)
                               + gate(1) * _heads_to_rows(oslc_t, n_rep, pair)
                               + gate(2) * ow_ref[0, :, hh])


def nsa_prompt(q_pad, gd, kc, x_rows, x_pages, kv_block, o_win):
    b, seq = q_pad.shape[:2]
    nb = kc.shape[1]
    nrow = NSA_GROUP * TQ
    kct = jnp.swapaxes(kc, 1, 2)
    return pl.pallas_call(
        functools.partial(_nsa_prompt_kernel, n_sel=min(SEL_TOPK, nb), kv_block=kv_block),
        grid=(b, seq // TQ),
        in_specs=[pl.BlockSpec((1, TQ, NSA_WIDTH), lambda i, j: (i, j, 0)),
                  pl.BlockSpec((1, TQ, LANE), lambda i, j: (i, j, 0)),
                  pl.BlockSpec((1, nb, KV_ROW), lambda i, j: (i, 0, 0)),
                  pl.BlockSpec((1, KV_ROW, nb), lambda i, j: (i, 0, 0)),
                  pl.BlockSpec((1, seq, KV_ROW), lambda i, j: (i, 0, kv_block)),
                  pl.BlockSpec((1,) + x_pages.shape[1:], lambda i, j: (i, 0, 0, 0)),
                  pl.BlockSpec((1, TQ, NSA_WIDTH), lambda i, j: (i, j, 0))],
        out_specs=pl.BlockSpec((1, TQ, NSA_WIDTH), lambda i, j: (i, j, 0)),
        out_shape=jax.ShapeDtypeStruct((b, seq, NSA_WIDTH), F32),
        scratch_shapes=[pltpu.VMEM((NSA_KV_HEADS, 1, nrow), F32),
                        pltpu.VMEM((NSA_KV_HEADS, HEAD_DIM + 2 * SUBLANE, nrow), F32)],
        compiler_params=_cparams("parallel", "parallel"),
        name="nsa_prompt",
    )(q_pad, gd, kc, kct, x_rows, x_pages, o_win)


def _select_blocks(imp, n_sel):
    nb = imp.shape[1]
    ncol = lax.broadcasted_iota(jnp.int32, imp.shape, 1)
    count = jnp.zeros(imp.shape, F32)
    for i in range(nb):
        ci = imp[:, i:i + 1]
        count = count + jnp.where(ci > imp, 1.0, jnp.where(ci == imp, jnp.where(ncol > i, 1.0, 0.0), 0.0))
    return jnp.where(count < n_sel - 1, 1.0, 0.0)


def _nsa_decode_kernel(pt_ref, q_ref, gd_ref, kc_ref, new_ref, ow_ref, ex_ref, *rest, pages_per_step, pos0, n_sel):
    page_refs = rest[:pages_per_step]
    o_ref, m_scr, l_scr, acc_scr, mask_scr, bias_scr, ocmp_scr, q_scr = rest[pages_per_step:]
    p = pl.program_id(1)
    n_rep = NSA_GROUP
    nq = q_ref.shape[1]
    ngrp = n_rep * nq
    nrow = NSA_KV_HEADS * ngrp
    nb = kc_ref.shape[1]
    page = page_refs[0].shape[1]
    nkey = pages_per_step * page
    slopes = _alibi(NSA_HEADS)
    slope_col = _col(slopes, nq)

    def per_query(a):
        return jnp.concatenate([a[g] for g in range(NSA_KV_HEADS) for _ in range(n_rep)], axis=0)

    @pl.when(p == 0)
    def _():
        q_scr[...] = jnp.concatenate([_q_rows(q_ref, [g * n_rep + r for r in range(n_rep)], n_rep)
                                      for g in range(NSA_KV_HEADS)], axis=0)

    qrows = q_scr[...]

    @pl.when(p == 0)
    def _():
        kcb = kc_ref[0]
        tpos = pos0 + jnp.concatenate([lax.broadcasted_iota(jnp.int32, (nq, nb), 0)] * (nrow // nq), axis=0)
        ncol = lax.broadcasted_iota(jnp.int32, (nrow, nb), 1)
        dist = tpos - ((ncol + 1) * CMP_BLOCK - 1)
        valid = dist >= 0
        s = jnp.where(valid, _dot_nt(qrows, kcb) - slope_col * dist.astype(F32), NEG)
        e = jnp.where(valid, jnp.exp(s - jnp.max(s, axis=-1, keepdims=True)), 0.0)
        pc = e / jnp.maximum(jnp.sum(e, axis=-1, keepdims=True), 1e-30)
        ocmp_scr[...] = _dot(pc.astype(BF16), kcb)
        not_taken = []
        for g in range(NSA_KV_HEADS):
            imp = pc[g * ngrp:g * ngrp + nq]
            for r in range(1, n_rep):
                imp = imp + pc[g * ngrp + r * nq:g * ngrp + (r + 1) * nq]
            not_taken.append(jnp.where(_select_blocks(imp, n_sel) > 0.5, 0.0, MASK))
        not_taken = jnp.concatenate(not_taken, axis=0).astype(BF16)
        for st in range(mask_scr.shape[0]):
            mask_scr[st] = _dot(not_taken, ex_ref[:, st * nkey:(st + 1) * nkey])
        qi = jnp.concatenate([lax.broadcasted_iota(jnp.int32, (nq, nkey), 0)] * (nrow // nq), axis=0)
        kj = lax.broadcasted_iota(jnp.int32, (nrow, nkey), 1)
        bias_scr[...] = slope_col * (pos0 + qi - kj).astype(F32)
        m_scr[...] = jnp.full_like(m_scr, NEG)
        l_scr[...] = jnp.zeros_like(l_scr)
        acc_scr[...] = jnp.zeros_like(acc_scr)

    mask = mask_scr[p]
    shift = slope_col * (p * nkey).astype(F32)
    per = pages_per_step // N_STREAMS
    for st in range(N_STREAMS):
        ks = slice(st * per * page, (st + 1) * per * page)
        x_t = jnp.concatenate([r[...] for r in page_refs[st * per:(st + 1) * per]], axis=1).astype(BF16)
        s = (_dot(qrows, x_t) - bias_scr[:, ks]
             + per_query([mask[g * nq:(g + 1) * nq, ks] for g in range(NSA_KV_HEADS)]))
        m_old = m_scr[st]
        m_new = jnp.maximum(m_old, jnp.max(s, axis=-1, keepdims=True) + shift)
        alpha = jnp.exp(m_old - m_new)
        pr = jnp.exp(s - (m_new - shift))
        l_scr[st] = alpha * l_scr[st] + jnp.sum(pr, axis=-1, keepdims=True)
        acc_scr[st] = alpha * acc_scr[st] + _dot_nt(pr.astype(BF16), x_t)
        m_scr[st] = m_new

    @pl.when(p == pl.num_programs(1) - 1)
    def _():
        xn = jnp.concatenate([new_ref[0], jnp.zeros((TQ - nq, KV_ROW), F32)], axis=0).astype(BF16)
        qi_n = jnp.concatenate([lax.broadcasted_iota(jnp.int32, (nq, TQ), 0)] * (nrow // nq), axis=0)
        kj_n = lax.broadcasted_iota(jnp.int32, (nrow, TQ), 1)
        dist_n = qi_n - kj_n
        valid_n = (dist_n >= 0) & (kj_n < nq)
        s_n = jnp.where(valid_n, _dot_nt(qrows, xn) - slope_col * dist_n.astype(F32), NEG)
        m_new = jnp.max(s_n, axis=-1, keepdims=True)
        for st in range(N_STREAMS):
            m_new = jnp.maximum(m_new, m_scr[st])
        p_n = jnp.where(valid_n, jnp.exp(s_n - m_new), 0.0)
        den = jnp.sum(p_n, axis=-1, keepdims=True)
        num = _dot(p_n.astype(BF16), xn)
        for st in range(N_STREAMS):
            alpha = jnp.exp(m_scr[st] - m_new)
            den = den + alpha * l_scr[st]
            num = num + alpha * acc_scr[st]
        o_slc = num / jnp.maximum(den, 1e-30)
        o_cmp = ocmp_scr[...]
        gates = _sigmoid(gd_ref[0])
        for h in range(NSA_HEADS):
            g = h // n_rep
            v0 = (NSA_KV_HEADS + g) * HEAD_DIM
            rr = slice(h * nq, (h + 1) * nq)
            hh = slice(h * HEAD_DIM, (h + 1) * HEAD_DIM)
            o_ref[0, :, hh] = (gates[:, 3 * h:3 * h + 1] * o_cmp[rr, v0:v0 + HEAD_DIM]
                               + gates[:, 3 * h + 1:3 * h + 2] * o_slc[rr, v0:v0 + HEAD_DIM]
                               + gates[:, 3 * h + 2:3 * h + 3] * ow_ref[0, :, hh])


def nsa_decode(page_table, q_pad, gd, kc, new_kv, o_win, pool_t, pos0, pages_per_step=16):
    b, nq = q_pad.shape[:2]
    nb = kc.shape[1]
    n_pages = page_table.shape[1]
    page = pool_t.shape[2]
    pps = math.gcd(pages_per_step, n_pages)
    assert pps % N_STREAMS == 0
    n_steps = n_pages // pps
    nkey = pps * page
    nrow = NSA_HEADS * nq
    expand = (jnp.arange(n_pages * page)[None, :] // CMP_BLOCK == jnp.arange(nb)[:, None]).astype(BF16)

    def page_spec(j):
        return pl.BlockSpec((None, KV_ROW, page), lambda i, p, pt: (pt[i, p * pps + j], 0, 0))

    grid_spec = pltpu.PrefetchScalarGridSpec(
        num_scalar_prefetch=1,
        grid=(b, n_steps),
        in_specs=[pl.BlockSpec((1, nq, NSA_WIDTH), lambda i, p, pt: (i, 0, 0)),
                  pl.BlockSpec((1, nq, LANE), lambda i, p, pt: (i, 0, 0)),
                  pl.BlockSpec((1, nb, KV_ROW), lambda i, p, pt: (i, 0, 0)),
                  pl.BlockSpec((1, nq, KV_ROW), lambda i, p, pt: (i, 0, 0)),
                  pl.BlockSpec((1, nq, NSA_WIDTH), lambda i, p, pt: (i, 0, 0)),
                  pl.BlockSpec(expand.shape, lambda i, p, pt: (0, 0))]
                 + [page_spec(j) for j in range(pps)],
        out_specs=pl.BlockSpec((1, nq, NSA_WIDTH), lambda i, p, pt: (i, 0, 0)),
        scratch_shapes=[pltpu.VMEM((N_STREAMS, nrow, 1), F32), pltpu.VMEM((N_STREAMS, nrow, 1), F32),
                        pltpu.VMEM((N_STREAMS, nrow, KV_ROW), F32),
                        pltpu.VMEM((n_steps, NSA_KV_HEADS * nq, nkey), F32),
                        pltpu.VMEM((nrow, nkey), F32),
                        pltpu.VMEM((nrow, KV_ROW), F32),
                        pltpu.VMEM((nrow, KV_ROW), BF16)])
    return pl.pallas_call(
        functools.partial(_nsa_decode_kernel, pages_per_step=pps, pos0=pos0, n_sel=min(SEL_TOPK, nb + 1)),
        grid_spec=grid_spec,
        out_shape=jax.ShapeDtypeStruct((b, nq, NSA_WIDTH), F32),
        compiler_params=_cparams("parallel", "arbitrary"),
        name="nsa_decode",
    )(page_table, q_pad, gd, kc, new_kv, o_win, expand, *([pool_t] * pps))


def _cumsum_rows(v):
    n = v.shape[0]
    ri = lax.broadcasted_iota(jnp.int32, v.shape, 0)
    sh = 1
    while sh < n:
        v = v + jnp.where(ri >= sh, pltpu.roll(v, sh, 0), 0.0)
        sh *= 2
    return v


def _ssd_kernel(z_ref, xbc_ref, gd_ref, tail_ref, s0_ref, cw_ref, cb_ref, dtb_ref, alog_ref, dsk_ref, ng_ref,
                y_ref, sfin_ref, xp_scr, st_scr, y_scr, *, n_valid):
    c = pl.program_id(1)
    q = xbc_ref.shape[1]

    @pl.when(c == 0)
    def _():
        xp_scr[0:SUBLANE, :] = tail_ref[0]
        st_scr[...] = s0_ref[0]

    xp_scr[SUBLANE:SUBLANE + q, :] = xbc_ref[0]
    acc = cb_ref[...]
    for k in range(SSM_CONV):
        lo = SUBLANE - (SSM_CONV - 1) + k
        acc = acc + xp_scr[lo:lo + q, :] * cw_ref[k:k + 1, :]
    nxt = xp_scr[q:q + SUBLANE, :]
    xp_scr[0:SUBLANE, :] = nxt
    act = acc * _sigmoid(acc)

    raw = gd_ref[0] + dtb_ref[...]
    dt = jnp.maximum(raw, 0.0) + jnp.log1p(jnp.exp(-jnp.abs(raw)))
    if n_valid < q:
        dt = jnp.where(lax.broadcasted_iota(jnp.int32, dt.shape, 0) < n_valid, dt, 0.0)
    acum = _cumsum_rows(dt * (-jnp.exp(alog_ref[...])))
    acum_t = acum.T
    ri = lax.broadcasted_iota(jnp.int32, (q, q), 0)
    ci = lax.broadcasted_iota(jnp.int32, (q, q), 1)
    causal = ri >= ci
    half = lax.broadcasted_iota(jnp.int32, (2 * SSM_HEAD_DIM, 1), 0) < SSM_HEAD_DIM
    hpg = SSM_HEADS // SSM_GROUPS
    for pair in range(SSM_HEADS // 2):
        grp = (2 * pair) // hpg
        lanes = slice(pair * LANE, (pair + 1) * LANE)
        bm = act[:, SSM_D_INNER + grp * SSM_STATE:SSM_D_INNER + (grp + 1) * SSM_STATE].astype(BF16)
        cm_lo = SSM_D_INNER + SSM_GROUPS * SSM_STATE + grp * SSM_STATE
        cm = act[:, cm_lo:cm_lo + SSM_STATE].astype(BF16)
        cb = _dot_nt(cm, bm)
        xs = act[:, lanes]
        cols = []
        for h in (2 * pair, 2 * pair + 1):
            ln = DT_LANE + h
            cols.append((acum[:, ln:ln + 1], acum_t[ln:ln + 1, :], acum[q - 1:q, ln:ln + 1], dt[:, ln:ln + 1]))
        xdt = xs * _pair_lanes(cols[0][3], cols[1][3])
        xdt_bf = xdt.astype(BF16)
        y_parts = []
        for a_col, a_row, _, _ in cols:
            seg = a_col - a_row
            lmat = jnp.where(causal, jnp.exp(jnp.where(causal, seg, 0.0)), 0.0)
            y_parts.append(_dot((cb * lmat).astype(BF16), xdt_bf))
        lane = lax.broadcasted_iota(jnp.int32, (q, LANE), 1)
        y_diag = jnp.where(lane < SSM_HEAD_DIM, y_parts[0], y_parts[1])
        st = st_scr[pair * LANE:(pair + 1) * LANE, :]
        y_off = _dot_nt(cm, st.astype(BF16)) * _pair_lanes(jnp.exp(cols[0][0]), jnp.exp(cols[1][0]))
        dec_end = _pair_lanes(jnp.exp(cols[0][2] - cols[0][0]), jnp.exp(cols[1][2] - cols[1][0]))
        cs = _dot((xdt * dec_end).T.astype(BF16), bm)
        st_scr[pair * LANE:(pair + 1) * LANE, :] = st * jnp.where(half, jnp.exp(cols[0][2]), jnp.exp(cols[1][2])) + cs
        dsk = _pair_lanes(jnp.zeros((q, 1), F32) + dsk_ref[:, 2 * pair:2 * pair + 1],
                          jnp.zeros((q, 1), F32) + dsk_ref[:, 2 * pair + 1:2 * pair + 2])
        y_scr[:, lanes] = y_diag + y_off + dsk * xs

    z = z_ref[0]
    y_ref[0] = _rms(y_scr[...] * (z * _sigmoid(z)), ng_ref[...])

    @pl.when(c == pl.num_programs(1) - 1)
    def _():
        sfin_ref[0] = st_scr[...]


def ssd_mixer(z, xbc, gd, tail8, s0, conv_w, conv_b, dt_bias, a_log, d_skip, norm_g, n_valid):
    b, seq = z.shape[:2]
    nst = SSM_HEADS * SSM_HEAD_DIM

    def lane_piece(v):
        return jnp.zeros((1, LANE), F32).at[0, DT_LANE:DT_LANE + SSM_HEADS].set(v)

    alog_piece = jnp.full((1, LANE), -100.0, F32).at[0, DT_LANE:DT_LANE + SSM_HEADS].set(a_log)
    full2 = lambda shape: pl.BlockSpec(shape, lambda i, c: (0, 0))
    return pl.pallas_call(
        functools.partial(_ssd_kernel, n_valid=n_valid),
        grid=(b, seq // TQ),
        in_specs=[pl.BlockSpec((1, TQ, SSM_D_INNER), lambda i, c: (i, c, 0)),
                  pl.BlockSpec((1, TQ, SSM_CONV_DIM), lambda i, c: (i, c, 0)),
                  pl.BlockSpec((1, TQ, LANE), lambda i, c: (i, c, 0)),
                  pl.BlockSpec((1, SUBLANE, SSM_CONV_DIM), lambda i, c: (i, 0, 0)),
                  pl.BlockSpec((1, nst, SSM_STATE), lambda i, c: (i, 0, 0)),
                  full2((SSM_CONV, SSM_CONV_DIM)), full2((1, SSM_CONV_DIM)), full2((1, LANE)),
                  full2((1, LANE)), full2((1, SSM_HEADS)), full2((1, SSM_D_INNER))],
        out_specs=[pl.BlockSpec((1, TQ, SSM_D_INNER), lambda i, c: (i, c, 0)),
                   pl.BlockSpec((1, nst, SSM_STATE), lambda i, c: (i, 0, 0))],
        out_shape=[jax.ShapeDtypeStruct((b, seq, SSM_D_INNER), F32),
                   jax.ShapeDtypeStruct((b, nst, SSM_STATE), F32)],
        scratch_shapes=[pltpu.VMEM((SUBLANE + TQ, SSM_CONV_DIM), F32),
                        pltpu.VMEM((nst, SSM_STATE), F32),
                        pltpu.VMEM((TQ, SSM_D_INNER), F32)],
        compiler_params=_cparams("parallel", "arbitrary"),
        name="ssd_mixer",
    )(z, xbc, gd, tail8, s0, conv_w, conv_b.reshape(1, -1), lane_piece(dt_bias), alog_piece,
      d_skip.reshape(1, -1), norm_g.reshape(1, -1))


def _channel_major(a6):
    b, rows = a6.shape[:2]
    return jnp.transpose(a6, (0, 2, 3, 4, 1)).reshape(b, KV_ROW, rows)


def _token_major(a_t):
    b, _, rows = a_t.shape
    return jnp.transpose(a_t.reshape(b, 2, NSA_KV_HEADS, HEAD_DIM, rows), (0, 4, 1, 2, 3))


def _even_weights(w_in):
    d = w_in.shape[0]
    o = np.cumsum([0, NSA_WIDTH, KV_ROW, KV_ROW, KV_ROW, 3 * NSA_HEADS, SSM_D_INNER, SSM_CONV_DIM, SSM_HEADS])
    q, kvc, kvs, kvw, gt, z, xbc, dtr = (w_in[:, o[i]:o[i + 1]] for i in range(8))
    gd = jnp.zeros((d, LANE), F32).at[:, :3 * NSA_HEADS].set(gt).at[:, DT_LANE:DT_LANE + SSM_HEADS].set(dtr)
    w = jnp.concatenate([q * SCALE, kvc, kvs, kvw, z, xbc, gd], axis=1).astype(BF16)
    qw = NSA_WIDTH
    off = {"q": (0, qw), "kvc": (qw, KV_ROW), "kvs": (qw + KV_ROW, KV_ROW), "kvw": (qw + 2 * KV_ROW, KV_ROW),
           "kv3": (qw, 3 * KV_ROW), "z": (qw + 3 * KV_ROW, SSM_D_INNER),
           "xbc": (qw + 3 * KV_ROW + SSM_D_INNER, SSM_CONV_DIM),
           "gd": (qw + 3 * KV_ROW + SSM_D_INNER + SSM_CONV_DIM, LANE)}
    return w, off


def _even_layer(x, past, p, cache):
    (norm_g, w_in, w_out, cw, ssm_p) = p
    b, seq, d = x.shape
    m = b * seq
    decode = cache is not None
    w, off = _even_weights(w_in)
    conv_w, conv_b, dt_bias, a_log, d_skip, ssm_norm = ssm_p
    if not decode:
        names = [("row", "q", F32), ("chan", "kvc", F32), ("chan", "kvs", F32), ("chan", "kvw", F32),
                 ("page", "kvc", F32), ("row", "kv3", BF16), ("page", "kv3", BF16),
                 ("row", "z", F32), ("row", "xbc", F32), ("row", "gd", F32)]
        q, kvc_t, kvs_t, kvw_t, kvc_pages, x_rows, x_pages, z, xbc, gd = norm_proj(
            x, norm_g, w, [(k,) + off[n] for k, n, _ in names], [dt for _, _, dt in names])
        q = q.reshape(b, seq, -1)
        gd3 = gd.reshape(b, seq, LANE)
        x_rows = x_rows.reshape(b, seq, 3 * KV_ROW)
        x_pages = x_pages.reshape(b, seq // TQ, 3 * KV_ROW, TQ)
        kc = compress(kvc_pages.reshape(-1, TQ), cw).reshape(b, seq // CMP_BLOCK, KV_ROW)
        o_win = banded_attention(q, x_rows, x_pages, 2, NSA_KV_HEADS, NSA_GROUP, NSA_WINDOW)
        o_nsa = nsa_prompt(q, gd3, kc, x_rows, x_pages, 1, o_win)
        tail8 = jnp.zeros((b, SUBLANE, SSM_CONV_DIM), F32)
        s0 = jnp.zeros((b, SSM_D_INNER, SSM_STATE), F32)
        xbc3 = xbc.reshape(b, seq, -1)
        o_ssm, s_fin = ssd_mixer(z.reshape(b, seq, -1), xbc3, gd3, tail8, s0,
                                 conv_w, conv_b, dt_bias, a_log, d_skip, ssm_norm, TQ)
        conv_new = xbc3[:, seq - (SSM_CONV - 1):]
        kv_outs = (_token_major(kvc_t), _token_major(kvs_t), _token_major(kvw_t[:, :, seq - min(NSA_WINDOW, seq):]))
    else:
        cmp_pool_t, slc_pool_t, page_table, win_state_t, conv0, ssm0 = cache
        n_pool, _, page = cmp_pool_t.shape
        assert page == TQ and seq < CMP_BLOCK and TQ % seq == 0
        names = [("row", "q", F32), ("row", "kvc", F32), ("row", "kvs", F32), ("row", "kvw", F32),
                 ("row", "z", F32), ("row", "xbc", F32), ("row", "gd", F32)]
        q, kvc, kvs, kvw, z, xbc, gd = norm_proj(
            x, norm_g, w, [(k,) + off[n] for k, n, _ in names], [dt for _, _, dt in names])
        q = q.reshape(b, seq, -1)
        gd3 = gd.reshape(b, seq, LANE)
        bpp = page // CMP_BLOCK
        kc_pool = compress(cmp_pool_t.reshape(n_pool * KV_ROW, page), cw)
        kc = kc_pool.reshape(n_pool, bpp, KV_ROW)[page_table].reshape(b, past // CMP_BLOCK, KV_ROW)
        kvw3 = kvw.reshape(b, seq, KV_ROW)
        o_win = banded_decode(q, win_state_t, kvw3, NSA_KV_HEADS, NSA_GROUP, NSA_WINDOW, past)
        o_nsa = nsa_decode(page_table, q, gd3, kc, kvs.reshape(b, seq, KV_ROW), o_win, slc_pool_t, past)
        padr = lambda a: jnp.pad(a.reshape(b, seq, -1), ((0, 0), (0, TQ - seq), (0, 0)))
        tail8 = jnp.pad(conv0, ((0, 0), (SUBLANE - (SSM_CONV - 1), 0), (0, 0)))
        o_ssm, s_fin = ssd_mixer(padr(z), padr(xbc), padr(gd), tail8, ssm0.reshape(b, SSM_D_INNER, SSM_STATE),
                                 conv_w, conv_b, dt_bias, a_log, d_skip, ssm_norm, seq)
        o_ssm = o_ssm[:, :seq]
        conv_new = jnp.concatenate([conv0, xbc.reshape(b, seq, -1)], axis=1)[:, -(SSM_CONV - 1):]
        npast = win_state_t.shape[2]
        win_out = jnp.concatenate([win_state_t, jnp.swapaxes(kvw3, 1, 2)], axis=2)[:, :, -npast:]
        kv6 = (b, seq, 2, NSA_KV_HEADS, HEAD_DIM)
        kv_outs = (kvc.reshape(kv6), kvs.reshape(kv6), _token_major(win_out))
    x = proj_res([o_nsa.reshape(m, -1), o_ssm.reshape(m, -1)],
                 [w_out[:NSA_WIDTH].astype(BF16), w_out[NSA_WIDTH:].astype(BF16)], x.reshape(m, d))
    outs = kv_outs + (s_fin.reshape(b, SSM_HEADS, SSM_HEAD_DIM, SSM_STATE), conv_new)
    return x.reshape(b, seq, d), outs


def _odd_layer(x, past, p, swa_state_t):
    norm_g, w_in, w_out, sinks = p
    b, seq, d = x.shape
    m = b * seq
    decode = swa_state_t is not None
    w = jnp.concatenate([w_in[:, :SWA_WIDTH] * SCALE, w_in[:, SWA_WIDTH:]], axis=1).astype(BF16)
    qw = SWA_WIDTH
    if not decode:
        q, kv_t, x_rows, x_pages = norm_proj(
            x, norm_g, w, [("row", 0, qw), ("chan", qw, KV_ROW), ("row", qw, KV_ROW), ("page", qw, KV_ROW)],
            [F32, F32, BF16, BF16])
        o = banded_attention(q.reshape(b, seq, -1), x_rows.reshape(b, seq, KV_ROW),
                             x_pages.reshape(b, seq // TQ, KV_ROW, TQ), 0, SWA_KV_HEADS, SWA_GROUP, SWA_WINDOW, sinks)
        kv_out = _token_major(kv_t[:, :, seq - min(SWA_WINDOW, seq):])
    else:
        q, kv = norm_proj(x, norm_g, w, [("row", 0, qw), ("row", qw, KV_ROW)], [F32, F32])
        kv3 = kv.reshape(b, seq, KV_ROW)
        o = banded_decode(q.reshape(b, seq, -1), swa_state_t, kv3, SWA_KV_HEADS, SWA_GROUP, SWA_WINDOW, past, sinks)
        npast = swa_state_t.shape[2]
        kv_out = _token_major(jnp.concatenate([swa_state_t, jnp.swapaxes(kv3, 1, 2)], axis=2)[:, :, -npast:])
    x = proj_res([o.reshape(m, -1)], [w_out.astype(BF16)], x.reshape(m, d))
    return x.reshape(b, seq, d), kv_out


def kernel(x_prompt, x_sample, cache_nsa_cmp_kv, cache_nsa_slc_kv, state_nsa_win_kv, state_ssm, state_ssm_conv, state_swa_kv, page_table, norm_mix, norm_ffn, norm_final, w_in_even, w_out_even, cmp_pe_k, cmp_w1_k, cmp_w2_k, cmp_pe_v, cmp_w1_v, cmp_w2_v, ssm_conv_w, ssm_conv_b, ssm_dt_bias, ssm_a_log, ssm_d, ssm_norm, w_in_odd, w_out_odd, swa_sinks, w_gate_up, w_down):
    depth = norm_mix.shape[0]
    page = cache_nsa_cmp_kv.shape[2]
    past = page_table.shape[1] * page
    xp, xs = x_prompt, x_sample
    outs_p = [[] for _ in range(6)]
    outs_s = [[] for _ in range(6)]
    for layer in range(depth):
        if layer % 2 == 0:
            e = layer // 2
            cw = _compress_weights(cmp_pe_k[e], cmp_w1_k[e], cmp_w2_k[e], cmp_pe_v[e], cmp_w1_v[e], cmp_w2_v[e])
            ssm_p = (ssm_conv_w[e], ssm_conv_b[e], ssm_dt_bias[e], ssm_a_log[e], ssm_d[e], ssm_norm[e])
            p = (norm_mix[layer], w_in_even[e], w_out_even[e], cw, ssm_p)
            xp, o = _even_layer(xp, 0, p, None)
            for lst, v in zip(outs_p[:5], o):
                lst.append(v)
            cache = (_channel_major(cache_nsa_cmp_kv[e]), _channel_major(cache_nsa_slc_kv[e]), page_table,
                     _channel_major(state_nsa_win_kv[e]), state_ssm_conv[e], state_ssm[e])
            xs, o = _even_layer(xs, past, p, cache)
            for lst, v in zip(outs_s[:5], o):
                lst.append(v)
        else:
            o_idx = layer // 2
            p = (norm_mix[layer], w_in_odd[o_idx], w_out_odd[o_idx], swa_sinks[o_idx])
            xp, kw = _odd_layer(xp, 0, p, None)
            outs_p[5].append(kw)
            xs, kw = _odd_layer(xs, past, p, _channel_major(state_swa_kv[o_idx]))
            outs_s[5].append(kw)
        gf = norm_final if layer == depth - 1 else None
        wgu, wd = w_gate_up[layer].astype(BF16), w_down[layer].astype(BF16)
        xp = ffn(xp.reshape(-1, xp.shape[-1]), norm_ffn[layer], wgu, wd, gf).reshape(xp.shape)
        xs = ffn(xs.reshape(-1, xs.shape[-1]), norm_ffn[layer], wgu, wd, gf).reshape(xs.shape)
    return (xp, xs) + tuple(jnp.stack(v) for v in outs_p) + tuple(jnp.stack(v) for v in outs_s)
```

```python
import functools
import math

import numpy as np
import jax
import jax.numpy as jnp
from jax import lax
from jax.experimental import pallas as pl
from jax.experimental.pallas import tpu as pltpu

F32 = jnp.float32
BF16 = jnp.bfloat16

HEAD_DIM = 64
NSA_HEADS = 8
NSA_KV_HEADS = 2
NSA_GROUP = NSA_HEADS // NSA_KV_HEADS
CMP_BLOCK = 64
SEL_TOPK = 16
NSA_WINDOW = 512
SSM_HEADS = 8
SSM_HEAD_DIM = 64
SSM_D_INNER = SSM_HEADS * SSM_HEAD_DIM
SSM_GROUPS = 2
SSM_STATE = 128
SSM_CONV = 4
SSM_CONV_DIM = SSM_D_INNER + 2 * SSM_GROUPS * SSM_STATE
SWA_HEADS = 16
SWA_KV_HEADS = 2
SWA_GROUP = SWA_HEADS // SWA_KV_HEADS
SWA_WINDOW = 128
RMS_EPS = 1e-6
NSA_WIDTH = NSA_HEADS * HEAD_DIM
SWA_WIDTH = SWA_HEADS * HEAD_DIM
SCALE = HEAD_DIM ** -0.5

LANE = 128
SUBLANE = 8
KV_ROW = 2 * NSA_KV_HEADS * HEAD_DIM
TQ = 128
NEG = -1e30
MASK = -2e30
VMEM_LIMIT = 56 * 1024 * 1024
DT_LANE = 3 * NSA_HEADS
N_STREAMS = 1
NSA_CHUNK_TILES = 4

def _cparams(*sem):
    return pltpu.CompilerParams(dimension_semantics=sem, vmem_limit_bytes=VMEM_LIMIT)


def _dot(a, b):
    return jnp.dot(a, b, preferred_element_type=F32)


def _dot_nt(a, b):
    return lax.dot_general(a, b, (((1,), (1,)), ((), ())), preferred_element_type=F32)


def _rms(x, g):
    return x * lax.rsqrt(jnp.mean(x * x, axis=-1, keepdims=True) + RMS_EPS) * g


def _sigmoid(x):
    return 1.0 / (1.0 + jnp.exp(-x))


def _alibi(n_heads):
    return [float(2.0 ** (-8.0 * i / n_heads)) for i in range(1, n_heads + 1)]


def _col(vals, rows):
    return jnp.concatenate([jnp.full((rows, 1), v, F32) for v in vals], axis=0)


def _row(vals, cols):
    return jnp.concatenate([jnp.full((1, cols), v, F32) for v in vals], axis=1)


def _pair_lanes(c0, c1):
    lane = lax.broadcasted_iota(jnp.int32, (c0.shape[0], LANE), 1)
    return jnp.where(lane < HEAD_DIM, c0, c1)


def _norm_proj_kernel(x_ref, g_ref, w_ref, *o_refs, pieces):
    hb = _rms(x_ref[...], g_ref[...]).astype(BF16)
    cache = {}
    spans = sorted({(off, n) for _, off, n in pieces}, key=lambda s: -s[1])
    for o_ref, (kind, off, n) in zip(o_refs, pieces):
        p_off, p_n = next((o2, n2) for o2, n2 in spans if o2 <= off and off + n <= o2 + n2)
        if (p_off, p_n) not in cache:
            cache[(p_off, p_n)] = _dot(hb, w_ref[:, p_off:p_off + p_n])
        if (off, n) not in cache:
            cache[(off, n)] = cache[(p_off, p_n)][:, off - p_off:off - p_off + n]
        y = cache[(off, n)]
        if kind == "row":
            o_ref[...] = y.astype(o_ref.dtype)
        else:
            if ("t", off, n) not in cache:
                cache[("t", off, n)] = y.T
            y_t = cache[("t", off, n)]
            if kind == "chan":
                o_ref[0] = y_t.astype(o_ref.dtype)
            else:
                for j in range(o_ref.shape[0]):
                    o_ref[j] = y_t[:, j * TQ:(j + 1) * TQ].astype(o_ref.dtype)


def norm_proj(x3d, g, w_bf, pieces, dtypes, tm=256):
    b, seq, d = x3d.shape
    m = b * seq
    n_tot = w_bf.shape[1]
    tm = min(tm, m)
    per_b = seq // tm if seq >= tm else 1
    out_specs, out_shape = [], []
    for (kind, _, n), dt in zip(pieces, dtypes):
        if kind == "row":
            out_specs.append(pl.BlockSpec((tm, n), lambda i: (i, 0)))
            out_shape.append(jax.ShapeDtypeStruct((m, n), dt))
        elif kind == "chan":
            out_specs.append(pl.BlockSpec((1, n, tm), lambda i: (i // per_b, 0, i % per_b)))
            out_shape.append(jax.ShapeDtypeStruct((b, n, seq), dt))
        else:
            out_specs.append(pl.BlockSpec((tm // TQ, n, TQ), lambda i: (i, 0, 0)))
            out_shape.append(jax.ShapeDtypeStruct((m // TQ, n, TQ), dt))
    return pl.pallas_call(
        functools.partial(_norm_proj_kernel, pieces=tuple(pieces)),
        grid=(m // tm,),
        in_specs=[pl.BlockSpec((tm, d), lambda i: (i, 0)),
                  pl.BlockSpec((1, d), lambda i: (0, 0)),
                  pl.BlockSpec((d, n_tot), lambda i: (0, 0))],
        out_specs=out_specs,
        out_shape=out_shape,
        compiler_params=_cparams("parallel"),
        name="norm_proj",
    )(x3d.reshape(m, d), g.reshape(1, d), w_bf)


def _proj_res_kernel(*refs, n_in):
    a_refs, w_refs = refs[:n_in], refs[n_in:2 * n_in]
    res_ref, o_ref = refs[2 * n_in], refs[2 * n_in + 1]
    acc = res_ref[...]
    for a_ref, w_ref in zip(a_refs, w_refs):
        acc = acc + _dot(a_ref[...].astype(BF16), w_ref[...])
    o_ref[...] = acc


def proj_res(a_list, w_list, res, tm=512):
    m, d = res.shape
    tm = min(tm, m)
    n_in = len(a_list)
    in_specs = ([pl.BlockSpec((tm, a.shape[1]), lambda i: (i, 0)) for a in a_list]
                + [pl.BlockSpec(w.shape, lambda i: (0, 0)) for w in w_list]
                + [pl.BlockSpec((tm, d), lambda i: (i, 0))])
    return pl.pallas_call(
        functools.partial(_proj_res_kernel, n_in=n_in),
        grid=(m // tm,),
        in_specs=in_specs,
        out_specs=pl.BlockSpec((tm, d), lambda i: (i, 0)),
        out_shape=jax.ShapeDtypeStruct((m, d), F32),
        compiler_params=_cparams("parallel"),
        name="proj_res",
    )(*a_list, *w_list, res)


def _ffn_kernel(x_ref, g_ref, wg_ref, wu_ref, wd_ref, *rest, final):
    if final:
        gf_ref, o_ref, h_scr, acc_scr = rest
    else:
        o_ref, h_scr, acc_scr = rest
    j = pl.program_id(1)

    @pl.when(j == 0)
    def _():
        x = x_ref[...]
        h_scr[...] = _rms(x, g_ref[...]).astype(BF16)
        acc_scr[...] = x

    hb = h_scr[...]
    gate = _dot(hb, wg_ref[...])
    up = _dot(hb, wu_ref[...])
    act = gate * _sigmoid(gate) * up
    acc_scr[...] += _dot(act.astype(BF16), wd_ref[...])

    @pl.when(j == pl.num_programs(1) - 1)
    def _():
        y = acc_scr[...]
        if final:
            y = _rms(y, gf_ref[...])
        o_ref[...] = y


def ffn(x2d, g, w_gu_bf, w_down_bf, g_final=None, tm=512):
    m, d = x2d.shape
    f = w_down_bf.shape[0]
    tm = min(tm, m)
    nf = 2 if (f // 2) % LANE == 0 else 1
    tf = f // nf
    final = g_final is not None
    in_specs = [pl.BlockSpec((tm, d), lambda i, j: (i, 0)),
                pl.BlockSpec((1, d), lambda i, j: (0, 0)),
                pl.BlockSpec((d, tf), lambda i, j: (0, j)),
                pl.BlockSpec((d, tf), lambda i, j: (0, j + nf)),
                pl.BlockSpec((tf, d), lambda i, j: (j, 0))]
    args = [x2d, g.reshape(1, d), w_gu_bf, w_gu_bf, w_down_bf]
    if final:
        in_specs.append(pl.BlockSpec((1, d), lambda i, j: (0, 0)))
        args.append(g_final.reshape(1, d))
    return pl.pallas_call(
        functools.partial(_ffn_kernel, final=final),
        grid=(m // tm, nf),
        in_specs=in_specs,
        out_specs=pl.BlockSpec((tm, d), lambda i, j: (i, 0)),
        out_shape=jax.ShapeDtypeStruct((m, d), F32),
        scratch_shapes=[pltpu.VMEM((tm, d), BF16), pltpu.VMEM((tm, d), F32)],
        compiler_params=_cparams("parallel", "arbitrary"),
        name="ffn",
    )(*args)


def _gelu_tanh(x):
    c = math.sqrt(2.0 / math.pi)
    return x * (0.5 * (1.0 + jnp.tanh(c * (x + 0.044715 * (x * x * x)))))


def _transpose8(v):
    row = lax.broadcasted_iota(jnp.int32, (SUBLANE, LANE), 0)
    for s in (4, 2, 1):
        low = (row & s) == 0
        out = list(v)
        for i in range(SUBLANE):
            if i & s == 0:
                out[i] = jnp.where(low, v[i], pltpu.roll(v[i + s], s, 0))
                out[i + s] = jnp.where(low, pltpu.roll(v[i], SUBLANE - s, 0), v[i + s])
        v = out
    return v


def _compress_kernel(x_ref, pe_ref, w1k_ref, w1v_ref, w2_ref, o_ref, *, n_pages):
    hidden = []
    for kv, w1_ref in enumerate((w1k_ref, w1v_ref)):
        slabs = []
        for c in range(kv * NSA_KV_HEADS, (kv + 1) * NSA_KV_HEADS):
            per_d = [[] for _ in range(HEAD_DIM)]
            for p0 in range(0, n_pages, SUBLANE):
                for d0 in range(0, HEAD_DIM, SUBLANE):
                    tiles = [x_ref[pl.ds((p0 + p) * KV_ROW + c * HEAD_DIM + d0, SUBLANE), :] for p in range(SUBLANE)]
                    for r, u in enumerate(_transpose8(tiles)):
                        per_d[d0 + r].append(u)
            rows = [jnp.concatenate(per_d[d], axis=0) for d in range(HEAD_DIM)]
            slabs.append(jnp.concatenate(rows, axis=1).astype(BF16))
        h = _gelu_tanh(_dot(jnp.concatenate(slabs, axis=0), w1_ref[...]) + pe_ref[kv:kv + 1, :]).astype(BF16)
        hidden += [h[i * n_pages:(i + 1) * n_pages] for i in range(NSA_KV_HEADS)]
    o_ref[...] = _dot(jnp.concatenate(hidden, axis=1), w2_ref[...]).astype(o_ref.dtype)


def compress(pages2d, cw, pages_per_step=64):
    pe4, w1k, w1v, w2 = cw
    n_pages = pages2d.shape[0] // KV_ROW
    pps = math.gcd(pages_per_step, n_pages)
    blocks_per_page = TQ // CMP_BLOCK
    const = lambda a: pl.BlockSpec(a.shape, lambda i: (0, 0))
    return pl.pallas_call(
        functools.partial(_compress_kernel, n_pages=pps),
        grid=(n_pages // pps,),
        in_specs=[pl.BlockSpec((pps * KV_ROW, TQ), lambda i: (i, 0)), const(pe4), const(w1k), const(w1v), const(w2)],
        out_specs=pl.BlockSpec((pps, blocks_per_page * KV_ROW), lambda i: (i, 0)),
        out_shape=jax.ShapeDtypeStruct((n_pages, blocks_per_page * KV_ROW), BF16),
        compiler_params=_cparams("parallel"),
        name="compress",
    )(pages2d, pe4, w1k, w1v, w2)


def _compress_weights(pe_k, w1_k, w2_k, pe_v, w1_v, w2_v):
    nj = TQ // CMP_BLOCK
    nc = 2 * NSA_KV_HEADS
    l, d, h = w1_k.shape

    def first(w1):
        w = jnp.zeros((d, nj, l, nj, h), F32)
        for j in range(nj):
            w = w.at[:, j, :, j, :].set(jnp.transpose(w1, (1, 0, 2)))
        return w.reshape(d * nj * l, nj * h).astype(BF16)

    def pe_row(pe):
        return jnp.broadcast_to(pe.T[:, None, :], (d, nj, l)).reshape(1, d * nj * l)

    w2 = jnp.zeros((nc, nj, h, nj, nc, w2_k.shape[1]), F32)
    for c in range(nc):
        for j in range(nj):
            w2 = w2.at[c, j, :, j, c, :].set(w2_k if c < NSA_KV_HEADS else w2_v)
    w1k, w1v = first(w1_k), first(w1_v)
    pe_hidden = jnp.concatenate([jnp.dot(pe_row(pe), w1.astype(F32), precision=lax.Precision.HIGHEST)
                                 for pe, w1 in ((pe_k, w1k), (pe_v, w1v))], axis=0)
    return pe_hidden, w1k, w1v, w2.reshape(nc * nj * h, nj * nc * w2_k.shape[1]).astype(BF16)


def _q_rows(q_ref, heads, n_rep, bi=0):
    assert NSA_KV_HEADS * HEAD_DIM == LANE
    nq = q_ref.shape[1]
    lane_half = lax.broadcasted_iota(jnp.int32, (nq, LANE), 1) // HEAD_DIM
    rows = []
    for h in heads:
        g = h // n_rep
        tile = q_ref[bi, :, (h // 2) * LANE:(h // 2 + 1) * LANE]
        if h % 2 != g:
            tile = pltpu.roll(tile, HEAD_DIM, 1)
        rows.append(jnp.where(lane_half == g, tile, 0.0))
    q_keys = jnp.concatenate(rows, axis=0)
    return jnp.concatenate([q_keys, jnp.zeros_like(q_keys)], axis=1).astype(BF16)


def _heads_to_rows(o_t, n_rep, pair):
    r0 = 2 * pair
    two = jnp.concatenate([o_t[:, r0 * TQ:(r0 + 1) * TQ], o_t[:, (r0 + 1) * TQ:(r0 + 2) * TQ]], axis=0)
    return two.T


def _banded_kernel(*refs, n_groups, n_rep, window, slopes, has_sink, kv_block):
    if has_sink:
        sink_ref, q_ref, x_ref, xt_ref, o_ref = refs
    else:
        q_ref, x_ref, xt_ref, o_ref = refs
    qt = pl.program_id(1)
    n_heads = n_groups * n_rep
    nrow = n_heads * TQ
    n_tiles = window // TQ + 1
    ki = lax.broadcasted_iota(jnp.int32, (TQ, nrow), 0)
    qi = lax.broadcasted_iota(jnp.int32, (TQ, nrow), 1) % TQ
    diff = qi - ki
    diff_f = diff.astype(F32)
    tiles = []
    for j in range(n_tiles):
        kt = qt - (n_tiles - 1 - j)
        ktc = jnp.maximum(kt, 0)
        x = x_ref[0, pl.ds(pl.multiple_of(ktc * TQ, TQ), TQ), :]
        tiles.append((x, ktc, jnp.where(kt >= 0, 0.0, MASK)))
    heads = list(range(n_heads))
    qrows = jnp.concatenate([_q_rows(q_ref, heads[g * n_rep:(g + 1) * n_rep], n_rep) for g in range(n_groups)], axis=0)
    slope_row = _row([slopes[h] for h in heads], TQ)
    bias0 = slope_row * diff_f
    s_list = []
    for j, (x, _, before_start) in enumerate(tiles):
        s = _dot_nt(x, qrows) - bias0 + (before_start - slope_row * float((n_tiles - 1 - j) * TQ))
        if j == 0:
            s = jnp.where(diff <= 0, s, MASK)
        if j == n_tiles - 1:
            s = jnp.where(diff >= 0, s, MASK)
        s_list.append(s)
    m = s_list[0].max(axis=0, keepdims=True)
    for s in s_list[1:]:
        m = jnp.maximum(m, s.max(axis=0, keepdims=True))
    if has_sink:
        sink_row = jnp.concatenate([jnp.full((1, TQ), sink_ref[h], F32) for h in heads], axis=1)
        m = jnp.maximum(m, sink_row)
    p_list = [jnp.exp(s - m) for s in s_list]
    den = p_list[0].sum(axis=0, keepdims=True)
    for p in p_list[1:]:
        den = den + p.sum(axis=0, keepdims=True)
    if has_sink:
        den = den + jnp.exp(sink_row - m)
    p_all = jnp.concatenate([p.astype(BF16) for p in p_list], axis=0)
    for g in range(n_groups):
        cols = slice(g * n_rep * TQ, (g + 1) * n_rep * TQ)
        v0 = kv_block * KV_ROW + (NSA_KV_HEADS + g) * HEAD_DIM
        v_all = jnp.concatenate([xt_ref[0, ktc, v0:v0 + HEAD_DIM, :] for _, ktc, _ in tiles], axis=1)
        o_t = _dot(v_all, p_all[:, cols]) / jnp.maximum(den[:, cols], 1e-30)
        for pair in range(n_rep // 2):
            h0 = g * n_rep + 2 * pair
            o_ref[0, :, h0 * HEAD_DIM:(h0 + 2) * HEAD_DIM] = _heads_to_rows(o_t, n_rep, pair)


def banded_attention(q_pad, x_rows, x_pages, kv_block, n_groups, n_rep, window, sinks=None):
    b, seq = q_pad.shape[:2]
    n_heads = n_groups * n_rep
    has_sink = sinks is not None
    in_specs = [pl.BlockSpec((1, TQ, n_heads * HEAD_DIM), lambda i, j: (i, j, 0)),
                pl.BlockSpec((1, seq, KV_ROW), lambda i, j: (i, 0, kv_block)),
                pl.BlockSpec((1,) + x_pages.shape[1:], lambda i, j: (i, 0, 0, 0))]
    args = [q_pad, x_rows, x_pages]
    if has_sink:
        in_specs.insert(0, pl.BlockSpec(memory_space=pltpu.SMEM))
        args.insert(0, sinks)
    assert window % TQ == 0 and window >= TQ
    return pl.pallas_call(
        functools.partial(_banded_kernel, n_groups=n_groups, n_rep=n_rep, window=window,
                          slopes=_alibi(n_heads), has_sink=has_sink, kv_block=kv_block),
        grid=(b, seq // TQ),
        in_specs=in_specs,
        out_specs=pl.BlockSpec((1, TQ, n_heads * HEAD_DIM), lambda i, j: (i, j, 0)),
        out_shape=jax.ShapeDtypeStruct((b, seq, n_heads * HEAD_DIM), F32),
        compiler_params=_cparams("parallel", "parallel"),
        name="banded_attention",
    )(*args)


def _banded_decode_kernel(*refs, n_groups, n_rep, window, slopes, has_sink, pos0):
    if has_sink:
        sink_ref, q_ref, st_ref, new_ref, o_ref = refs
    else:
        q_ref, st_ref, new_ref, o_ref = refs
    nq = q_ref.shape[1]
    npast = st_ref.shape[2]
    nrow = n_rep * nq
    qi_s = jnp.concatenate([lax.broadcasted_iota(jnp.int32, (nq, npast), 0)] * n_rep, axis=0)
    kj_s = lax.broadcasted_iota(jnp.int32, (nrow, npast), 1)
    dist_s = npast + qi_s - kj_s
    valid_s = (dist_s >= 0) & (dist_s <= window) & (pos0 - npast + kj_s >= 0)
    qi_n = jnp.concatenate([lax.broadcasted_iota(jnp.int32, (nq, TQ), 0)] * n_rep, axis=0)
    kj_n = lax.broadcasted_iota(jnp.int32, (nrow, TQ), 1)
    dist_n = qi_n - kj_n
    valid_n = (dist_n >= 0) & (dist_n <= window) & (kj_n < nq)
    for bi, g in [(bi, g) for bi in range(q_ref.shape[0]) for g in range(n_groups)]:
        xs_t = st_ref[bi].astype(BF16)
        xn = jnp.concatenate([new_ref[bi], jnp.zeros((TQ - nq, KV_ROW), F32)], axis=0).astype(BF16)
        heads = [g * n_rep + r for r in range(n_rep)]
        qrows = _q_rows(q_ref, heads, n_rep, bi)
        slope_col = _col([slopes[h] for h in heads], nq)
        s_s = jnp.where(valid_s, _dot(qrows, xs_t) - slope_col * dist_s.astype(F32), NEG)
        s_n = jnp.where(valid_n, _dot_nt(qrows, xn) - slope_col * dist_n.astype(F32), NEG)
        m = jnp.maximum(jnp.max(s_s, axis=-1, keepdims=True), jnp.max(s_n, axis=-1, keepdims=True))
        if has_sink:
            sink_col = jnp.concatenate([jnp.full((nq, 1), sink_ref[h], F32) for h in heads], axis=0)
            m = jnp.maximum(m, sink_col)
        p_s = jnp.where(valid_s, jnp.exp(s_s - m), 0.0)
        p_n = jnp.where(valid_n, jnp.exp(s_n - m), 0.0)
        den = jnp.sum(p_s, axis=-1, keepdims=True) + jnp.sum(p_n, axis=-1, keepdims=True)
        if has_sink:
            den = den + jnp.exp(sink_col - m)
        o = (_dot_nt(p_s.astype(BF16), xs_t) + _dot(p_n.astype(BF16), xn)) / jnp.maximum(den, 1e-30)
        v0 = (NSA_KV_HEADS + g) * HEAD_DIM
        for r, h in enumerate(heads):
            o_ref[bi, :, h * HEAD_DIM:(h + 1) * HEAD_DIM] = o[r * nq:(r + 1) * nq, v0:v0 + HEAD_DIM]


def banded_decode(q_pad, state_t, new_kv, n_groups, n_rep, window, pos0, sinks=None, rows_per_step=4):
    b, nq = q_pad.shape[:2]
    npast = state_t.shape[2]
    n_heads = n_groups * n_rep
    has_sink = sinks is not None
    bb = math.gcd(rows_per_step, b)
    in_specs = [pl.BlockSpec((bb, nq, n_heads * HEAD_DIM), lambda i: (i, 0, 0)),
                pl.BlockSpec((bb, KV_ROW, npast), lambda i: (i, 0, 0)),
                pl.BlockSpec((bb, nq, KV_ROW), lambda i: (i, 0, 0))]
    args = [q_pad, state_t, new_kv]
    if has_sink:
        in_specs.insert(0, pl.BlockSpec(memory_space=pltpu.SMEM))
        args.insert(0, sinks)
    return pl.pallas_call(
        functools.partial(_banded_decode_kernel, n_groups=n_groups, n_rep=n_rep, window=window,
                          slopes=_alibi(n_heads), has_sink=has_sink, pos0=pos0),
        grid=(b // bb,),
        in_specs=in_specs,
        out_specs=pl.BlockSpec((bb, nq, n_heads * HEAD_DIM), lambda i: (i, 0, 0)),
        out_shape=jax.ShapeDtypeStruct((b, nq, n_heads * HEAD_DIM), F32),
        compiler_params=_cparams("parallel"),
        name="banded_decode",
    )(*args)


def _select_blocks_t(imp, cur, n_sel):
    nb = imp.shape[0]
    nrow = lax.broadcasted_iota(jnp.int32, imp.shape, 0)
    count = jnp.zeros(imp.shape, F32)
    for i in range(nb):
        ci = imp[i:i + 1, :]
        ahead = jnp.where(ci > imp, 1.0, jnp.where(ci == imp, jnp.where(nrow > i, 1.0, 0.0), 0.0))
        count = count + jnp.where(cur > i, ahead, 0.0)
    return jnp.where(nrow < cur, jnp.where(count < n_sel - 1, 1.0, 0.0), jnp.where(nrow == cur, 1.0, 0.0))


def _nsa_prompt_kernel(q_ref, gd_ref, kc_ref, kct_ref, x_ref, xt_ref, ow_ref, o_ref, m_scr, acc_scr,
                       *, n_sel, kv_block):
    qt = pl.program_id(1)
    n_rep = NSA_GROUP
    nrow = n_rep * TQ
    slopes = _alibi(NSA_HEADS)
    nb = kc_ref.shape[1]
    kcb = kc_ref[0]
    gates = _sigmoid(gd_ref[0])
    n_ct = NSA_CHUNK_TILES
    tk = n_ct * TQ
    n_seq_tiles = x_ref.shape[1] // TQ
    ki = lax.broadcasted_iota(jnp.int32, (tk, NSA_KV_HEADS * nrow), 0)
    qi = lax.broadcasted_iota(jnp.int32, (tk, NSA_KV_HEADS * nrow), 1) % TQ
    diff = qi - ki
    tpos = qt * TQ + lax.broadcasted_iota(jnp.int32, (nb, nrow), 1) % TQ
    blk = lax.broadcasted_iota(jnp.int32, (nb, nrow), 0)
    dist_c = tpos - ((blk + 1) * CMP_BLOCK - 1)
    valid_c = dist_c >= 0
    cur = (qt * TQ + lax.broadcasted_iota(jnp.int32, (nb, TQ), 1)) // CMP_BLOCK
    assert nb + 1 < LANE and all(math.frexp(sl)[0] == 0.5 for sl in slopes)
    lane_x = lax.broadcasted_iota(jnp.int32, (tk, LANE), 1)
    key_in_chunk = lax.broadcasted_iota(jnp.int32, (tk, LANE), 0)
    key_blk = key_in_chunk // CMP_BLOCK
    key_lo = (key_in_chunk % TQ).astype(F32)
    key_hi = (key_in_chunk // TQ).astype(F32)
    lane_q = lax.broadcasted_iota(jnp.int32, (TQ, LANE), 1)
    ones_rows = jnp.ones((2 * SUBLANE, tk), BF16)
    groups = []
    for g in range(NSA_KV_HEADS):
        heads = [g * n_rep + r for r in range(n_rep)]
        qrows = _q_rows(q_ref, heads, n_rep)
        slope_row = _row([slopes[h] for h in heads], TQ)
        vrow = (NSA_KV_HEADS + g) * HEAD_DIM
        s_c = jnp.where(valid_c, _dot_nt(kcb, qrows) - slope_row * dist_c.astype(F32), NEG)
        e_c = jnp.where(valid_c, jnp.exp(s_c - jnp.max(s_c, axis=0, keepdims=True)), 0.0)
        p_c = e_c / jnp.maximum(jnp.sum(e_c, axis=0, keepdims=True), 1e-30)
        ocmp_t = _dot(kct_ref[0, vrow:vrow + HEAD_DIM, :], p_c.astype(BF16))
        imp = p_c[:, 0:TQ]
        for r in range(1, n_rep):
            imp = imp + p_c[:, r * TQ:(r + 1) * TQ]
        taken = _select_blocks_t(imp, cur, n_sel)
        taken_q = jnp.concatenate([taken, jnp.zeros((LANE - nb, TQ), F32)], axis=0).T
        block_mask = jnp.where(taken_q > 0.5, 0.0, MASK)
        q_extra = jnp.concatenate([jnp.where(lane_q == nb, slopes[h], jnp.where(lane_q == nb + 1, slopes[h] * TQ,
                                                                                 block_mask)) for h in heads], axis=0)
        q_aug = jnp.concatenate([qrows, q_extra.astype(BF16)], axis=1)
        acc_scr[g] = jnp.zeros(acc_scr.shape[1:], F32)
        groups.append((heads, q_aug, slope_row, kv_block * KV_ROW + vrow, ocmp_t))
    q_all = jnp.concatenate([grp[1] for grp in groups], axis=0)
    slope_all = jnp.concatenate([grp[2] for grp in groups], axis=1)
    m_scr[...] = jnp.full(m_scr.shape, NEG, F32)

    def chunk(kt0, causal):
        kts = [jnp.minimum(kt0 + i, n_seq_tiles - 1) for i in range(n_ct)]
        x_extra = jnp.where(lane_x == nb, key_lo, jnp.where(lane_x == nb + 1, key_hi,
                            jnp.where(lane_x == key_blk + kt0 * (TQ // CMP_BLOCK), 1.0, 0.0))).astype(BF16)
        x = jnp.concatenate([x_ref[0, pl.ds(pl.multiple_of(kt * TQ, TQ), TQ), :] for kt in kts], axis=0)
        x_aug = jnp.concatenate([x, x_extra], axis=1)
        off = (qt - kt0) * TQ
        s = _dot_nt(x_aug, q_all)
        if causal:
            s = jnp.where(diff + off >= 0, s, MASK)
        shift = slope_all * off.astype(F32)
        m_old = m_scr[...]
        m_new = jnp.maximum(m_old, jnp.max(s, axis=0, keepdims=True) - shift)
        p = jnp.exp(s - (m_new + shift)).astype(BF16)
        alpha = jnp.exp(m_old - m_new)
        m_scr[...] = m_new
        for g, (_, _, _, v0, _) in enumerate(groups):
            cols = slice(g * nrow, (g + 1) * nrow)
            v_t = jnp.concatenate([xt_ref[0, kt, v0:v0 + HEAD_DIM, :] for kt in kts], axis=1)
            acc_scr[g] = alpha[:, cols] * acc_scr[g] + _dot(jnp.concatenate([v_t, ones_rows], axis=0), p[:, cols])

    n_full = qt // n_ct

    def body(j, carry):
        chunk(n_ct * j, False)
        return carry

    lax.fori_loop(0, n_full, body, 0)
    chunk(n_ct * n_full, True)
    for g, (heads, _, _, _, ocmp_t) in enumerate(groups):
        oslc_t = acc_scr[g, 0:HEAD_DIM, :] / jnp.maximum(acc_scr[g, HEAD_DIM:HEAD_DIM + 1, :], 1e-30)
        for pair in range(n_rep // 2):
            h0 = heads[2 * pair]
            hh = slice(h0 * HEAD_DIM, (h0 + 2) * HEAD_DIM)
            gate = lambda j: _pair_lanes(gates[:, 3 * h0 + j:3 * h0 + j + 1], gates[:, 3 * h0 + 3 + j:3 * h0 + 4 + j])
            o_ref[0, :, hh] = (gate(0) * _heads_to_rows(ocmp_t, n_rep, pair)
                               + gate(1) * _heads_to_rows(oslc_t, n_rep, pair)
                               + gate(2) * ow_ref[0, :, hh])


def nsa_prompt(q_pad, gd, kc, x_rows, x_pages, kv_block, o_win):
    b, seq = q_pad.shape[:2]
    nb = kc.shape[1]
    nrow = NSA_GROUP * TQ
    kct = jnp.swapaxes(kc, 1, 2)
    return pl.pallas_call(
        functools.partial(_nsa_prompt_kernel, n_sel=min(SEL_TOPK, nb), kv_block=kv_block),
        grid=(b, seq // TQ),
        in_specs=[pl.BlockSpec((1, TQ, NSA_WIDTH), lambda i, j: (i, j, 0)),
                  pl.BlockSpec((1, TQ, LANE), lambda i, j: (i, j, 0)),
                  pl.BlockSpec((1, nb, KV_ROW), lambda i, j: (i, 0, 0)),
                  pl.BlockSpec((1, KV_ROW, nb), lambda i, j: (i, 0, 0)),
                  pl.BlockSpec((1, seq, KV_ROW), lambda i, j: (i, 0, kv_block)),
                  pl.BlockSpec((1,) + x_pages.shape[1:], lambda i, j: (i, 0, 0, 0)),
                  pl.BlockSpec((1, TQ, NSA_WIDTH), lambda i, j: (i, j, 0))],
        out_specs=pl.BlockSpec((1, TQ, NSA_WIDTH), lambda i, j: (i, j, 0)),
        out_shape=jax.ShapeDtypeStruct((b, seq, NSA_WIDTH), F32),
        scratch_shapes=[pltpu.VMEM((1, NSA_KV_HEADS * nrow), F32),
                        pltpu.VMEM((NSA_KV_HEADS, HEAD_DIM + 2 * SUBLANE, nrow), F32)],
        compiler_params=_cparams("parallel", "parallel"),
        name="nsa_prompt",
    )(q_pad, gd, kc, kct, x_rows, x_pages, o_win)


def _select_blocks(imp, n_sel):
    nb = imp.shape[1]
    ncol = lax.broadcasted_iota(jnp.int32, imp.shape, 1)
    count = jnp.zeros(imp.shape, F32)
    for i in range(nb):
        ci = imp[:, i:i + 1]
        count = count + jnp.where(ci > imp, 1.0, jnp.where(ci == imp, jnp.where(ncol > i, 1.0, 0.0), 0.0))
    return jnp.where(count < n_sel - 1, 1.0, 0.0)


def _nsa_decode_kernel(pt_ref, q_ref, gd_ref, kc_ref, new_ref, ow_ref, ex_ref, *rest, pages_per_step, pos0, n_sel):
    page_refs = rest[:pages_per_step]
    o_ref, m_scr, l_scr, acc_scr, mask_scr, bias_scr, ocmp_scr, q_scr = rest[pages_per_step:]
    p = pl.program_id(1)
    n_rep = NSA_GROUP
    nq = q_ref.shape[1]
    ngrp = n_rep * nq
    nrow = NSA_KV_HEADS * ngrp
    nb = kc_ref.shape[1]
    page = page_refs[0].shape[1]
    nkey = pages_per_step * page
    slopes = _alibi(NSA_HEADS)
    slope_col = _col(slopes, nq)

    def per_query(a):
        return jnp.concatenate([a[g] for g in range(NSA_KV_HEADS) for _ in range(n_rep)], axis=0)

    @pl.when(p == 0)
    def _():
        q_scr[...] = jnp.concatenate([_q_rows(q_ref, [g * n_rep + r for r in range(n_rep)], n_rep)
                                      for g in range(NSA_KV_HEADS)], axis=0)

    qrows = q_scr[...]

    @pl.when(p == 0)
    def _():
        kcb = kc_ref[0]
        tpos = pos0 + jnp.concatenate([lax.broadcasted_iota(jnp.int32, (nq, nb), 0)] * (nrow // nq), axis=0)
        ncol = lax.broadcasted_iota(jnp.int32, (nrow, nb), 1)
        dist = tpos - ((ncol + 1) * CMP_BLOCK - 1)
        valid = dist >= 0
        s = jnp.where(valid, _dot_nt(qrows, kcb) - slope_col * dist.astype(F32), NEG)
        e = jnp.where(valid, jnp.exp(s - jnp.max(s, axis=-1, keepdims=True)), 0.0)
        pc = e / jnp.maximum(jnp.sum(e, axis=-1, keepdims=True), 1e-30)
        ocmp_scr[...] = _dot(pc.astype(BF16), kcb)
        not_taken = []
        for g in range(NSA_KV_HEADS):
            imp = pc[g * ngrp:g * ngrp + nq]
            for r in range(1, n_rep):
                imp = imp + pc[g * ngrp + r * nq:g * ngrp + (r + 1) * nq]
            not_taken.append(jnp.where(_select_blocks(imp, n_sel) > 0.5, 0.0, MASK))
        not_taken = jnp.concatenate(not_taken, axis=0).astype(BF16)
        for st in range(mask_scr.shape[0]):
            mask_scr[st] = _dot(not_taken, ex_ref[:, st * nkey:(st + 1) * nkey])
        qi = jnp.concatenate([lax.broadcasted_iota(jnp.int32, (nq, nkey), 0)] * (nrow // nq), axis=0)
        kj = lax.broadcasted_iota(jnp.int32, (nrow, nkey), 1)
        bias_scr[...] = slope_col * (pos0 + qi - kj).astype(F32)
        m_scr[...] = jnp.full_like(m_scr, NEG)
        l_scr[...] = jnp.zeros_like(l_scr)
        acc_scr[...] = jnp.zeros_like(acc_scr)

    mask = mask_scr[p]
    shift = slope_col * (p * nkey).astype(F32)
    per = pages_per_step // N_STREAMS
    for st in range(N_STREAMS):
        ks = slice(st * per * page, (st + 1) * per * page)
        x_t = jnp.concatenate([r[...] for r in page_refs[st * per:(st + 1) * per]], axis=1).astype(BF16)
        s = (_dot(qrows, x_t) - bias_scr[:, ks]
             + per_query([mask[g * nq:(g + 1) * nq, ks] for g in range(NSA_KV_HEADS)]))
        m_old = m_scr[st]
        m_new = jnp.maximum(m_old, jnp.max(s, axis=-1, keepdims=True) + shift)
        alpha = jnp.exp(m_old - m_new)
        pr = jnp.exp(s - (m_new - shift))
        l_scr[st] = alpha * l_scr[st] + jnp.sum(pr, axis=-1, keepdims=True)
        acc_scr[st] = alpha * acc_scr[st] + _dot_nt(pr.astype(BF16), x_t)
        m_scr[st] = m_new

    @pl.when(p == pl.num_programs(1) - 1)
    def _():
        xn = jnp.concatenate([new_ref[0], jnp.zeros((TQ - nq, KV_ROW), F32)], axis=0).astype(BF16)
        qi_n = jnp.concatenate([lax.broadcasted_iota(jnp.int32, (nq, TQ), 0)] * (nrow // nq), axis=0)
        kj_n = lax.broadcasted_iota(jnp.int32, (nrow, TQ), 1)
        dist_n = qi_n - kj_n
        valid_n = (dist_n >= 0) & (kj_n < nq)
        s_n = jnp.where(valid_n, _dot_nt(qrows, xn) - slope_col * dist_n.astype(F32), NEG)
        m_new = jnp.max(s_n, axis=-1, keepdims=True)
        for st in range(N_STREAMS):
            m_new = jnp.maximum(m_new, m_scr[st])
        p_n = jnp.where(valid_n, jnp.exp(s_n - m_new), 0.0)
        den = jnp.sum(p_n, axis=-1, keepdims=True)
        num = _dot(p_n.astype(BF16), xn)
        for st in range(N_STREAMS):
            alpha = jnp.exp(m_scr[st] - m_new)
            den = den + alpha * l_scr[st]
            num = num + alpha * acc_scr[st]
        o_slc = num / jnp.maximum(den, 1e-30)
        o_cmp = ocmp_scr[...]
        gates = _sigmoid(gd_ref[0])
        for h in range(NSA_HEADS):
            g = h // n_rep
            v0 = (NSA_KV_HEADS + g) * HEAD_DIM
            rr = slice(h * nq, (h + 1) * nq)
            hh = slice(h * HEAD_DIM, (h + 1) * HEAD_DIM)
            o_ref[0, :, hh] = (gates[:, 3 * h:3 * h + 1] * o_cmp[rr, v0:v0 + HEAD_DIM]
                               + gates[:, 3 * h + 1:3 * h + 2] * o_slc[rr, v0:v0 + HEAD_DIM]
                               + gates[:, 3 * h + 2:3 * h + 3] * ow_ref[0, :, hh])


def nsa_decode(page_table, q_pad, gd, kc, new_kv, o_win, pool_t, pos0, pages_per_step=16):
    b, nq = q_pad.shape[:2]
    nb = kc.shape[1]
    n_pages = page_table.shape[1]
    page = pool_t.shape[2]
    pps = math.gcd(pages_per_step, n_pages)
    assert pps % N_STREAMS == 0
    n_steps = n_pages // pps
    nkey = pps * page
    nrow = NSA_HEADS * nq
    expand = (jnp.arange(n_pages * page)[None, :] // CMP_BLOCK == jnp.arange(nb)[:, None]).astype(BF16)

    def page_spec(j):
        return pl.BlockSpec((None, KV_ROW, page), lambda i, p, pt: (pt[i, p * pps + j], 0, 0))

    grid_spec = pltpu.PrefetchScalarGridSpec(
        num_scalar_prefetch=1,
        grid=(b, n_steps),
        in_specs=[pl.BlockSpec((1, nq, NSA_WIDTH), lambda i, p, pt: (i, 0, 0)),
                  pl.BlockSpec((1, nq, LANE), lambda i, p, pt: (i, 0, 0)),
                  pl.BlockSpec((1, nb, KV_ROW), lambda i, p, pt: (i, 0, 0)),
                  pl.BlockSpec((1, nq, KV_ROW), lambda i, p, pt: (i, 0, 0)),
                  pl.BlockSpec((1, nq, NSA_WIDTH), lambda i, p, pt: (i, 0, 0)),
                  pl.BlockSpec(expand.shape, lambda i, p, pt: (0, 0))]
                 + [page_spec(j) for j in range(pps)],
        out_specs=pl.BlockSpec((1, nq, NSA_WIDTH), lambda i, p, pt: (i, 0, 0)),
        scratch_shapes=[pltpu.VMEM((N_STREAMS, nrow, 1), F32), pltpu.VMEM((N_STREAMS, nrow, 1), F32),
                        pltpu.VMEM((N_STREAMS, nrow, KV_ROW), F32),
                        pltpu.VMEM((n_steps, NSA_KV_HEADS * nq, nkey), F32),
                        pltpu.VMEM((nrow, nkey), F32),
                        pltpu.VMEM((nrow, KV_ROW), F32),
                        pltpu.VMEM((nrow, KV_ROW), BF16)])
    return pl.pallas_call(
        functools.partial(_nsa_decode_kernel, pages_per_step=pps, pos0=pos0, n_sel=min(SEL_TOPK, nb + 1)),
        grid_spec=grid_spec,
        out_shape=jax.ShapeDtypeStruct((b, nq, NSA_WIDTH), F32),
        compiler_params=_cparams("parallel", "arbitrary"),
        name="nsa_decode",
    )(page_table, q_pad, gd, kc, new_kv, o_win, expand, *([pool_t] * pps))


def _cumsum_rows(v):
    n = v.shape[0]
    ri = lax.broadcasted_iota(jnp.int32, v.shape, 0)
    sh = 1
    while sh < n:
        v = v + jnp.where(ri >= sh, pltpu.roll(v, sh, 0), 0.0)
        sh *= 2
    return v


def _ssd_kernel(z_ref, xbc_ref, gd_ref, tail_ref, s0_ref, cw_ref, cb_ref, dtb_ref, alog_ref, dsk_ref, ng_ref,
                y_ref, sfin_ref, xp_scr, st_scr, y_scr, *, n_valid):
    c = pl.program_id(1)
    q = xbc_ref.shape[1]

    @pl.when(c == 0)
    def _():
        xp_scr[0:SUBLANE, :] = tail_ref[0]
        st_scr[...] = s0_ref[0]

    xp_scr[SUBLANE:SUBLANE + q, :] = xbc_ref[0]
    acc = cb_ref[...]
    for k in range(SSM_CONV):
        lo = SUBLANE - (SSM_CONV - 1) + k
        acc = acc + xp_scr[lo:lo + q, :] * cw_ref[k:k + 1, :]
    nxt = xp_scr[q:q + SUBLANE, :]
    xp_scr[0:SUBLANE, :] = nxt
    act = acc * _sigmoid(acc)

    raw = gd_ref[0] + dtb_ref[...]
    dt = jnp.maximum(raw, 0.0) + jnp.log1p(jnp.exp(-jnp.abs(raw)))
    if n_valid < q:
        dt = jnp.where(lax.broadcasted_iota(jnp.int32, dt.shape, 0) < n_valid, dt, 0.0)
    acum = _cumsum_rows(dt * (-jnp.exp(alog_ref[...])))
    acum_t = acum.T
    ri = lax.broadcasted_iota(jnp.int32, (q, q), 0)
    ci = lax.broadcasted_iota(jnp.int32, (q, q), 1)
    causal = ri >= ci
    half = lax.broadcasted_iota(jnp.int32, (2 * SSM_HEAD_DIM, 1), 0) < SSM_HEAD_DIM
    hpg = SSM_HEADS // SSM_GROUPS
    for pair in range(SSM_HEADS // 2):
        grp = (2 * pair) // hpg
        lanes = slice(pair * LANE, (pair + 1) * LANE)
        bm = act[:, SSM_D_INNER + grp * SSM_STATE:SSM_D_INNER + (grp + 1) * SSM_STATE].astype(BF16)
        cm_lo = SSM_D_INNER + SSM_GROUPS * SSM_STATE + grp * SSM_STATE
        cm = act[:, cm_lo:cm_lo + SSM_STATE].astype(BF16)
        cb = _dot_nt(cm, bm)
        xs = act[:, lanes]
        cols = []
        for h in (2 * pair, 2 * pair + 1):
            ln = DT_LANE + h
            cols.append((acum[:, ln:ln + 1], acum_t[ln:ln + 1, :], acum[q - 1:q, ln:ln + 1], dt[:, ln:ln + 1]))
        xdt = xs * _pair_lanes(cols[0][3], cols[1][3])
        xdt_bf = xdt.astype(BF16)
        y_parts = []
        for a_col, a_row, _, _ in cols:
            seg = a_col - a_row
            lmat = jnp.where(causal, jnp.exp(jnp.where(causal, seg, 0.0)), 0.0)
            y_parts.append(_dot((cb * lmat).astype(BF16), xdt_bf))
        lane = lax.broadcasted_iota(jnp.int32, (q, LANE), 1)
        y_diag = jnp.where(lane < SSM_HEAD_DIM, y_parts[0], y_parts[1])
        st = st_scr[pair * LANE:(pair + 1) * LANE, :]
        y_off = _dot_nt(cm, st.astype(BF16)) * _pair_lanes(jnp.exp(cols[0][0]), jnp.exp(cols[1][0]))
        dec_end = _pair_lanes(jnp.exp(cols[0][2] - cols[0][0]), jnp.exp(cols[1][2] - cols[1][0]))
        cs = _dot((xdt * dec_end).T.astype(BF16), bm)
        st_scr[pair * LANE:(pair + 1) * LANE, :] = st * jnp.where(half, jnp.exp(cols[0][2]), jnp.exp(cols[1][2])) + cs
        dsk = _pair_lanes(jnp.zeros((q, 1), F32) + dsk_ref[:, 2 * pair:2 * pair + 1],
                          jnp.zeros((q, 1), F32) + dsk_ref[:, 2 * pair + 1:2 * pair + 2])
        y_scr[:, lanes] = y_diag + y_off + dsk * xs

    z = z_ref[0]
    y_ref[0] = _rms(y_scr[...] * (z * _sigmoid(z)), ng_ref[...])

    @pl.when(c == pl.num_programs(1) - 1)
    def _():
        sfin_ref[0] = st_scr[...]


def ssd_mixer(z, xbc, gd, tail8, s0, conv_w, conv_b, dt_bias, a_log, d_skip, norm_g, n_valid):
    b, seq = z.shape[:2]
    nst = SSM_HEADS * SSM_HEAD_DIM

    def lane_piece(v):
        return jnp.zeros((1, LANE), F32).at[0, DT_LANE:DT_LANE + SSM_HEADS].set(v)

    alog_piece = jnp.full((1, LANE), -100.0, F32).at[0, DT_LANE:DT_LANE + SSM_HEADS].set(a_log)
    full2 = lambda shape: pl.BlockSpec(shape, lambda i, c: (0, 0))
    return pl.pallas_call(
        functools.partial(_ssd_kernel, n_valid=n_valid),
        grid=(b, seq // TQ),
        in_specs=[pl.BlockSpec((1, TQ, SSM_D_INNER), lambda i, c: (i, c, 0)),
                  pl.BlockSpec((1, TQ, SSM_CONV_DIM), lambda i, c: (i, c, 0)),
                  pl.BlockSpec((1, TQ, LANE), lambda i, c: (i, c, 0)),
                  pl.BlockSpec((1, SUBLANE, SSM_CONV_DIM), lambda i, c: (i, 0, 0)),
                  pl.BlockSpec((1, nst, SSM_STATE), lambda i, c: (i, 0, 0)),
                  full2((SSM_CONV, SSM_CONV_DIM)), full2((1, SSM_CONV_DIM)), full2((1, LANE)),
                  full2((1, LANE)), full2((1, SSM_HEADS)), full2((1, SSM_D_INNER))],
        out_specs=[pl.BlockSpec((1, TQ, SSM_D_INNER), lambda i, c: (i, c, 0)),
                   pl.BlockSpec((1, nst, SSM_STATE), lambda i, c: (i, 0, 0))],
        out_shape=[jax.ShapeDtypeStruct((b, seq, SSM_D_INNER), F32),
                   jax.ShapeDtypeStruct((b, nst, SSM_STATE), F32)],
        scratch_shapes=[pltpu.VMEM((SUBLANE + TQ, SSM_CONV_DIM), F32),
                        pltpu.VMEM((nst, SSM_STATE), F32),
                        pltpu.VMEM((TQ, SSM_D_INNER), F32)],
        compiler_params=_cparams("parallel", "arbitrary"),
        name="ssd_mixer",
    )(z, xbc, gd, tail8, s0, conv_w, conv_b.reshape(1, -1), lane_piece(dt_bias), alog_piece,
      d_skip.reshape(1, -1), norm_g.reshape(1, -1))


def _channel_major(a6):
    b, rows = a6.shape[:2]
    return jnp.transpose(a6, (0, 2, 3, 4, 1)).reshape(b, KV_ROW, rows)


def _token_major(a_t):
    b, _, rows = a_t.shape
    return jnp.transpose(a_t.reshape(b, 2, NSA_KV_HEADS, HEAD_DIM, rows), (0, 4, 1, 2, 3))


def _even_weights(w_in):
    d = w_in.shape[0]
    o = np.cumsum([0, NSA_WIDTH, KV_ROW, KV_ROW, KV_ROW, 3 * NSA_HEADS, SSM_D_INNER, SSM_CONV_DIM, SSM_HEADS])
    q, kvc, kvs, kvw, gt, z, xbc, dtr = (w_in[:, o[i]:o[i + 1]] for i in range(8))
    gd = jnp.zeros((d, LANE), F32).at[:, :3 * NSA_HEADS].set(gt).at[:, DT_LANE:DT_LANE + SSM_HEADS].set(dtr)
    w = jnp.concatenate([q * SCALE, kvc, kvs, kvw, z, xbc, gd], axis=1).astype(BF16)
    qw = NSA_WIDTH
    off = {"q": (0, qw), "kvc": (qw, KV_ROW), "kvs": (qw + KV_ROW, KV_ROW), "kvw": (qw + 2 * KV_ROW, KV_ROW),
           "kv3": (qw, 3 * KV_ROW), "z": (qw + 3 * KV_ROW, SSM_D_INNER),
           "xbc": (qw + 3 * KV_ROW + SSM_D_INNER, SSM_CONV_DIM),
           "gd": (qw + 3 * KV_ROW + SSM_D_INNER + SSM_CONV_DIM, LANE)}
    return w, off


def _even_layer(x, past, p, cache):
    (norm_g, w_in, w_out, cw, ssm_p) = p
    b, seq, d = x.shape
    m = b * seq
    decode = cache is not None
    w, off = _even_weights(w_in)
    conv_w, conv_b, dt_bias, a_log, d_skip, ssm_norm = ssm_p
    if not decode:
        names = [("row", "q", F32), ("chan", "kvc", F32), ("chan", "kvs", F32), ("chan", "kvw", F32),
                 ("page", "kvc", F32), ("row", "kv3", BF16), ("page", "kv3", BF16),
                 ("row", "z", F32), ("row", "xbc", F32), ("row", "gd", F32)]
        q, kvc_t, kvs_t, kvw_t, kvc_pages, x_rows, x_pages, z, xbc, gd = norm_proj(
            x, norm_g, w, [(k,) + off[n] for k, n, _ in names], [dt for _, _, dt in names])
        q = q.reshape(b, seq, -1)
        gd3 = gd.reshape(b, seq, LANE)
        x_rows = x_rows.reshape(b, seq, 3 * KV_ROW)
        x_pages = x_pages.reshape(b, seq // TQ, 3 * KV_ROW, TQ)
        kc = compress(kvc_pages.reshape(-1, TQ), cw).reshape(b, seq // CMP_BLOCK, KV_ROW)
        o_win = banded_attention(q, x_rows, x_pages, 2, NSA_KV_HEADS, NSA_GROUP, NSA_WINDOW)
        o_nsa = nsa_prompt(q, gd3, kc, x_rows, x_pages, 1, o_win)
        tail8 = jnp.zeros((b, SUBLANE, SSM_CONV_DIM), F32)
        s0 = jnp.zeros((b, SSM_D_INNER, SSM_STATE), F32)
        xbc3 = xbc.reshape(b, seq, -1)
        o_ssm, s_fin = ssd_mixer(z.reshape(b, seq, -1), xbc3, gd3, tail8, s0,
                                 conv_w, conv_b, dt_bias, a_log, d_skip, ssm_norm, TQ)
        conv_new = xbc3[:, seq - (SSM_CONV - 1):]
        kv_outs = (_token_major(kvc_t), _token_major(kvs_t), _token_major(kvw_t[:, :, seq - min(NSA_WINDOW, seq):]))
    else:
        cmp_pool_t, slc_pool_t, page_table, win_state_t, conv0, ssm0 = cache
        n_pool, _, page = cmp_pool_t.shape
        assert page == TQ and seq < CMP_BLOCK and TQ % seq == 0
        names = [("row", "q", F32), ("row", "kvc", F32), ("row", "kvs", F32), ("row", "kvw", F32),
                 ("row", "z", F32), ("row", "xbc", F32), ("row", "gd", F32)]
        q, kvc, kvs, kvw, z, xbc, gd = norm_proj(
            x, norm_g, w, [(k,) + off[n] for k, n, _ in names], [dt for _, _, dt in names])
        q = q.reshape(b, seq, -1)
        gd3 = gd.reshape(b, seq, LANE)
        bpp = page // CMP_BLOCK
        kc_pool = compress(cmp_pool_t.reshape(n_pool * KV_ROW, page), cw)
        kc = kc_pool.reshape(n_pool, bpp, KV_ROW)[page_table].reshape(b, past // CMP_BLOCK, KV_ROW)
        kvw3 = kvw.reshape(b, seq, KV_ROW)
        o_win = banded_decode(q, win_state_t, kvw3, NSA_KV_HEADS, NSA_GROUP, NSA_WINDOW, past)
        o_nsa = nsa_decode(page_table, q, gd3, kc, kvs.reshape(b, seq, KV_ROW), o_win, slc_pool_t, past)
        padr = lambda a: jnp.pad(a.reshape(b, seq, -1), ((0, 0), (0, TQ - seq), (0, 0)))
        tail8 = jnp.pad(conv0, ((0, 0), (SUBLANE - (SSM_CONV - 1), 0), (0, 0)))
        o_ssm, s_fin = ssd_mixer(padr(z), padr(xbc), padr(gd), tail8, ssm0.reshape(b, SSM_D_INNER, SSM_STATE),
                                 conv_w, conv_b, dt_bias, a_log, d_skip, ssm_norm, seq)
        o_ssm = o_ssm[:, :seq]
        conv_new = jnp.concatenate([conv0, xbc.reshape(b, seq, -1)], axis=1)[:, -(SSM_CONV - 1):]
        npast = win_state_t.shape[2]
        win_out = jnp.concatenate([win_state_t, jnp.swapaxes(kvw3, 1, 2)], axis=2)[:, :, -npast:]
        kv6 = (b, seq, 2, NSA_KV_HEADS, HEAD_DIM)
        kv_outs = (kvc.reshape(kv6), kvs.reshape(kv6), _token_major(win_out))
    x = proj_res([o_nsa.reshape(m, -1), o_ssm.reshape(m, -1)],
                 [w_out[:NSA_WIDTH].astype(BF16), w_out[NSA_WIDTH:].astype(BF16)], x.reshape(m, d))
    outs = kv_outs + (s_fin.reshape(b, SSM_HEADS, SSM_HEAD_DIM, SSM_STATE), conv_new)
    return x.reshape(b, seq, d), outs


def _odd_layer(x, past, p, swa_state_t):
    norm_g, w_in, w_out, sinks = p
    b, seq, d = x.shape
    m = b * seq
    decode = swa_state_t is not None
    w = jnp.concatenate([w_in[:, :SWA_WIDTH] * SCALE, w_in[:, SWA_WIDTH:]], axis=1).astype(BF16)
    qw = SWA_WIDTH
    if not decode:
        q, kv_t, x_rows, x_pages = norm_proj(
            x, norm_g, w, [("row", 0, qw), ("chan", qw, KV_ROW), ("row", qw, KV_ROW), ("page", qw, KV_ROW)],
            [F32, F32, BF16, BF16])
        o = banded_attention(q.reshape(b, seq, -1), x_rows.reshape(b, seq, KV_ROW),
                             x_pages.reshape(b, seq // TQ, KV_ROW, TQ), 0, SWA_KV_HEADS, SWA_GROUP, SWA_WINDOW, sinks)
        kv_out = _token_major(kv_t[:, :, seq - min(SWA_WINDOW, seq):])
    else:
        q, kv = norm_proj(x, norm_g, w, [("row", 0, qw), ("row", qw, KV_ROW)], [F32, F32])
        kv3 = kv.reshape(b, seq, KV_ROW)
        o = banded_decode(q.reshape(b, seq, -1), swa_state_t, kv3, SWA_KV_HEADS, SWA_GROUP, SWA_WINDOW, past, sinks)
        npast = swa_state_t.shape[2]
        kv_out = _token_major(jnp.concatenate([swa_state_t, jnp.swapaxes(kv3, 1, 2)], axis=2)[:, :, -npast:])
    x = proj_res([o.reshape(m, -1)], [w_out.astype(BF16)], x.reshape(m, d))
    return x.reshape(b, seq, d), kv_out


def kernel(x_prompt, x_sample, cache_nsa_cmp_kv, cache_nsa_slc_kv, state_nsa_win_kv, state_ssm, state_ssm_conv, state_swa_kv, page_table, norm_mix, norm_ffn, norm_final, w_in_even, w_out_even, cmp_pe_k, cmp_w1_k, cmp_w2_k, cmp_pe_v, cmp_w1_v, cmp_w2_v, ssm_conv_w, ssm_conv_b, ssm_dt_bias, ssm_a_log, ssm_d, ssm_norm, w_in_odd, w_out_odd, swa_sinks, w_gate_up, w_down):
    depth = norm_mix.shape[0]
    page = cache_nsa_cmp_kv.shape[2]
    past = page_table.shape[1] * page
    xp, xs = x_prompt, x_sample
    outs_p = [[] for _ in range(6)]
    outs_s = [[] for _ in range(6)]
    for layer in range(depth):
        if layer % 2 == 0:
            e = layer // 2
            cw = _compress_weights(cmp_pe_k[e], cmp_w1_k[e], cmp_w2_k[e], cmp_pe_v[e], cmp_w1_v[e], cmp_w2_v[e])
            ssm_p = (ssm_conv_w[e], ssm_conv_b[e], ssm_dt_bias[e], ssm_a_log[e], ssm_d[e], ssm_norm[e])
            p = (norm_mix[layer], w_in_even[e], w_out_even[e], cw, ssm_p)
            xp, o = _even_layer(xp, 0, p, None)
            for lst, v in zip(outs_p[:5], o):
                lst.append(v)
            cache = (_channel_major(cache_nsa_cmp_kv[e]), _channel_major(cache_nsa_slc_kv[e]), page_table,
                     _channel_major(state_nsa_win_kv[e]), state_ssm_conv[e], state_ssm[e])
            xs, o = _even_layer(xs, past, p, cache)
            for lst, v in zip(outs_s[:5], o):
                lst.append(v)
        else:
            o_idx = layer // 2
            p = (norm_mix[layer], w_in_odd[o_idx], w_out_odd[o_idx], swa_sinks[o_idx])
            xp, kw = _odd_layer(xp, 0, p, None)
            outs_p[5].append(kw)
            xs, kw = _odd_layer(xs, past, p, _channel_major(state_swa_kv[o_idx]))
            outs_s[5].append(kw)
        gf = norm_final if layer == depth - 1 else None
        wgu, wd = w_gate_up[layer].astype(BF16), w_down[layer].astype(BF16)
        xp = ffn(xp.reshape(-1, xp.shape[-1]), norm_ffn[layer], wgu, wd, gf).reshape(xp.shape)
        xs = ffn(xs.reshape(-1, xs.shape[-1]), norm_ffn[layer], wgu, wd, gf).reshape(xs.shape)
    return (xp, xs) + tuple(jnp.stack(v) for v in outs_p) + tuple(jnp.stack(v) for v in outs_s)
```

```python
import functools
import math

import numpy as np
import jax
import jax.numpy as jnp
from jax import lax
from jax.experimental import pallas as pl
from jax.experimental.pallas import tpu as pltpu

F32 = jnp.float32
BF16 = jnp.bfloat16

HEAD_DIM = 64
NSA_HEADS = 8
NSA_KV_HEADS = 2
NSA_GROUP = NSA_HEADS // NSA_KV_HEADS
CMP_BLOCK = 64
SEL_TOPK = 16
NSA_WINDOW = 512
SSM_HEADS = 8
SSM_HEAD_DIM = 64
SSM_D_INNER = SSM_HEADS * SSM_HEAD_DIM
SSM_GROUPS = 2
SSM_STATE = 128
SSM_CONV = 4
SSM_CONV_DIM = SSM_D_INNER + 2 * SSM_GROUPS * SSM_STATE
SWA_HEADS = 16
SWA_KV_HEADS = 2
SWA_GROUP = SWA_HEADS // SWA_KV_HEADS
SWA_WINDOW = 128
RMS_EPS = 1e-6
NSA_WIDTH = NSA_HEADS * HEAD_DIM
SWA_WIDTH = SWA_HEADS * HEAD_DIM
SCALE = HEAD_DIM ** -0.5

LANE = 128
SUBLANE = 8
KV_ROW = 2 * NSA_KV_HEADS * HEAD_DIM
TQ = 128
NEG = -1e30
MASK = -2e30
VMEM_LIMIT = 56 * 1024 * 1024
DT_LANE = 3 * NSA_HEADS
N_STREAMS = 1
NSA_CHUNK_TILES = 4

def _cparams(*sem):
    return pltpu.CompilerParams(dimension_semantics=sem, vmem_limit_bytes=VMEM_LIMIT)


def _dot(a, b):
    return jnp.dot(a, b, preferred_element_type=F32)


def _dot_nt(a, b):
    return lax.dot_general(a, b, (((1,), (1,)), ((), ())), preferred_element_type=F32)


def _rms(x, g):
    return x * lax.rsqrt(jnp.mean(x * x, axis=-1, keepdims=True) + RMS_EPS) * g


def _sigmoid(x):
    return 1.0 / (1.0 + jnp.exp(-x))


def _alibi(n_heads):
    return [float(2.0 ** (-8.0 * i / n_heads)) for i in range(1, n_heads + 1)]


def _col(vals, rows):
    return jnp.concatenate([jnp.full((rows, 1), v, F32) for v in vals], axis=0)


def _row(vals, cols):
    return jnp.concatenate([jnp.full((1, cols), v, F32) for v in vals], axis=1)


def _pair_lanes(c0, c1):
    lane = lax.broadcasted_iota(jnp.int32, (c0.shape[0], LANE), 1)
    return jnp.where(lane < HEAD_DIM, c0, c1)


def _norm_proj_kernel(x_ref, g_ref, w_ref, *o_refs, pieces):
    hb = _rms(x_ref[...], g_ref[...]).astype(BF16)
    cache = {}
    spans = sorted({(off, n) for _, off, n in pieces}, key=lambda s: -s[1])
    for o_ref, (kind, off, n) in zip(o_refs, pieces):
        p_off, p_n = next((o2, n2) for o2, n2 in spans if o2 <= off and off + n <= o2 + n2)
        if (p_off, p_n) not in cache:
            cache[(p_off, p_n)] = _dot(hb, w_ref[:, p_off:p_off + p_n])
        if (off, n) not in cache:
            cache[(off, n)] = cache[(p_off, p_n)][:, off - p_off:off - p_off + n]
        y = cache[(off, n)]
        if kind == "row":
            o_ref[...] = y.astype(o_ref.dtype)
        else:
            if ("t", off, n) not in cache:
                cache[("t", off, n)] = y.T
            y_t = cache[("t", off, n)]
            if kind == "chan":
                o_ref[0] = y_t.astype(o_ref.dtype)
            else:
                for j in range(o_ref.shape[0]):
                    o_ref[j] = y_t[:, j * TQ:(j + 1) * TQ].astype(o_ref.dtype)


def norm_proj(x3d, g, w_bf, pieces, dtypes, tm=256):
    b, seq, d = x3d.shape
    m = b * seq
    n_tot = w_bf.shape[1]
    tm = min(tm, m)
    per_b = seq // tm if seq >= tm else 1
    out_specs, out_shape = [], []
    for (kind, _, n), dt in zip(pieces, dtypes):
        if kind == "row":
            out_specs.append(pl.BlockSpec((tm, n), lambda i: (i, 0)))
            out_shape.append(jax.ShapeDtypeStruct((m, n), dt))
        elif kind == "chan":
            out_specs.append(pl.BlockSpec((1, n, tm), lambda i: (i // per_b, 0, i % per_b)))
            out_shape.append(jax.ShapeDtypeStruct((b, n, seq), dt))
        else:
            out_specs.append(pl.BlockSpec((tm // TQ, n, TQ), lambda i: (i, 0, 0)))
            out_shape.append(jax.ShapeDtypeStruct((m // TQ, n, TQ), dt))
    return pl.pallas_call(
        functools.partial(_norm_proj_kernel, pieces=tuple(pieces)),
        grid=(m // tm,),
        in_specs=[pl.BlockSpec((tm, d), lambda i: (i, 0)),
                  pl.BlockSpec((1, d), lambda i: (0, 0)),
                  pl.BlockSpec((d, n_tot), lambda i: (0, 0))],
        out_specs=out_specs,
        out_shape=out_shape,
        compiler_params=_cparams("parallel"),
        name="norm_proj",
    )(x3d.reshape(m, d), g.reshape(1, d), w_bf)


def _proj_res_kernel(*refs, n_in):
    a_refs, w_refs = refs[:n_in], refs[n_in:2 * n_in]
    res_ref, o_ref = refs[2 * n_in], refs[2 * n_in + 1]
    acc = res_ref[...]
    for a_ref, w_ref in zip(a_refs, w_refs):
        acc = acc + _dot(a_ref[...].astype(BF16), w_ref[...])
    o_ref[...] = acc


def proj_res(a_list, w_list, res, tm=512):
    m, d = res.shape
    tm = min(tm, m)
    n_in = len(a_list)
    in_specs = ([pl.BlockSpec((tm, a.shape[1]), lambda i: (i, 0)) for a in a_list]
                + [pl.BlockSpec(w.shape, lambda i: (0, 0)) for w in w_list]
                + [pl.BlockSpec((tm, d), lambda i: (i, 0))])
    return pl.pallas_call(
        functools.partial(_proj_res_kernel, n_in=n_in),
        grid=(m // tm,),
        in_specs=in_specs,
        out_specs=pl.BlockSpec((tm, d), lambda i: (i, 0)),
        out_shape=jax.ShapeDtypeStruct((m, d), F32),
        compiler_params=_cparams("parallel"),
        name="proj_res",
    )(*a_list, *w_list, res)


def _ffn_kernel(x_ref, g_ref, wg_ref, wu_ref, wd_ref, *rest, final):
    if final:
        gf_ref, o_ref, h_scr, acc_scr = rest
    else:
        o_ref, h_scr, acc_scr = rest
    j = pl.program_id(1)

    @pl.when(j == 0)
    def _():
        x = x_ref[...]
        h_scr[...] = _rms(x, g_ref[...]).astype(BF16)
        acc_scr[...] = x

    hb = h_scr[...]
    gate = _dot(hb, wg_ref[...])
    up = _dot(hb, wu_ref[...])
    act = gate * _sigmoid(gate) * up
    acc_scr[...] += _dot(act.astype(BF16), wd_ref[...])

    @pl.when(j == pl.num_programs(1) - 1)
    def _():
        y = acc_scr[...]
        if final:
            y = _rms(y, gf_ref[...])
        o_ref[...] = y


def ffn(x2d, g, w_gu_bf, w_down_bf, g_final=None, tm=512):
    m, d = x2d.shape
    f = w_down_bf.shape[0]
    tm = min(tm, m)
    nf = 2 if (f // 2) % LANE == 0 else 1
    tf = f // nf
    final = g_final is not None
    in_specs = [pl.BlockSpec((tm, d), lambda i, j: (i, 0)),
                pl.BlockSpec((1, d), lambda i, j: (0, 0)),
                pl.BlockSpec((d, tf), lambda i, j: (0, j)),
                pl.BlockSpec((d, tf), lambda i, j: (0, j + nf)),
                pl.BlockSpec((tf, d), lambda i, j: (j, 0))]
    args = [x2d, g.reshape(1, d), w_gu_bf, w_gu_bf, w_down_bf]
    if final:
        in_specs.append(pl.BlockSpec((1, d), lambda i, j: (0, 0)))
        args.append(g_final.reshape(1, d))
    return pl.pallas_call(
        functools.partial(_ffn_kernel, final=final),
        grid=(m // tm, nf),
        in_specs=in_specs,
        out_specs=pl.BlockSpec((tm, d), lambda i, j: (i, 0)),
        out_shape=jax.ShapeDtypeStruct((m, d), F32),
        scratch_shapes=[pltpu.VMEM((tm, d), BF16), pltpu.VMEM((tm, d), F32)],
        compiler_params=_cparams("parallel", "arbitrary"),
        name="ffn",
    )(*args)


def _gelu_tanh(x):
    c = math.sqrt(2.0 / math.pi)
    return x * (0.5 * (1.0 + jnp.tanh(c * (x + 0.044715 * (x * x * x)))))


def _transpose8(v):
    row = lax.broadcasted_iota(jnp.int32, (SUBLANE, LANE), 0)
    for s in (4, 2, 1):
        low = (row & s) == 0
        out = list(v)
        for i in range(SUBLANE):
            if i & s == 0:
                out[i] = jnp.where(low, v[i], pltpu.roll(v[i + s], s, 0))
                out[i + s] = jnp.where(low, pltpu.roll(v[i], SUBLANE - s, 0), v[i + s])
        v = out
    return v


def _compress_kernel(x_ref, pe_ref, w1k_ref, w1v_ref, w2_ref, o_ref, *, n_pages):
    hidden = []
    for kv, w1_ref in enumerate((w1k_ref, w1v_ref)):
        slabs = []
        for c in range(kv * NSA_KV_HEADS, (kv + 1) * NSA_KV_HEADS):
            per_d = [[] for _ in range(HEAD_DIM)]
            for p0 in range(0, n_pages, SUBLANE):
                for d0 in range(0, HEAD_DIM, SUBLANE):
                    tiles = [x_ref[pl.ds((p0 + p) * KV_ROW + c * HEAD_DIM + d0, SUBLANE), :] for p in range(SUBLANE)]
                    for r, u in enumerate(_transpose8(tiles)):
                        per_d[d0 + r].append(u)
            rows = [jnp.concatenate(per_d[d], axis=0) for d in range(HEAD_DIM)]
            slabs.append(jnp.concatenate(rows, axis=1).astype(BF16))
        h = _gelu_tanh(_dot(jnp.concatenate(slabs, axis=0), w1_ref[...]) + pe_ref[kv:kv + 1, :]).astype(BF16)
        hidden += [h[i * n_pages:(i + 1) * n_pages] for i in range(NSA_KV_HEADS)]
    o_ref[...] = _dot(jnp.concatenate(hidden, axis=1), w2_ref[...]).astype(o_ref.dtype)


def compress(pages2d, cw, pages_per_step=64):
    pe4, w1k, w1v, w2 = cw
    n_pages = pages2d.shape[0] // KV_ROW
    pps = math.gcd(pages_per_step, n_pages)
    blocks_per_page = TQ // CMP_BLOCK
    const = lambda a: pl.BlockSpec(a.shape, lambda i: (0, 0))
    return pl.pallas_call(
        functools.partial(_compress_kernel, n_pages=pps),
        grid=(n_pages // pps,),
        in_specs=[pl.BlockSpec((pps * KV_ROW, TQ), lambda i: (i, 0)), const(pe4), const(w1k), const(w1v), const(w2)],
        out_specs=pl.BlockSpec((pps, blocks_per_page * KV_ROW), lambda i: (i, 0)),
        out_shape=jax.ShapeDtypeStruct((n_pages, blocks_per_page * KV_ROW), BF16),
        compiler_params=_cparams("parallel"),
        name="compress",
    )(pages2d, pe4, w1k, w1v, w2)


def _compress_weights(pe_k, w1_k, w2_k, pe_v, w1_v, w2_v):
    nj = TQ // CMP_BLOCK
    nc = 2 * NSA_KV_HEADS
    l, d, h = w1_k.shape

    def first(w1):
        w = jnp.zeros((d, nj, l, nj, h), F32)
        for j in range(nj):
            w = w.at[:, j, :, j, :].set(jnp.transpose(w1, (1, 0, 2)))
        return w.reshape(d * nj * l, nj * h).astype(BF16)

    def pe_row(pe):
        return jnp.broadcast_to(pe.T[:, None, :], (d, nj, l)).reshape(1, d * nj * l)

    w2 = jnp.zeros((nc, nj, h, nj, nc, w2_k.shape[1]), F32)
    for c in range(nc):
        for j in range(nj):
            w2 = w2.at[c, j, :, j, c, :].set(w2_k if c < NSA_KV_HEADS else w2_v)
    w1k, w1v = first(w1_k), first(w1_v)
    pe_hidden = jnp.concatenate([jnp.dot(pe_row(pe), w1.astype(F32), precision=lax.Precision.HIGHEST)
                                 for pe, w1 in ((pe_k, w1k), (pe_v, w1v))], axis=0)
    return pe_hidden, w1k, w1v, w2.reshape(nc * nj * h, nj * nc * w2_k.shape[1]).astype(BF16)


def _q_rows(q_ref, heads, n_rep, bi=0):
    assert NSA_KV_HEADS * HEAD_DIM == LANE
    nq = q_ref.shape[1]
    lane_half = lax.broadcasted_iota(jnp.int32, (nq, LANE), 1) // HEAD_DIM
    rows = []
    for h in heads:
        g = h // n_rep
        tile = q_ref[bi, :, (h // 2) * LANE:(h // 2 + 1) * LANE]
        if h % 2 != g:
            tile = pltpu.roll(tile, HEAD_DIM, 1)
        rows.append(jnp.where(lane_half == g, tile, 0.0))
    q_keys = jnp.concatenate(rows, axis=0)
    return jnp.concatenate([q_keys, jnp.zeros_like(q_keys)], axis=1).astype(BF16)


def _heads_to_rows(o_t, n_rep, pair):
    r0 = 2 * pair
    two = jnp.concatenate([o_t[:, r0 * TQ:(r0 + 1) * TQ], o_t[:, (r0 + 1) * TQ:(r0 + 2) * TQ]], axis=0)
    return two.T


def _banded_kernel(*refs, n_groups, n_rep, window, slopes, has_sink, kv_block):
    if has_sink:
        sink_ref, q_ref, x_ref, xt_ref, o_ref = refs
    else:
        q_ref, x_ref, xt_ref, o_ref = refs
    qt = pl.program_id(1)
    n_heads = n_groups * n_rep
    nrow = n_heads * TQ
    n_tiles = window // TQ + 1
    ki = lax.broadcasted_iota(jnp.int32, (TQ, nrow), 0)
    qi = lax.broadcasted_iota(jnp.int32, (TQ, nrow), 1) % TQ
    diff = qi - ki
    diff_f = diff.astype(F32)
    tiles = []
    for j in range(n_tiles):
        kt = qt - (n_tiles - 1 - j)
        ktc = jnp.maximum(kt, 0)
        x = x_ref[0, pl.ds(pl.multiple_of(ktc * TQ, TQ), TQ), :]
        tiles.append((x, ktc, jnp.where(kt >= 0, 0.0, MASK)))
    heads = list(range(n_heads))
    qrows = jnp.concatenate([_q_rows(q_ref, heads[g * n_rep:(g + 1) * n_rep], n_rep) for g in range(n_groups)], axis=0)
    slope_row = _row([slopes[h] for h in heads], TQ)
    bias0 = slope_row * diff_f
    s_list = []
    for j, (x, _, before_start) in enumerate(tiles):
        s = _dot_nt(x, qrows) - bias0 + (before_start - slope_row * float((n_tiles - 1 - j) * TQ))
        if j == 0:
            s = jnp.where(diff <= 0, s, MASK)
        if j == n_tiles - 1:
            s = jnp.where(diff >= 0, s, MASK)
        s_list.append(s)
    m = s_list[0].max(axis=0, keepdims=True)
    for s in s_list[1:]:
        m = jnp.maximum(m, s.max(axis=0, keepdims=True))
    if has_sink:
        sink_row = jnp.concatenate([jnp.full((1, TQ), sink_ref[h], F32) for h in heads], axis=1)
        m = jnp.maximum(m, sink_row)
    p_list = [jnp.exp(s - m) for s in s_list]
    den = p_list[0].sum(axis=0, keepdims=True)
    for p in p_list[1:]:
        den = den + p.sum(axis=0, keepdims=True)
    if has_sink:
        den = den + jnp.exp(sink_row - m)
    p_all = jnp.concatenate([p.astype(BF16) for p in p_list], axis=0)
    for g in range(n_groups):
        cols = slice(g * n_rep * TQ, (g + 1) * n_rep * TQ)
        v0 = kv_block * KV_ROW + (NSA_KV_HEADS + g) * HEAD_DIM
        v_all = jnp.concatenate([xt_ref[0, ktc, v0:v0 + HEAD_DIM, :] for _, ktc, _ in tiles], axis=1)
        o_t = _dot(v_all, p_all[:, cols]) / jnp.maximum(den[:, cols], 1e-30)
        for pair in range(n_rep // 2):
            h0 = g * n_rep + 2 * pair
            o_ref[0, :, h0 * HEAD_DIM:(h0 + 2) * HEAD_DIM] = _heads_to_rows(o_t, n_rep, pair)


def banded_attention(q_pad, x_rows, x_pages, kv_block, n_groups, n_rep, window, sinks=None):
    b, seq = q_pad.shape[:2]
    n_heads = n_groups * n_rep
    has_sink = sinks is not None
    in_specs = [pl.BlockSpec((1, TQ, n_heads * HEAD_DIM), lambda i, j: (i, j, 0)),
                pl.BlockSpec((1, seq, KV_ROW), lambda i, j: (i, 0, kv_block)),
                pl.BlockSpec((1,) + x_pages.shape[1:], lambda i, j: (i, 0, 0, 0))]
    args = [q_pad, x_rows, x_pages]
    if has_sink:
        in_specs.insert(0, pl.BlockSpec(memory_space=pltpu.SMEM))
        args.insert(0, sinks)
    assert window % TQ == 0 and window >= TQ
    return pl.pallas_call(
        functools.partial(_banded_kernel, n_groups=n_groups, n_rep=n_rep, window=window,
                          slopes=_alibi(n_heads), has_sink=has_sink, kv_block=kv_block),
        grid=(b, seq // TQ),
        in_specs=in_specs,
        out_specs=pl.BlockSpec((1, TQ, n_heads * HEAD_DIM), lambda i, j: (i, j, 0)),
        out_shape=jax.ShapeDtypeStruct((b, seq, n_heads * HEAD_DIM), F32),
        compiler_params=_cparams("parallel", "parallel"),
        name="banded_attention",
    )(*args)


def _banded_decode_kernel(*refs, n_groups, n_rep, window, slopes, has_sink, pos0):
    if has_sink:
        sink_ref, q_ref, st_ref, new_ref, o_ref = refs
    else:
        q_ref, st_ref, new_ref, o_ref = refs
    nq = q_ref.shape[1]
    npast = st_ref.shape[2]
    nrow = n_rep * nq
    qi_s = jnp.concatenate([lax.broadcasted_iota(jnp.int32, (nq, npast), 0)] * n_rep, axis=0)
    kj_s = lax.broadcasted_iota(jnp.int32, (nrow, npast), 1)
    dist_s = npast + qi_s - kj_s
    valid_s = (dist_s >= 0) & (dist_s <= window) & (pos0 - npast + kj_s >= 0)
    qi_n = jnp.concatenate([lax.broadcasted_iota(jnp.int32, (nq, TQ), 0)] * n_rep, axis=0)
    kj_n = lax.broadcasted_iota(jnp.int32, (nrow, TQ), 1)
    dist_n = qi_n - kj_n
    valid_n = (dist_n >= 0) & (dist_n <= window) & (kj_n < nq)
    for bi, g in [(bi, g) for bi in range(q_ref.shape[0]) for g in range(n_groups)]:
        xs_t = st_ref[bi].astype(BF16)
        xn = jnp.concatenate([new_ref[bi], jnp.zeros((TQ - nq, KV_ROW), F32)], axis=0).astype(BF16)
        heads = [g * n_rep + r for r in range(n_rep)]
        qrows = _q_rows(q_ref, heads, n_rep, bi)
        slope_col = _col([slopes[h] for h in heads], nq)
        s_s = jnp.where(valid_s, _dot(qrows, xs_t) - slope_col * dist_s.astype(F32), NEG)
        s_n = jnp.where(valid_n, _dot_nt(qrows, xn) - slope_col * dist_n.astype(F32), NEG)
        m = jnp.maximum(jnp.max(s_s, axis=-1, keepdims=True), jnp.max(s_n, axis=-1, keepdims=True))
        if has_sink:
            sink_col = jnp.concatenate([jnp.full((nq, 1), sink_ref[h], F32) for h in heads], axis=0)
            m = jnp.maximum(m, sink_col)
        p_s = jnp.where(valid_s, jnp.exp(s_s - m), 0.0)
        p_n = jnp.where(valid_n, jnp.exp(s_n - m), 0.0)
        den = jnp.sum(p_s, axis=-1, keepdims=True) + jnp.sum(p_n, axis=-1, keepdims=True)
        if has_sink:
            den = den + jnp.exp(sink_col - m)
        o = (_dot_nt(p_s.astype(BF16), xs_t) + _dot(p_n.astype(BF16), xn)) / jnp.maximum(den, 1e-30)
        v0 = (NSA_KV_HEADS + g) * HEAD_DIM
        for r, h in enumerate(heads):
            o_ref[bi, :, h * HEAD_DIM:(h + 1) * HEAD_DIM] = o[r * nq:(r + 1) * nq, v0:v0 + HEAD_DIM]


def banded_decode(q_pad, state_t, new_kv, n_groups, n_rep, window, pos0, sinks=None, rows_per_step=4):
    b, nq = q_pad.shape[:2]
    npast = state_t.shape[2]
    n_heads = n_groups * n_rep
    has_sink = sinks is not None
    bb = math.gcd(rows_per_step, b)
    in_specs = [pl.BlockSpec((bb, nq, n_heads * HEAD_DIM), lambda i: (i, 0, 0)),
                pl.BlockSpec((bb, KV_ROW, npast), lambda i: (i, 0, 0)),
                pl.BlockSpec((bb, nq, KV_ROW), lambda i: (i, 0, 0))]
    args = [q_pad, state_t, new_kv]
    if has_sink:
        in_specs.insert(0, pl.BlockSpec(memory_space=pltpu.SMEM))
        args.insert(0, sinks)
    return pl.pallas_call(
        functools.partial(_banded_decode_kernel, n_groups=n_groups, n_rep=n_rep, window=window,
                          slopes=_alibi(n_heads), has_sink=has_sink, pos0=pos0),
        grid=(b // bb,),
        in_specs=in_specs,
        out_specs=pl.BlockSpec((bb, nq, n_heads * HEAD_DIM), lambda i: (i, 0, 0)),
        out_shape=jax.ShapeDtypeStruct((b, nq, n_heads * HEAD_DIM), F32),
        compiler_params=_cparams("parallel"),
        name="banded_decode",
    )(*args)


def _select_blocks_t(imp, cur, n_sel):
    nb = imp.shape[0]
    nrow = lax.broadcasted_iota(jnp.int32, imp.shape, 0)
    count = jnp.zeros(imp.shape, F32)
    for i in range(nb):
        ci = imp[i:i + 1, :]
        ahead = jnp.where(ci > imp, 1.0, jnp.where(ci == imp, jnp.where(nrow > i, 1.0, 0.0), 0.0))
        count = count + jnp.where(cur > i, ahead, 0.0)
    return jnp.where(nrow < cur, jnp.where(count < n_sel - 1, 1.0, 0.0), jnp.where(nrow == cur, 1.0, 0.0))


def _nsa_prompt_kernel(q_ref, gd_ref, kc_ref, kct_ref, x_ref, xt_ref, ow_ref, o_ref, m_scr, acc_scr,
                       *, n_sel, kv_block):
    qt = pl.program_id(1)
    n_rep = NSA_GROUP
    nrow = n_rep * TQ
    slopes = _alibi(NSA_HEADS)
    nb = kc_ref.shape[1]
    kcb = kc_ref[0]
    gates = _sigmoid(gd_ref[0])
    n_ct = NSA_CHUNK_TILES
    tk = n_ct * TQ
    n_seq_tiles = x_ref.shape[1] // TQ
    ki = lax.broadcasted_iota(jnp.int32, (tk, NSA_KV_HEADS * nrow), 0)
    qi = lax.broadcasted_iota(jnp.int32, (tk, NSA_KV_HEADS * nrow), 1) % TQ
    diff = qi - ki
    tpos = qt * TQ + lax.broadcasted_iota(jnp.int32, (nb, nrow), 1) % TQ
    blk = lax.broadcasted_iota(jnp.int32, (nb, nrow), 0)
    dist_c = tpos - ((blk + 1) * CMP_BLOCK - 1)
    valid_c = dist_c >= 0
    cur = (qt * TQ + lax.broadcasted_iota(jnp.int32, (nb, TQ), 1)) // CMP_BLOCK
    assert nb + 1 < LANE and all(math.frexp(sl)[0] == 0.5 for sl in slopes)
    lane_x = lax.broadcasted_iota(jnp.int32, (tk, LANE), 1)
    key_in_chunk = lax.broadcasted_iota(jnp.int32, (tk, LANE), 0)
    key_blk = key_in_chunk // CMP_BLOCK
    key_lo = (key_in_chunk % TQ).astype(F32)
    key_hi = (key_in_chunk // TQ).astype(F32)
    lane_q = lax.broadcasted_iota(jnp.int32, (TQ, LANE), 1)
    ones_rows = jnp.ones((2 * SUBLANE, tk), BF16)
    groups = []
    for g in range(NSA_KV_HEADS):
        heads = [g * n_rep + r for r in range(n_rep)]
        qrows = _q_rows(q_ref, heads, n_rep)
        slope_row = _row([slopes[h] for h in heads], TQ)
        vrow = (NSA_KV_HEADS + g) * HEAD_DIM
        s_c = jnp.where(valid_c, _dot_nt(kcb, qrows) - slope_row * dist_c.astype(F32), NEG)
        e_c = jnp.where(valid_c, jnp.exp(s_c - jnp.max(s_c, axis=0, keepdims=True)), 0.0)
        p_c = e_c / jnp.maximum(jnp.sum(e_c, axis=0, keepdims=True), 1e-30)
        ocmp_t = _dot(kct_ref[0, vrow:vrow + HEAD_DIM, :], p_c.astype(BF16))
        imp = p_c[:, 0:TQ]
        for r in range(1, n_rep):
            imp = imp + p_c[:, r * TQ:(r + 1) * TQ]
        taken = _select_blocks_t(imp, cur, n_sel)
        taken_q = jnp.concatenate([taken, jnp.zeros((LANE - nb, TQ), F32)], axis=0).T
        block_mask = jnp.where(taken_q > 0.5, 0.0, MASK)
        q_extra = jnp.concatenate([jnp.where(lane_q == nb, slopes[h], jnp.where(lane_q == nb + 1, slopes[h] * TQ,
                                                                                 block_mask)) for h in heads], axis=0)
        q_aug = jnp.concatenate([qrows, q_extra.astype(BF16)], axis=1)
        acc_scr[g] = jnp.zeros(acc_scr.shape[1:], F32)
        groups.append((heads, q_aug, slope_row, kv_block * KV_ROW + vrow, ocmp_t))
    q_all = jnp.concatenate([grp[1] for grp in groups], axis=0)
    slope_all = jnp.concatenate([grp[2] for grp in groups], axis=1)
    m_scr[...] = jnp.full(m_scr.shape, NEG, F32)

    def chunk(kt0, causal):
        kts = [jnp.minimum(kt0 + i, n_seq_tiles - 1) for i in range(n_ct)]
        x_extra = jnp.where(lane_x == nb, key_lo, jnp.where(lane_x == nb + 1, key_hi,
                            jnp.where(lane_x == key_blk + kt0 * (TQ // CMP_BLOCK), 1.0, 0.0))).astype(BF16)
        x = jnp.concatenate([x_ref[0, pl.ds(pl.multiple_of(kt * TQ, TQ), TQ), :] for kt in kts], axis=0)
        x_aug = jnp.concatenate([x, x_extra], axis=1)
        off = (qt - kt0) * TQ
        s = _dot_nt(x_aug, q_all)
        if causal:
            s = jnp.where(diff + off >= 0, s, MASK)
        shift = slope_all * off.astype(F32)
        m_old = m_scr[...]
        m_new = jnp.maximum(m_old, jnp.max(s, axis=0, keepdims=True) - shift)
        p = jnp.exp(s - (m_new + shift)).astype(BF16)
        alpha = jnp.exp(m_old - m_new)
        m_scr[...] = m_new
        for g, (_, _, _, v0, _) in enumerate(groups):
            cols = slice(g * nrow, (g + 1) * nrow)
            v_t = jnp.concatenate([xt_ref[0, kt, v0:v0 + HEAD_DIM, :] for kt in kts], axis=1)
            acc_scr[g] = alpha[:, cols] * acc_scr[g] + _dot(jnp.concatenate([v_t, ones_rows], axis=0), p[:, cols])

    n_full = qt // n_ct

    def body(j, carry):
        chunk(n_ct * j, False)
        return carry

    lax.fori_loop(0, n_full, body, 0)
    chunk(n_ct * n_full, True)
    for g, (heads, _, _, _, ocmp_t) in enumerate(groups):
        oslc_t = acc_scr[g, 0:HEAD_DIM, :] / jnp.maximum(acc_scr[g, HEAD_DIM:HEAD_DIM + 1, :], 1e-30)
        for pair in range(n_rep // 2):
            h0 = heads[2 * pair]
            hh = slice(h0 * HEAD_DIM, (h0 + 2) * HEAD_DIM)
            gate = lambda j: _pair_lanes(gates[:, 3 * h0 + j:3 * h0 + j + 1], gates[:, 3 * h0 + 3 + j:3 * h0 + 4 + j])
            o_ref[0, :, hh] = (gate(0) * _heads_to_rows(ocmp_t, n_rep, pair)
                               + gate(1) * _heads_to_rows(oslc_t, n_rep, pair)
                               + gate(2) * ow_ref[0, :, hh])


def nsa_prompt(q_pad, gd, kc, x_rows, x_pages, kv_block, o_win):
    b, seq = q_pad.shape[:2]
    nb = kc.shape[1]
    nrow = NSA_GROUP * TQ
    kct = jnp.swapaxes(kc, 1, 2)
    return pl.pallas_call(
        functools.partial(_nsa_prompt_kernel, n_sel=min(SEL_TOPK, nb), kv_block=kv_block),
        grid=(b, seq // TQ),
        in_specs=[pl.BlockSpec((1, TQ, NSA_WIDTH), lambda i, j: (i, j, 0)),
                  pl.BlockSpec((1, TQ, LANE), lambda i, j: (i, j, 0)),
                  pl.BlockSpec((1, nb, KV_ROW), lambda i, j: (i, 0, 0)),
                  pl.BlockSpec((1, KV_ROW, nb), lambda i, j: (i, 0, 0)),
                  pl.BlockSpec((1, seq, KV_ROW), lambda i, j: (i, 0, kv_block)),
                  pl.BlockSpec((1,) + x_pages.shape[1:], lambda i, j: (i, 0, 0, 0)),
                  pl.BlockSpec((1, TQ, NSA_WIDTH), lambda i, j: (i, j, 0))],
        out_specs=pl.BlockSpec((1, TQ, NSA_WIDTH), lambda i, j: (i, j, 0)),
        out_shape=jax.ShapeDtypeStruct((b, seq, NSA_WIDTH), F32),
        scratch_shapes=[pltpu.VMEM((1, NSA_KV_HEADS * nrow), F32),
                        pltpu.VMEM((NSA_KV_HEADS, HEAD_DIM + 2 * SUBLANE, nrow), F32)],
        compiler_params=_cparams("parallel", "parallel"),
        name="nsa_prompt",
    )(q_pad, gd, kc, kct, x_rows, x_pages, o_win)


def _select_blocks(imp, n_sel):
    nb = imp.shape[1]
    ncol = lax.broadcasted_iota(jnp.int32, imp.shape, 1)
    count = jnp.zeros(imp.shape, F32)
    for i in range(nb):
        ci = imp[:, i:i + 1]
        count = count + jnp.where(ci > imp, 1.0, jnp.where(ci == imp, jnp.where(ncol > i, 1.0, 0.0), 0.0))
    return jnp.where(count < n_sel - 1, 1.0, 0.0)


def _nsa_decode_kernel(pt_ref, q_ref, gd_ref, kc_ref, new_ref, ow_ref, ex_ref, *rest, pages_per_step, pos0, n_sel):
    page_refs = rest[:pages_per_step]
    o_ref, m_scr, l_scr, acc_scr, mask_scr, bias_scr, ocmp_scr, q_scr = rest[pages_per_step:]
    p = pl.program_id(1)
    n_rep = NSA_GROUP
    nq = q_ref.shape[1]
    ngrp = n_rep * nq
    nrow = NSA_KV_HEADS * ngrp
    nb = kc_ref.shape[1]
    page = page_refs[0].shape[1]
    nkey = pages_per_step * page
    slopes = _alibi(NSA_HEADS)
    slope_col = _col(slopes, nq)

    def per_query(a):
        return jnp.concatenate([a[g] for g in range(NSA_KV_HEADS) for _ in range(n_rep)], axis=0)

    @pl.when(p == 0)
    def _():
        q_scr[...] = jnp.concatenate([_q_rows(q_ref, [g * n_rep + r for r in range(n_rep)], n_rep)
                                      for g in range(NSA_KV_HEADS)], axis=0)

    qrows = q_scr[...]

    @pl.when(p == 0)
    def _():
        kcb = kc_ref[0]
        tpos = pos0 + jnp.concatenate([lax.broadcasted_iota(jnp.int32, (nq, nb), 0)] * (nrow // nq), axis=0)
        ncol = lax.broadcasted_iota(jnp.int32, (nrow, nb), 1)
        dist = tpos - ((ncol + 1) * CMP_BLOCK - 1)
        valid = dist >= 0
        s = jnp.where(valid, _dot_nt(qrows, kcb) - slope_col * dist.astype(F32), NEG)
        e = jnp.where(valid, jnp.exp(s - jnp.max(s, axis=-1, keepdims=True)), 0.0)
        pc = e / jnp.maximum(jnp.sum(e, axis=-1, keepdims=True), 1e-30)
        ocmp_scr[...] = _dot(pc.astype(BF16), kcb)
        not_taken = []
        for g in range(NSA_KV_HEADS):
            imp = pc[g * ngrp:g * ngrp + nq]
            for r in range(1, n_rep):
                imp = imp + pc[g * ngrp + r * nq:g * ngrp + (r + 1) * nq]
            not_taken.append(jnp.where(_select_blocks(imp, n_sel) > 0.5, 0.0, MASK))
        not_taken = jnp.concatenate(not_taken, axis=0).astype(BF16)
        for st in range(mask_scr.shape[0]):
            mask_scr[st] = _dot(not_taken, ex_ref[:, st * nkey:(st + 1) * nkey])
        qi = jnp.concatenate([lax.broadcasted_iota(jnp.int32, (nq, nkey), 0)] * (nrow // nq), axis=0)
        kj = lax.broadcasted_iota(jnp.int32, (nrow, nkey), 1)
        bias_scr[...] = slope_col * (pos0 + qi - kj).astype(F32)
        m_scr[...] = jnp.full_like(m_scr, NEG)
        l_scr[...] = jnp.zeros_like(l_scr)
        acc_scr[...] = jnp.zeros_like(acc_scr)

    mask = mask_scr[p]
    shift = slope_col * (p * nkey).astype(F32)
    per = pages_per_step // N_STREAMS
    for st in range(N_STREAMS):
        ks = slice(st * per * page, (st + 1) * per * page)
        x_t = jnp.concatenate([r[...] for r in page_refs[st * per:(st + 1) * per]], axis=1).astype(BF16)
        s = (_dot(qrows, x_t) - bias_scr[:, ks]
             + per_query([mask[g * nq:(g + 1) * nq, ks] for g in range(NSA_KV_HEADS)]))
        m_old = m_scr[st]
        m_new = jnp.maximum(m_old, jnp.max(s, axis=-1, keepdims=True) + shift)
        alpha = jnp.exp(m_old - m_new)
        pr = jnp.exp(s - (m_new - shift))
        l_scr[st] = alpha * l_scr[st] + jnp.sum(pr, axis=-1, keepdims=True)
        acc_scr[st] = alpha * acc_scr[st] + _dot_nt(pr.astype(BF16), x_t)
        m_scr[st] = m_new

    @pl.when(p == pl.num_programs(1) - 1)
    def _():
        xn = jnp.concatenate([new_ref[0], jnp.zeros((TQ - nq, KV_ROW), F32)], axis=0).astype(BF16)
        qi_n = jnp.concatenate([lax.broadcasted_iota(jnp.int32, (nq, TQ), 0)] * (nrow // nq), axis=0)
        kj_n = lax.broadcasted_iota(jnp.int32, (nrow, TQ), 1)
        dist_n = qi_n - kj_n
        valid_n = (dist_n >= 0) & (kj_n < nq)
        s_n = jnp.where(valid_n, _dot_nt(qrows, xn) - slope_col * dist_n.astype(F32), NEG)
        m_new = jnp.max(s_n, axis=-1, keepdims=True)
        for st in range(N_STREAMS):
            m_new = jnp.maximum(m_new, m_scr[st])
        p_n = jnp.where(valid_n, jnp.exp(s_n - m_new), 0.0)
        den = jnp.sum(p_n, axis=-1, keepdims=True)
        num = _dot(p_n.astype(BF16), xn)
        for st in range(N_STREAMS):
            alpha = jnp.exp(m_scr[st] - m_new)
            den = den + alpha * l_scr[st]
            num = num + alpha * acc_scr[st]
        o_slc = num / jnp.maximum(den, 1e-30)
        o_cmp = ocmp_scr[...]
        gates = _sigmoid(gd_ref[0])
        for h in range(NSA_HEADS):
            g = h // n_rep
            v0 = (NSA_KV_HEADS + g) * HEAD_DIM
            rr = slice(h * nq, (h + 1) * nq)
            hh = slice(h * HEAD_DIM, (h + 1) * HEAD_DIM)
            o_ref[0, :, hh] = (gates[:, 3 * h:3 * h + 1] * o_cmp[rr, v0:v0 + HEAD_DIM]
                               + gates[:, 3 * h + 1:3 * h + 2] * o_slc[rr, v0:v0 + HEAD_DIM]
                               + gates[:, 3 * h + 2:3 * h + 3] * ow_ref[0, :, hh])


def nsa_decode(page_table, q_pad, gd, kc, new_kv, o_win, pool_t, pos0, pages_per_step=64):
    b, nq = q_pad.shape[:2]
    nb = kc.shape[1]
    n_pages = page_table.shape[1]
    page = pool_t.shape[2]
    pps = math.gcd(pages_per_step, n_pages)
    assert pps % N_STREAMS == 0
    n_steps = n_pages // pps
    nkey = pps * page
    nrow = NSA_HEADS * nq
    expand = (jnp.arange(n_pages * page)[None, :] // CMP_BLOCK == jnp.arange(nb)[:, None]).astype(BF16)

    def page_spec(j):
        return pl.BlockSpec((None, KV_ROW, page), lambda i, p, pt: (pt[i, p * pps + j], 0, 0))

    grid_spec = pltpu.PrefetchScalarGridSpec(
        num_scalar_prefetch=1,
        grid=(b, n_steps),
        in_specs=[pl.BlockSpec((1, nq, NSA_WIDTH), lambda i, p, pt: (i, 0, 0)),
                  pl.BlockSpec((1, nq, LANE), lambda i, p, pt: (i, 0, 0)),
                  pl.BlockSpec((1, nb, KV_ROW), lambda i, p, pt: (i, 0, 0)),
                  pl.BlockSpec((1, nq, KV_ROW), lambda i, p, pt: (i, 0, 0)),
                  pl.BlockSpec((1, nq, NSA_WIDTH), lambda i, p, pt: (i, 0, 0)),
                  pl.BlockSpec(expand.shape, lambda i, p, pt: (0, 0))]
                 + [page_spec(j) for j in range(pps)],
        out_specs=pl.BlockSpec((1, nq, NSA_WIDTH), lambda i, p, pt: (i, 0, 0)),
        scratch_shapes=[pltpu.VMEM((N_STREAMS, nrow, 1), F32), pltpu.VMEM((N_STREAMS, nrow, 1), F32),
                        pltpu.VMEM((N_STREAMS, nrow, KV_ROW), F32),
                        pltpu.VMEM((n_steps, NSA_KV_HEADS * nq, nkey), F32),
                        pltpu.VMEM((nrow, nkey), F32),
                        pltpu.VMEM((nrow, KV_ROW), F32),
                        pltpu.VMEM((nrow, KV_ROW), BF16)])
    return pl.pallas_call(
        functools.partial(_nsa_decode_kernel, pages_per_step=pps, pos0=pos0, n_sel=min(SEL_TOPK, nb + 1)),
        grid_spec=grid_spec,
        out_shape=jax.ShapeDtypeStruct((b, nq, NSA_WIDTH), F32),
        compiler_params=_cparams("parallel", "arbitrary"),
        name="nsa_decode",
    )(page_table, q_pad, gd, kc, new_kv, o_win, expand, *([pool_t] * pps))


def _cumsum_rows(v):
    n = v.shape[0]
    ri = lax.broadcasted_iota(jnp.int32, v.shape, 0)
    sh = 1
    while sh < n:
        v = v + jnp.where(ri >= sh, pltpu.roll(v, sh, 0), 0.0)
        sh *= 2
    return v


def _ssd_kernel(z_ref, xbc_ref, gd_ref, tail_ref, s0_ref, cw_ref, cb_ref, dtb_ref, alog_ref, dsk_ref, ng_ref,
                y_ref, sfin_ref, xp_scr, st_scr, y_scr, *, n_valid):
    c = pl.program_id(1)
    q = xbc_ref.shape[1]

    @pl.when(c == 0)
    def _():
        xp_scr[0:SUBLANE, :] = tail_ref[0]
        st_scr[...] = s0_ref[0]

    xp_scr[SUBLANE:SUBLANE + q, :] = xbc_ref[0]
    acc = cb_ref[...]
    for k in range(SSM_CONV):
        lo = SUBLANE - (SSM_CONV - 1) + k
        acc = acc + xp_scr[lo:lo + q, :] * cw_ref[k:k + 1, :]
    nxt = xp_scr[q:q + SUBLANE, :]
    xp_scr[0:SUBLANE, :] = nxt
    act = acc * _sigmoid(acc)

    raw = gd_ref[0] + dtb_ref[...]
    dt = jnp.maximum(raw, 0.0) + jnp.log1p(jnp.exp(-jnp.abs(raw)))
    if n_valid < q:
        dt = jnp.where(lax.broadcasted_iota(jnp.int32, dt.shape, 0) < n_valid, dt, 0.0)
    acum = _cumsum_rows(dt * (-jnp.exp(alog_ref[...])))
    acum_t = acum.T
    ri = lax.broadcasted_iota(jnp.int32, (q, q), 0)
    ci = lax.broadcasted_iota(jnp.int32, (q, q), 1)
    causal = ri >= ci
    half = lax.broadcasted_iota(jnp.int32, (2 * SSM_HEAD_DIM, 1), 0) < SSM_HEAD_DIM
    hpg = SSM_HEADS // SSM_GROUPS
    for pair in range(SSM_HEADS // 2):
        grp = (2 * pair) // hpg
        lanes = slice(pair * LANE, (pair + 1) * LANE)
        bm = act[:, SSM_D_INNER + grp * SSM_STATE:SSM_D_INNER + (grp + 1) * SSM_STATE].astype(BF16)
        cm_lo = SSM_D_INNER + SSM_GROUPS * SSM_STATE + grp * SSM_STATE
        cm = act[:, cm_lo:cm_lo + SSM_STATE].astype(BF16)
        cb = _dot_nt(cm, bm)
        xs = act[:, lanes]
        cols = []
        for h in (2 * pair, 2 * pair + 1):
            ln = DT_LANE + h
            cols.append((acum[:, ln:ln + 1], acum_t[ln:ln + 1, :], acum[q - 1:q, ln:ln + 1], dt[:, ln:ln + 1]))
        xdt = xs * _pair_lanes(cols[0][3], cols[1][3])
        xdt_bf = xdt.astype(BF16)
        y_parts = []
        for a_col, a_row, _, _ in cols:
            seg = a_col - a_row
            lmat = jnp.where(causal, jnp.exp(jnp.where(causal, seg, 0.0)), 0.0)
            y_parts.append(_dot((cb * lmat).astype(BF16), xdt_bf))
        lane = lax.broadcasted_iota(jnp.int32, (q, LANE), 1)
        y_diag = jnp.where(lane < SSM_HEAD_DIM, y_parts[0], y_parts[1])
        st = st_scr[pair * LANE:(pair + 1) * LANE, :]
        y_off = _dot_nt(cm, st.astype(BF16)) * _pair_lanes(jnp.exp(cols[0][0]), jnp.exp(cols[1][0]))
        dec_end = _pair_lanes(jnp.exp(cols[0][2] - cols[0][0]), jnp.exp(cols[1][2] - cols[1][0]))
        cs = _dot((xdt * dec_end).T.astype(BF16), bm)
        st_scr[pair * LANE:(pair + 1) * LANE, :] = st * jnp.where(half, jnp.exp(cols[0][2]), jnp.exp(cols[1][2])) + cs
        dsk = _pair_lanes(jnp.zeros((q, 1), F32) + dsk_ref[:, 2 * pair:2 * pair + 1],
                          jnp.zeros((q, 1), F32) + dsk_ref[:, 2 * pair + 1:2 * pair + 2])
        y_scr[:, lanes] = y_diag + y_off + dsk * xs

    z = z_ref[0]
    y_ref[0] = _rms(y_scr[...] * (z * _sigmoid(z)), ng_ref[...])

    @pl.when(c == pl.num_programs(1) - 1)
    def _():
        sfin_ref[0] = st_scr[...]


def ssd_mixer(z, xbc, gd, tail8, s0, conv_w, conv_b, dt_bias, a_log, d_skip, norm_g, n_valid):
    b, seq = z.shape[:2]
    nst = SSM_HEADS * SSM_HEAD_DIM

    def lane_piece(v):
        return jnp.zeros((1, LANE), F32).at[0, DT_LANE:DT_LANE + SSM_HEADS].set(v)

    alog_piece = jnp.full((1, LANE), -100.0, F32).at[0, DT_LANE:DT_LANE + SSM_HEADS].set(a_log)
    full2 = lambda shape: pl.BlockSpec(shape, lambda i, c: (0, 0))
    return pl.pallas_call(
        functools.partial(_ssd_kernel, n_valid=n_valid),
        grid=(b, seq // TQ),
        in_specs=[pl.BlockSpec((1, TQ, SSM_D_INNER), lambda i, c: (i, c, 0)),
                  pl.BlockSpec((1, TQ, SSM_CONV_DIM), lambda i, c: (i, c, 0)),
                  pl.BlockSpec((1, TQ, LANE), lambda i, c: (i, c, 0)),
                  pl.BlockSpec((1, SUBLANE, SSM_CONV_DIM), lambda i, c: (i, 0, 0)),
                  pl.BlockSpec((1, nst, SSM_STATE), lambda i, c: (i, 0, 0)),
                  full2((SSM_CONV, SSM_CONV_DIM)), full2((1, SSM_CONV_DIM)), full2((1, LANE)),
                  full2((1, LANE)), full2((1, SSM_HEADS)), full2((1, SSM_D_INNER))],
        out_specs=[pl.BlockSpec((1, TQ, SSM_D_INNER), lambda i, c: (i, c, 0)),
                   pl.BlockSpec((1, nst, SSM_STATE), lambda i, c: (i, 0, 0))],
        out_shape=[jax.ShapeDtypeStruct((b, seq, SSM_D_INNER), F32),
                   jax.ShapeDtypeStruct((b, nst, SSM_STATE), F32)],
        scratch_shapes=[pltpu.VMEM((SUBLANE + TQ, SSM_CONV_DIM), F32),
                        pltpu.VMEM((nst, SSM_STATE), F32),
                        pltpu.VMEM((TQ, SSM_D_INNER), F32)],
        compiler_params=_cparams("parallel", "arbitrary"),
        name="ssd_mixer",
    )(z, xbc, gd, tail8, s0, conv_w, conv_b.reshape(1, -1), lane_piece(dt_bias), alog_piece,
      d_skip.reshape(1, -1), norm_g.reshape(1, -1))


def _channel_major(a6):
    b, rows = a6.shape[:2]
    return jnp.transpose(a6, (0, 2, 3, 4, 1)).reshape(b, KV_ROW, rows)


def _token_major(a_t):
    b, _, rows = a_t.shape
    return jnp.transpose(a_t.reshape(b, 2, NSA_KV_HEADS, HEAD_DIM, rows), (0, 4, 1, 2, 3))


def _even_weights(w_in):
    d = w_in.shape[0]
    o = np.cumsum([0, NSA_WIDTH, KV_ROW, KV_ROW, KV_ROW, 3 * NSA_HEADS, SSM_D_INNER, SSM_CONV_DIM, SSM_HEADS])
    q, kvc, kvs, kvw, gt, z, xbc, dtr = (w_in[:, o[i]:o[i + 1]] for i in range(8))
    gd = jnp.zeros((d, LANE), F32).at[:, :3 * NSA_HEADS].set(gt).at[:, DT_LANE:DT_LANE + SSM_HEADS].set(dtr)
    w = jnp.concatenate([q * SCALE, kvc, kvs, kvw, z, xbc, gd], axis=1).astype(BF16)
    qw = NSA_WIDTH
    off = {"q": (0, qw), "kvc": (qw, KV_ROW), "kvs": (qw + KV_ROW, KV_ROW), "kvw": (qw + 2 * KV_ROW, KV_ROW),
           "kv3": (qw, 3 * KV_ROW), "z": (qw + 3 * KV_ROW, SSM_D_INNER),
           "xbc": (qw + 3 * KV_ROW + SSM_D_INNER, SSM_CONV_DIM),
           "gd": (qw + 3 * KV_ROW + SSM_D_INNER + SSM_CONV_DIM, LANE)}
    return w, off


def _even_layer(x, past, p, cache):
    (norm_g, w_in, w_out, cw, ssm_p) = p
    b, seq, d = x.shape
    m = b * seq
    decode = cache is not None
    w, off = _even_weights(w_in)
    conv_w, conv_b, dt_bias, a_log, d_skip, ssm_norm = ssm_p
    if not decode:
        names = [("row", "q", F32), ("chan", "kvc", F32), ("chan", "kvs", F32), ("chan", "kvw", F32),
                 ("page", "kvc", F32), ("row", "kv3", BF16), ("page", "kv3", BF16),
                 ("row", "z", F32), ("row", "xbc", F32), ("row", "gd", F32)]
        q, kvc_t, kvs_t, kvw_t, kvc_pages, x_rows, x_pages, z, xbc, gd = norm_proj(
            x, norm_g, w, [(k,) + off[n] for k, n, _ in names], [dt for _, _, dt in names])
        q = q.reshape(b, seq, -1)
        gd3 = gd.reshape(b, seq, LANE)
        x_rows = x_rows.reshape(b, seq, 3 * KV_ROW)
        x_pages = x_pages.reshape(b, seq // TQ, 3 * KV_ROW, TQ)
        kc = compress(kvc_pages.reshape(-1, TQ), cw).reshape(b, seq // CMP_BLOCK, KV_ROW)
        o_win = banded_attention(q, x_rows, x_pages, 2, NSA_KV_HEADS, NSA_GROUP, NSA_WINDOW)
        o_nsa = nsa_prompt(q, gd3, kc, x_rows, x_pages, 1, o_win)
        tail8 = jnp.zeros((b, SUBLANE, SSM_CONV_DIM), F32)
        s0 = jnp.zeros((b, SSM_D_INNER, SSM_STATE), F32)
        xbc3 = xbc.reshape(b, seq, -1)
        o_ssm, s_fin = ssd_mixer(z.reshape(b, seq, -1), xbc3, gd3, tail8, s0,
                                 conv_w, conv_b, dt_bias, a_log, d_skip, ssm_norm, TQ)
        conv_new = xbc3[:, seq - (SSM_CONV - 1):]
        kv_outs = (_token_major(kvc_t), _token_major(kvs_t), _token_major(kvw_t[:, :, seq - min(NSA_WINDOW, seq):]))
    else:
        cmp_pool_t, slc_pool_t, page_table, win_state_t, conv0, ssm0 = cache
        n_pool, _, page = cmp_pool_t.shape
        assert page == TQ and seq < CMP_BLOCK and TQ % seq == 0
        names = [("row", "q", F32), ("row", "kvc", F32), ("row", "kvs", F32), ("row", "kvw", F32),
                 ("row", "z", F32), ("row", "xbc", F32), ("row", "gd", F32)]
        q, kvc, kvs, kvw, z, xbc, gd = norm_proj(
            x, norm_g, w, [(k,) + off[n] for k, n, _ in names], [dt for _, _, dt in names])
        q = q.reshape(b, seq, -1)
        gd3 = gd.reshape(b, seq, LANE)
        bpp = page // CMP_BLOCK
        kc_pool = compress(cmp_pool_t.reshape(n_pool * KV_ROW, page), cw)
        kc = kc_pool.reshape(n_pool, bpp, KV_ROW)[page_table].reshape(b, past // CMP_BLOCK, KV_ROW)
        kvw3 = kvw.reshape(b, seq, KV_ROW)
        o_win = banded_decode(q, win_state_t, kvw3, NSA_KV_HEADS, NSA_GROUP, NSA_WINDOW, past)
        o_nsa = nsa_decode(page_table, q, gd3, kc, kvs.reshape(b, seq, KV_ROW), o_win, slc_pool_t, past)
        padr = lambda a: jnp.pad(a.reshape(b, seq, -1), ((0, 0), (0, TQ - seq), (0, 0)))
        tail8 = jnp.pad(conv0, ((0, 0), (SUBLANE - (SSM_CONV - 1), 0), (0, 0)))
        o_ssm, s_fin = ssd_mixer(padr(z), padr(xbc), padr(gd), tail8, ssm0.reshape(b, SSM_D_INNER, SSM_STATE),
                                 conv_w, conv_b, dt_bias, a_log, d_skip, ssm_norm, seq)
        o_ssm = o_ssm[:, :seq]
        conv_new = jnp.concatenate([conv0, xbc.reshape(b, seq, -1)], axis=1)[:, -(SSM_CONV - 1):]
        npast = win_state_t.shape[2]
        win_out = jnp.concatenate([win_state_t, jnp.swapaxes(kvw3, 1, 2)], axis=2)[:, :, -npast:]
        kv6 = (b, seq, 2, NSA_KV_HEADS, HEAD_DIM)
        kv_outs = (kvc.reshape(kv6), kvs.reshape(kv6), _token_major(win_out))
    x = proj_res([o_nsa.reshape(m, -1), o_ssm.reshape(m, -1)],
                 [w_out[:NSA_WIDTH].astype(BF16), w_out[NSA_WIDTH:].astype(BF16)], x.reshape(m, d))
    outs = kv_outs + (s_fin.reshape(b, SSM_HEADS, SSM_HEAD_DIM, SSM_STATE), conv_new)
    return x.reshape(b, seq, d), outs


def _odd_layer(x, past, p, swa_state_t):
    norm_g, w_in, w_out, sinks = p
    b, seq, d = x.shape
    m = b * seq
    decode = swa_state_t is not None
    w = jnp.concatenate([w_in[:, :SWA_WIDTH] * SCALE, w_in[:, SWA_WIDTH:]], axis=1).astype(BF16)
    qw = SWA_WIDTH
    if not decode:
        q, kv_t, x_rows, x_pages = norm_proj(
            x, norm_g, w, [("row", 0, qw), ("chan", qw, KV_ROW), ("row", qw, KV_ROW), ("page", qw, KV_ROW)],
            [F32, F32, BF16, BF16])
        o = banded_attention(q.reshape(b, seq, -1), x_rows.reshape(b, seq, KV_ROW),
                             x_pages.reshape(b, seq // TQ, KV_ROW, TQ), 0, SWA_KV_HEADS, SWA_GROUP, SWA_WINDOW, sinks)
        kv_out = _token_major(kv_t[:, :, seq - min(SWA_WINDOW, seq):])
    else:
        q, kv = norm_proj(x, norm_g, w, [("row", 0, qw), ("row", qw, KV_ROW)], [F32, F32])
        kv3 = kv.reshape(b, seq, KV_ROW)
        o = banded_decode(q.reshape(b, seq, -1), swa_state_t, kv3, SWA_KV_HEADS, SWA_GROUP, SWA_WINDOW, past, sinks)
        npast = swa_state_t.shape[2]
        kv_out = _token_major(jnp.concatenate([swa_state_t, jnp.swapaxes(kv3, 1, 2)], axis=2)[:, :, -npast:])
    x = proj_res([o.reshape(m, -1)], [w_out.astype(BF16)], x.reshape(m, d))
    return x.reshape(b, seq, d), kv_out


def kernel(x_prompt, x_sample, cache_nsa_cmp_kv, cache_nsa_slc_kv, state_nsa_win_kv, state_ssm, state_ssm_conv, state_swa_kv, page_table, norm_mix, norm_ffn, norm_final, w_in_even, w_out_even, cmp_pe_k, cmp_w1_k, cmp_w2_k, cmp_pe_v, cmp_w1_v, cmp_w2_v, ssm_conv_w, ssm_conv_b, ssm_dt_bias, ssm_a_log, ssm_d, ssm_norm, w_in_odd, w_out_odd, swa_sinks, w_gate_up, w_down):
    depth = norm_mix.shape[0]
    page = cache_nsa_cmp_kv.shape[2]
    past = page_table.shape[1] * page
    xp, xs = x_prompt, x_sample
    outs_p = [[] for _ in range(6)]
    outs_s = [[] for _ in range(6)]
    for layer in range(depth):
        if layer % 2 == 0:
            e = layer // 2
            cw = _compress_weights(cmp_pe_k[e], cmp_w1_k[e], cmp_w2_k[e], cmp_pe_v[e], cmp_w1_v[e], cmp_w2_v[e])
            ssm_p = (ssm_conv_w[e], ssm_conv_b[e], ssm_dt_bias[e], ssm_a_log[e], ssm_d[e], ssm_norm[e])
            p = (norm_mix[layer], w_in_even[e], w_out_even[e], cw, ssm_p)
            xp, o = _even_layer(xp, 0, p, None)
            for lst, v in zip(outs_p[:5], o):
                lst.append(v)
            cache = (_channel_major(cache_nsa_cmp_kv[e]), _channel_major(cache_nsa_slc_kv[e]), page_table,
                     _channel_major(state_nsa_win_kv[e]), state_ssm_conv[e], state_ssm[e])
            xs, o = _even_layer(xs, past, p, cache)
            for lst, v in zip(outs_s[:5], o):
                lst.append(v)
        else:
            o_idx = layer // 2
            p = (norm_mix[layer], w_in_odd[o_idx], w_out_odd[o_idx], swa_sinks[o_idx])
            xp, kw = _odd_layer(xp, 0, p, None)
            outs_p[5].append(kw)
            xs, kw = _odd_layer(xs, past, p, _channel_major(state_swa_kv[o_idx]))
            outs_s[5].append(kw)
        gf = norm_final if layer == depth - 1 else None
        wgu, wd = w_gate_up[layer].astype(BF16), w_down[layer].astype(BF16)
        xp = ffn(xp.reshape(-1, xp.shape[-1]), norm_ffn[layer], wgu, wd, gf).reshape(xp.shape)
        xs = ffn(xs.reshape(-1, xs.shape[-1]), norm_ffn[layer], wgu, wd, gf).reshape(xs.shape)
    return (xp, xs) + tuple(jnp.stack(v) for v in outs_p) + tuple(jnp.stack(v) for v in outs_s)
```

```python
import functools
import math

import numpy as np
import jax
import jax.numpy as jnp
from jax import lax
from jax.experimental import pallas as pl
from jax.experimental.pallas import tpu as pltpu

F32 = jnp.float32
BF16 = jnp.bfloat16

HEAD_DIM = 64
NSA_HEADS = 8
NSA_KV_HEADS = 2
NSA_GROUP = NSA_HEADS // NSA_KV_HEADS
CMP_BLOCK = 64
SEL_TOPK = 16
NSA_WINDOW = 512
SSM_HEADS = 8
SSM_HEAD_DIM = 64
SSM_D_INNER = SSM_HEADS * SSM_HEAD_DIM
SSM_GROUPS = 2
SSM_STATE = 128
SSM_CONV = 4
SSM_CONV_DIM = SSM_D_INNER + 2 * SSM_GROUPS * SSM_STATE
SWA_HEADS = 16
SWA_KV_HEADS = 2
SWA_GROUP = SWA_HEADS // SWA_KV_HEADS
SWA_WINDOW = 128
RMS_EPS = 1e-6
NSA_WIDTH = NSA_HEADS * HEAD_DIM
SWA_WIDTH = SWA_HEADS * HEAD_DIM
SCALE = HEAD_DIM ** -0.5

LANE = 128
SUBLANE = 8
KV_ROW = 2 * NSA_KV_HEADS * HEAD_DIM
TQ = 128
NEG = -1e30
MASK = -2e30
VMEM_LIMIT = 56 * 1024 * 1024
DT_LANE = 3 * NSA_HEADS
N_STREAMS = 1
NSA_CHUNK_TILES = 4

def _cparams(*sem):
    return pltpu.CompilerParams(dimension_semantics=sem, vmem_limit_bytes=VMEM_LIMIT)


def _dot(a, b):
    return jnp.dot(a, b, preferred_element_type=F32)


def _dot_nt(a, b):
    return lax.dot_general(a, b, (((1,), (1,)), ((), ())), preferred_element_type=F32)


def _rms(x, g):
    return x * lax.rsqrt(jnp.mean(x * x, axis=-1, keepdims=True) + RMS_EPS) * g


def _sigmoid(x):
    return 1.0 / (1.0 + jnp.exp(-x))


def _alibi(n_heads):
    return [float(2.0 ** (-8.0 * i / n_heads)) for i in range(1, n_heads + 1)]


def _col(vals, rows):
    return jnp.concatenate([jnp.full((rows, 1), v, F32) for v in vals], axis=0)


def _row(vals, cols):
    return jnp.concatenate([jnp.full((1, cols), v, F32) for v in vals], axis=1)


def _pair_lanes(c0, c1):
    lane = lax.broadcasted_iota(jnp.int32, (c0.shape[0], LANE), 1)
    return jnp.where(lane < HEAD_DIM, c0, c1)


def _norm_proj_kernel(x_ref, g_ref, w_ref, *o_refs, pieces):
    hb = _rms(x_ref[...], g_ref[...]).astype(BF16)
    cache = {}
    spans = sorted({(off, n) for _, off, n in pieces}, key=lambda s: -s[1])
    for o_ref, (kind, off, n) in zip(o_refs, pieces):
        p_off, p_n = next((o2, n2) for o2, n2 in spans if o2 <= off and off + n <= o2 + n2)
        if (p_off, p_n) not in cache:
            cache[(p_off, p_n)] = _dot(hb, w_ref[:, p_off:p_off + p_n])
        if (off, n) not in cache:
            cache[(off, n)] = cache[(p_off, p_n)][:, off - p_off:off - p_off + n]
        y = cache[(off, n)]
        if kind == "row":
            o_ref[...] = y.astype(o_ref.dtype)
        else:
            if ("t", off, n) not in cache:
                cache[("t", off, n)] = y.T
            y_t = cache[("t", off, n)]
            if kind == "chan":
                o_ref[0] = y_t.astype(o_ref.dtype)
            else:
                for j in range(o_ref.shape[0]):
                    o_ref[j] = y_t[:, j * TQ:(j + 1) * TQ].astype(o_ref.dtype)


def norm_proj(x3d, g, w_bf, pieces, dtypes, tm=256):
    b, seq, d = x3d.shape
    m = b * seq
    n_tot = w_bf.shape[1]
    tm = min(tm, m)
    per_b = seq // tm if seq >= tm else 1
    out_specs, out_shape = [], []
    for (kind, _, n), dt in zip(pieces, dtypes):
        if kind == "row":
            out_specs.append(pl.BlockSpec((tm, n), lambda i: (i, 0)))
            out_shape.append(jax.ShapeDtypeStruct((m, n), dt))
        elif kind == "chan":
            out_specs.append(pl.BlockSpec((1, n, tm), lambda i: (i // per_b, 0, i % per_b)))
            out_shape.append(jax.ShapeDtypeStruct((b, n, seq), dt))
        else:
            out_specs.append(pl.BlockSpec((tm // TQ, n, TQ), lambda i: (i, 0, 0)))
            out_shape.append(jax.ShapeDtypeStruct((m // TQ, n, TQ), dt))
    return pl.pallas_call(
        functools.partial(_norm_proj_kernel, pieces=tuple(pieces)),
        grid=(m // tm,),
        in_specs=[pl.BlockSpec((tm, d), lambda i: (i, 0)),
                  pl.BlockSpec((1, d), lambda i: (0, 0)),
                  pl.BlockSpec((d, n_tot), lambda i: (0, 0))],
        out_specs=out_specs,
        out_shape=out_shape,
        compiler_params=_cparams("parallel"),
        name="norm_proj",
    )(x3d.reshape(m, d), g.reshape(1, d), w_bf)


def _proj_res_kernel(*refs, n_in):
    a_refs, w_refs = refs[:n_in], refs[n_in:2 * n_in]
    res_ref, o_ref = refs[2 * n_in], refs[2 * n_in + 1]
    acc = res_ref[...]
    for a_ref, w_ref in zip(a_refs, w_refs):
        acc = acc + _dot(a_ref[...].astype(BF16), w_ref[...])
    o_ref[...] = acc


def proj_res(a_list, w_list, res, tm=512):
    m, d = res.shape
    tm = min(tm, m)
    n_in = len(a_list)
    in_specs = ([pl.BlockSpec((tm, a.shape[1]), lambda i: (i, 0)) for a in a_list]
                + [pl.BlockSpec(w.shape, lambda i: (0, 0)) for w in w_list]
                + [pl.BlockSpec((tm, d), lambda i: (i, 0))])
    return pl.pallas_call(
        functools.partial(_proj_res_kernel, n_in=n_in),
        grid=(m // tm,),
        in_specs=in_specs,
        out_specs=pl.BlockSpec((tm, d), lambda i: (i, 0)),
        out_shape=jax.ShapeDtypeStruct((m, d), F32),
        compiler_params=_cparams("parallel"),
        name="proj_res",
    )(*a_list, *w_list, res)


def _ffn_kernel(x_ref, g_ref, wg_ref, wu_ref, wd_ref, *rest, final):
    if final:
        gf_ref, o_ref, h_scr, acc_scr = rest
    else:
        o_ref, h_scr, acc_scr = rest
    j = pl.program_id(1)

    @pl.when(j == 0)
    def _():
        x = x_ref[...]
        h_scr[...] = _rms(x, g_ref[...]).astype(BF16)
        acc_scr[...] = x

    hb = h_scr[...]
    gate = _dot(hb, wg_ref[...])
    up = _dot(hb, wu_ref[...])
    act = gate * _sigmoid(gate) * up
    acc_scr[...] += _dot(act.astype(BF16), wd_ref[...])

    @pl.when(j == pl.num_programs(1) - 1)
    def _():
        y = acc_scr[...]
        if final:
            y = _rms(y, gf_ref[...])
        o_ref[...] = y


def ffn(x2d, g, w_gu_bf, w_down_bf, g_final=None, tm=512):
    m, d = x2d.shape
    f = w_down_bf.shape[0]
    tm = min(tm, m)
    nf = 2 if (f // 2) % LANE == 0 else 1
    tf = f // nf
    final = g_final is not None
    in_specs = [pl.BlockSpec((tm, d), lambda i, j: (i, 0)),
                pl.BlockSpec((1, d), lambda i, j: (0, 0)),
                pl.BlockSpec((d, tf), lambda i, j: (0, j)),
                pl.BlockSpec((d, tf), lambda i, j: (0, j + nf)),
                pl.BlockSpec((tf, d), lambda i, j: (j, 0))]
    args = [x2d, g.reshape(1, d), w_gu_bf, w_gu_bf, w_down_bf]
    if final:
        in_specs.append(pl.BlockSpec((1, d), lambda i, j: (0, 0)))
        args.append(g_final.reshape(1, d))
    return pl.pallas_call(
        functools.partial(_ffn_kernel, final=final),
        grid=(m // tm, nf),
        in_specs=in_specs,
        out_specs=pl.BlockSpec((tm, d), lambda i, j: (i, 0)),
        out_shape=jax.ShapeDtypeStruct((m, d), F32),
        scratch_shapes=[pltpu.VMEM((tm, d), BF16), pltpu.VMEM((tm, d), F32)],
        compiler_params=_cparams("parallel", "arbitrary"),
        name="ffn",
    )(*args)


def _gelu_tanh(x):
    c = math.sqrt(2.0 / math.pi)
    return x * (0.5 * (1.0 + jnp.tanh(c * (x + 0.044715 * (x * x * x)))))


def _transpose8(v):
    row = lax.broadcasted_iota(jnp.int32, (SUBLANE, LANE), 0)
    for s in (4, 2, 1):
        low = (row & s) == 0
        out = list(v)
        for i in range(SUBLANE):
            if i & s == 0:
                out[i] = jnp.where(low, v[i], pltpu.roll(v[i + s], s, 0))
                out[i + s] = jnp.where(low, pltpu.roll(v[i], SUBLANE - s, 0), v[i + s])
        v = out
    return v


def _compress_kernel(x_ref, pe_ref, w1k_ref, w1v_ref, w2_ref, o_ref, *, n_pages):
    hidden = []
    for kv, w1_ref in enumerate((w1k_ref, w1v_ref)):
        slabs = []
        for c in range(kv * NSA_KV_HEADS, (kv + 1) * NSA_KV_HEADS):
            per_d = [[] for _ in range(HEAD_DIM)]
            for p0 in range(0, n_pages, SUBLANE):
                for d0 in range(0, HEAD_DIM, SUBLANE):
                    tiles = [x_ref[pl.ds((p0 + p) * KV_ROW + c * HEAD_DIM + d0, SUBLANE), :] for p in range(SUBLANE)]
                    for r, u in enumerate(_transpose8(tiles)):
                        per_d[d0 + r].append(u)
            rows = [jnp.concatenate(per_d[d], axis=0) for d in range(HEAD_DIM)]
            slabs.append(jnp.concatenate(rows, axis=1).astype(BF16))
        h = _gelu_tanh(_dot(jnp.concatenate(slabs, axis=0), w1_ref[...]) + pe_ref[kv:kv + 1, :]).astype(BF16)
        hidden += [h[i * n_pages:(i + 1) * n_pages] for i in range(NSA_KV_HEADS)]
    o_ref[...] = _dot(jnp.concatenate(hidden, axis=1), w2_ref[...]).astype(o_ref.dtype)


def compress(pages2d, cw, pages_per_step=64):
    pe4, w1k, w1v, w2 = cw
    n_pages = pages2d.shape[0] // KV_ROW
    pps = math.gcd(pages_per_step, n_pages)
    blocks_per_page = TQ // CMP_BLOCK
    const = lambda a: pl.BlockSpec(a.shape, lambda i: (0, 0))
    return pl.pallas_call(
        functools.partial(_compress_kernel, n_pages=pps),
        grid=(n_pages // pps,),
        in_specs=[pl.BlockSpec((pps * KV_ROW, TQ), lambda i: (i, 0)), const(pe4), const(w1k), const(w1v), const(w2)],
        out_specs=pl.BlockSpec((pps, blocks_per_page * KV_ROW), lambda i: (i, 0)),
        out_shape=jax.ShapeDtypeStruct((n_pages, blocks_per_page * KV_ROW), BF16),
        compiler_params=_cparams("parallel"),
        name="compress",
    )(pages2d, pe4, w1k, w1v, w2)


def _compress_weights(pe_k, w1_k, w2_k, pe_v, w1_v, w2_v):
    nj = TQ // CMP_BLOCK
    nc = 2 * NSA_KV_HEADS
    l, d, h = w1_k.shape

    def first(w1):
        w = jnp.zeros((d, nj, l, nj, h), F32)
        for j in range(nj):
            w = w.at[:, j, :, j, :].set(jnp.transpose(w1, (1, 0, 2)))
        return w.reshape(d * nj * l, nj * h).astype(BF16)

    def pe_row(pe):
        return jnp.broadcast_to(pe.T[:, None, :], (d, nj, l)).reshape(1, d * nj * l)

    w2 = jnp.zeros((nc, nj, h, nj, nc, w2_k.shape[1]), F32)
    for c in range(nc):
        for j in range(nj):
            w2 = w2.at[c, j, :, j, c, :].set(w2_k if c < NSA_KV_HEADS else w2_v)
    w1k, w1v = first(w1_k), first(w1_v)
    pe_hidden = jnp.concatenate([jnp.dot(pe_row(pe), w1.astype(F32), precision=lax.Precision.HIGHEST)
                                 for pe, w1 in ((pe_k, w1k), (pe_v, w1v))], axis=0)
    return pe_hidden, w1k, w1v, w2.reshape(nc * nj * h, nj * nc * w2_k.shape[1]).astype(BF16)


def _q_rows(q_ref, heads, n_rep, bi=0):
    assert NSA_KV_HEADS * HEAD_DIM == LANE
    nq = q_ref.shape[1]
    lane_half = lax.broadcasted_iota(jnp.int32, (nq, LANE), 1) // HEAD_DIM
    rows = []
    for h in heads:
        g = h // n_rep
        tile = q_ref[bi, :, (h // 2) * LANE:(h // 2 + 1) * LANE]
        if h % 2 != g:
            tile = pltpu.roll(tile, HEAD_DIM, 1)
        rows.append(jnp.where(lane_half == g, tile, 0.0))
    q_keys = jnp.concatenate(rows, axis=0)
    return jnp.concatenate([q_keys, jnp.zeros_like(q_keys)], axis=1).astype(BF16)


def _heads_to_rows(o_t, n_rep, pair):
    r0 = 2 * pair
    two = jnp.concatenate([o_t[:, r0 * TQ:(r0 + 1) * TQ], o_t[:, (r0 + 1) * TQ:(r0 + 2) * TQ]], axis=0)
    return two.T


def _banded_kernel(*refs, n_groups, n_rep, window, slopes, has_sink, kv_block):
    if has_sink:
        sink_ref, q_ref, x_ref, xt_ref, o_ref = refs
    else:
        q_ref, x_ref, xt_ref, o_ref = refs
    qt = pl.program_id(1)
    n_heads = n_groups * n_rep
    nrow = n_heads * TQ
    n_tiles = window // TQ + 1
    ki = lax.broadcasted_iota(jnp.int32, (TQ, nrow), 0)
    qi = lax.broadcasted_iota(jnp.int32, (TQ, nrow), 1) % TQ
    diff = qi - ki
    diff_f = diff.astype(F32)
    tiles = []
    for j in range(n_tiles):
        kt = qt - (n_tiles - 1 - j)
        ktc = jnp.maximum(kt, 0)
        x = x_ref[0, pl.ds(pl.multiple_of(ktc * TQ, TQ), TQ), :]
        tiles.append((x, ktc, jnp.where(kt >= 0, 0.0, MASK)))
    heads = list(range(n_heads))
    qrows = jnp.concatenate([_q_rows(q_ref, heads[g * n_rep:(g + 1) * n_rep], n_rep) for g in range(n_groups)], axis=0)
    slope_row = _row([slopes[h] for h in heads], TQ)
    bias0 = slope_row * diff_f
    s_list = []
    for j, (x, _, before_start) in enumerate(tiles):
        s = _dot_nt(x, qrows) - bias0 + (before_start - slope_row * float((n_tiles - 1 - j) * TQ))
        if j == 0:
            s = jnp.where(diff <= 0, s, MASK)
        if j == n_tiles - 1:
            s = jnp.where(diff >= 0, s, MASK)
        s_list.append(s)
    m = s_list[0].max(axis=0, keepdims=True)
    for s in s_list[1:]:
        m = jnp.maximum(m, s.max(axis=0, keepdims=True))
    if has_sink:
        sink_row = jnp.concatenate([jnp.full((1, TQ), sink_ref[h], F32) for h in heads], axis=1)
        m = jnp.maximum(m, sink_row)
    p_list = [jnp.exp(s - m) for s in s_list]
    den = p_list[0].sum(axis=0, keepdims=True)
    for p in p_list[1:]:
        den = den + p.sum(axis=0, keepdims=True)
    if has_sink:
        den = den + jnp.exp(sink_row - m)
    p_all = jnp.concatenate([p.astype(BF16) for p in p_list], axis=0)
    for g in range(n_groups):
        cols = slice(g * n_rep * TQ, (g + 1) * n_rep * TQ)
        v0 = kv_block * KV_ROW + (NSA_KV_HEADS + g) * HEAD_DIM
        v_all = jnp.concatenate([xt_ref[0, ktc, v0:v0 + HEAD_DIM, :] for _, ktc, _ in tiles], axis=1)
        o_t = _dot(v_all, p_all[:, cols]) / jnp.maximum(den[:, cols], 1e-30)
        for pair in range(n_rep // 2):
            h0 = g * n_rep + 2 * pair
            o_ref[0, :, h0 * HEAD_DIM:(h0 + 2) * HEAD_DIM] = _heads_to_rows(o_t, n_rep, pair)


def banded_attention(q_pad, x_rows, x_pages, kv_block, n_groups, n_rep, window, sinks=None):
    b, seq = q_pad.shape[:2]
    n_heads = n_groups * n_rep
    has_sink = sinks is not None
    in_specs = [pl.BlockSpec((1, TQ, n_heads * HEAD_DIM), lambda i, j: (i, j, 0)),
                pl.BlockSpec((1, seq, KV_ROW), lambda i, j: (i, 0, kv_block)),
                pl.BlockSpec((1,) + x_pages.shape[1:], lambda i, j: (i, 0, 0, 0))]
    args = [q_pad, x_rows, x_pages]
    if has_sink:
        in_specs.insert(0, pl.BlockSpec(memory_space=pltpu.SMEM))
        args.insert(0, sinks)
    assert window % TQ == 0 and window >= TQ
    return pl.pallas_call(
        functools.partial(_banded_kernel, n_groups=n_groups, n_rep=n_rep, window=window,
                          slopes=_alibi(n_heads), has_sink=has_sink, kv_block=kv_block),
        grid=(b, seq // TQ),
        in_specs=in_specs,
        out_specs=pl.BlockSpec((1, TQ, n_heads * HEAD_DIM), lambda i, j: (i, j, 0)),
        out_shape=jax.ShapeDtypeStruct((b, seq, n_heads * HEAD_DIM), F32),
        compiler_params=_cparams("parallel", "parallel"),
        name="banded_attention",
    )(*args)


def _banded_decode_kernel(*refs, n_groups, n_rep, window, slopes, has_sink, pos0):
    if has_sink:
        sink_ref, q_ref, st_ref, new_ref, o_ref = refs
    else:
        q_ref, st_ref, new_ref, o_ref = refs
    nq = q_ref.shape[1]
    npast = st_ref.shape[2]
    nrow = n_rep * nq
    qi_s = jnp.concatenate([lax.broadcasted_iota(jnp.int32, (nq, npast), 0)] * n_rep, axis=0)
    kj_s = lax.broadcasted_iota(jnp.int32, (nrow, npast), 1)
    dist_s = npast + qi_s - kj_s
    valid_s = (dist_s >= 0) & (dist_s <= window) & (pos0 - npast + kj_s >= 0)
    qi_n = jnp.concatenate([lax.broadcasted_iota(jnp.int32, (nq, TQ), 0)] * n_rep, axis=0)
    kj_n = lax.broadcasted_iota(jnp.int32, (nrow, TQ), 1)
    dist_n = qi_n - kj_n
    valid_n = (dist_n >= 0) & (dist_n <= window) & (kj_n < nq)
    for bi, g in [(bi, g) for bi in range(q_ref.shape[0]) for g in range(n_groups)]:
        xs_t = st_ref[bi].astype(BF16)
        xn = jnp.concatenate([new_ref[bi], jnp.zeros((TQ - nq, KV_ROW), F32)], axis=0).astype(BF16)
        heads = [g * n_rep + r for r in range(n_rep)]
        qrows = _q_rows(q_ref, heads, n_rep, bi)
        slope_col = _col([slopes[h] for h in heads], nq)
        s_s = jnp.where(valid_s, _dot(qrows, xs_t) - slope_col * dist_s.astype(F32), NEG)
        s_n = jnp.where(valid_n, _dot_nt(qrows, xn) - slope_col * dist_n.astype(F32), NEG)
        m = jnp.maximum(jnp.max(s_s, axis=-1, keepdims=True), jnp.max(s_n, axis=-1, keepdims=True))
        if has_sink:
            sink_col = jnp.concatenate([jnp.full((nq, 1), sink_ref[h], F32) for h in heads], axis=0)
            m = jnp.maximum(m, sink_col)
        p_s = jnp.where(valid_s, jnp.exp(s_s - m), 0.0)
        p_n = jnp.where(valid_n, jnp.exp(s_n - m), 0.0)
        den = jnp.sum(p_s, axis=-1, keepdims=True) + jnp.sum(p_n, axis=-1, keepdims=True)
        if has_sink:
            den = den + jnp.exp(sink_col - m)
        o = (_dot_nt(p_s.astype(BF16), xs_t) + _dot(p_n.astype(BF16), xn)) / jnp.maximum(den, 1e-30)
        v0 = (NSA_KV_HEADS + g) * HEAD_DIM
        for r, h in enumerate(heads):
            o_ref[bi, :, h * HEAD_DIM:(h + 1) * HEAD_DIM] = o[r * nq:(r + 1) * nq, v0:v0 + HEAD_DIM]


def banded_decode(q_pad, state_t, new_kv, n_groups, n_rep, window, pos0, sinks=None, rows_per_step=4):
    b, nq = q_pad.shape[:2]
    npast = state_t.shape[2]
    n_heads = n_groups * n_rep
    has_sink = sinks is not None
    bb = math.gcd(rows_per_step, b)
    in_specs = [pl.BlockSpec((bb, nq, n_heads * HEAD_DIM), lambda i: (i, 0, 0)),
                pl.BlockSpec((bb, KV_ROW, npast), lambda i: (i, 0, 0)),
                pl.BlockSpec((bb, nq, KV_ROW), lambda i: (i, 0, 0))]
    args = [q_pad, state_t, new_kv]
    if has_sink:
        in_specs.insert(0, pl.BlockSpec(memory_space=pltpu.SMEM))
        args.insert(0, sinks)
    return pl.pallas_call(
        functools.partial(_banded_decode_kernel, n_groups=n_groups, n_rep=n_rep, window=window,
                          slopes=_alibi(n_heads), has_sink=has_sink, pos0=pos0),
        grid=(b // bb,),
        in_specs=in_specs,
        out_specs=pl.BlockSpec((bb, nq, n_heads * HEAD_DIM), lambda i: (i, 0, 0)),
        out_shape=jax.ShapeDtypeStruct((b, nq, n_heads * HEAD_DIM), F32),
        compiler_params=_cparams("parallel"),
        name="banded_decode",
    )(*args)


def _select_blocks_t(imp, cur, n_sel):
    nb = imp.shape[0]
    nrow = lax.broadcasted_iota(jnp.int32, imp.shape, 0)
    count = jnp.zeros(imp.shape, F32)
    for i in range(nb):
        ci = imp[i:i + 1, :]
        ahead = jnp.where(ci > imp, 1.0, jnp.where(ci == imp, jnp.where(nrow > i, 1.0, 0.0), 0.0))
        count = count + jnp.where(cur > i, ahead, 0.0)
    return jnp.where(nrow < cur, jnp.where(count < n_sel - 1, 1.0, 0.0), jnp.where(nrow == cur, 1.0, 0.0))


def _nsa_prompt_kernel(q_ref, gd_ref, kc_ref, kct_ref, x_ref, xt_ref, ow_ref, o_ref, m_scr, acc_scr,
                       *, n_sel, kv_block):
    qt = pl.program_id(1)
    n_rep = NSA_GROUP
    nrow = n_rep * TQ
    slopes = _alibi(NSA_HEADS)
    nb = kc_ref.shape[1]
    kcb = kc_ref[0]
    gates = _sigmoid(gd_ref[0])
    n_ct = NSA_CHUNK_TILES
    tk = n_ct * TQ
    n_seq_tiles = x_ref.shape[1] // TQ

    def chunk_consts(n_t):
        k = n_t * TQ
        ki = lax.broadcasted_iota(jnp.int32, (k, NSA_KV_HEADS * nrow), 0)
        qi = lax.broadcasted_iota(jnp.int32, (k, NSA_KV_HEADS * nrow), 1) % TQ
        key_in_chunk = lax.broadcasted_iota(jnp.int32, (k, LANE), 0)
        return dict(diff=qi - ki,
                    lane=lax.broadcasted_iota(jnp.int32, (k, LANE), 1), blk=key_in_chunk // CMP_BLOCK,
                    lo=(key_in_chunk % TQ).astype(F32), hi=(key_in_chunk // TQ).astype(F32),
                    ones=jnp.ones((2 * SUBLANE, k), BF16))

    consts = {n_t: chunk_consts(n_t) for n_t in (n_ct, n_ct // 2)}
    tpos = qt * TQ + lax.broadcasted_iota(jnp.int32, (nb, nrow), 1) % TQ
    blk = lax.broadcasted_iota(jnp.int32, (nb, nrow), 0)
    dist_c = tpos - ((blk + 1) * CMP_BLOCK - 1)
    valid_c = dist_c >= 0
    cur = (qt * TQ + lax.broadcasted_iota(jnp.int32, (nb, TQ), 1)) // CMP_BLOCK
    assert nb + 1 < LANE and all(math.frexp(sl)[0] == 0.5 for sl in slopes)
    lane_q = lax.broadcasted_iota(jnp.int32, (TQ, LANE), 1)
    groups = []
    for g in range(NSA_KV_HEADS):
        heads = [g * n_rep + r for r in range(n_rep)]
        qrows = _q_rows(q_ref, heads, n_rep)
        slope_row = _row([slopes[h] for h in heads], TQ)
        vrow = (NSA_KV_HEADS + g) * HEAD_DIM
        s_c = jnp.where(valid_c, _dot_nt(kcb, qrows) - slope_row * dist_c.astype(F32), NEG)
        e_c = jnp.where(valid_c, jnp.exp(s_c - jnp.max(s_c, axis=0, keepdims=True)), 0.0)
        p_c = e_c / jnp.maximum(jnp.sum(e_c, axis=0, keepdims=True), 1e-30)
        ocmp_t = _dot(kct_ref[0, vrow:vrow + HEAD_DIM, :], p_c.astype(BF16))
        imp = p_c[:, 0:TQ]
        for r in range(1, n_rep):
            imp = imp + p_c[:, r * TQ:(r + 1) * TQ]
        taken = _select_blocks_t(imp, cur, n_sel)
        taken_q = jnp.concatenate([taken, jnp.zeros((LANE - nb, TQ), F32)], axis=0).T
        block_mask = jnp.where(taken_q > 0.5, 0.0, MASK)
        q_extra = jnp.concatenate([jnp.where(lane_q == nb, slopes[h], jnp.where(lane_q == nb + 1, slopes[h] * TQ,
                                                                                 block_mask)) for h in heads], axis=0)
        q_aug = jnp.concatenate([qrows, q_extra.astype(BF16)], axis=1)
        acc_scr[g] = jnp.zeros(acc_scr.shape[1:], F32)
        groups.append((heads, q_aug, slope_row, kv_block * KV_ROW + vrow, ocmp_t))
    q_all = jnp.concatenate([grp[1] for grp in groups], axis=0)
    slope_all = jnp.concatenate([grp[2] for grp in groups], axis=1)
    m_scr[...] = jnp.full(m_scr.shape, NEG, F32)

    def chunk(kt0, causal, n_t=n_ct):
        kts = [jnp.minimum(kt0 + i, n_seq_tiles - 1) for i in range(n_t)]
        c = consts[n_t]
        x_extra = jnp.where(c["lane"] == nb, c["lo"], jnp.where(c["lane"] == nb + 1, c["hi"],
                            jnp.where(c["lane"] == c["blk"] + kt0 * (TQ // CMP_BLOCK), 1.0, 0.0))).astype(BF16)
        x = jnp.concatenate([x_ref[0, pl.ds(pl.multiple_of(kt * TQ, TQ), TQ), :] for kt in kts], axis=0)
        x_aug = jnp.concatenate([x, x_extra], axis=1)
        off = (qt - kt0) * TQ
        s = _dot_nt(x_aug, q_all)
        if causal:
            s = jnp.where(c["diff"] + off >= 0, s, MASK)
        shift = slope_all * off.astype(F32)
        m_old = m_scr[...]
        m_new = jnp.maximum(m_old, jnp.max(s, axis=0, keepdims=True) - shift)
        p = jnp.exp(s - (m_new + shift)).astype(BF16)
        alpha = jnp.exp(m_old - m_new)
        m_scr[...] = m_new
        for g, (_, _, _, v0, _) in enumerate(groups):
            cols = slice(g * nrow, (g + 1) * nrow)
            v_t = jnp.concatenate([xt_ref[0, kt, v0:v0 + HEAD_DIM, :] for kt in kts], axis=1)
            acc_scr[g] = alpha[:, cols] * acc_scr[g] + _dot(jnp.concatenate([v_t, c["ones"]], axis=0), p[:, cols])

    n_full = qt // n_ct

    def body(j, carry):
        chunk(n_ct * j, False)
        return carry

    lax.fori_loop(0, n_full, body, 0)
    tail = qt - n_ct * n_full + 1

    @pl.when(tail <= n_ct // 2)
    def _():
        chunk(n_ct * n_full, True, n_ct // 2)

    @pl.when(tail > n_ct // 2)
    def _():
        chunk(n_ct * n_full, True, n_ct)
    for g, (heads, _, _, _, ocmp_t) in enumerate(groups):
        oslc_t = acc_scr[g, 0:HEAD_DIM, :] / jnp.maximum(acc_scr[g, HEAD_DIM:HEAD_DIM + 1, :], 1e-30)
        for pair in range(n_rep // 2):
            h0 = heads[2 * pair]
            hh = slice(h0 * HEAD_DIM, (h0 + 2) * HEAD_DIM)
            gate = lambda j: _pair_lanes(gates[:, 3 * h0 + j:3 * h0 + j + 1], gates[:, 3 * h0 + 3 + j:3 * h0 + 4 + j])
            o_ref[0, :, hh] = (gate(0) * _heads_to_rows(ocmp_t, n_rep, pair)
                               + gate(1) * _heads_to_rows(oslc_t, n_rep, pair)
                               + gate(2) * ow_ref[0, :, hh])


def nsa_prompt(q_pad, gd, kc, x_rows, x_pages, kv_block, o_win):
    b, seq = q_pad.shape[:2]
    nb = kc.shape[1]
    nrow = NSA_GROUP * TQ
    kct = jnp.swapaxes(kc, 1, 2)
    return pl.pallas_call(
        functools.partial(_nsa_prompt_kernel, n_sel=min(SEL_TOPK, nb), kv_block=kv_block),
        grid=(b, seq // TQ),
        in_specs=[pl.BlockSpec((1, TQ, NSA_WIDTH), lambda i, j: (i, j, 0)),
                  pl.BlockSpec((1, TQ, LANE), lambda i, j: (i, j, 0)),
                  pl.BlockSpec((1, nb, KV_ROW), lambda i, j: (i, 0, 0)),
                  pl.BlockSpec((1, KV_ROW, nb), lambda i, j: (i, 0, 0)),
                  pl.BlockSpec((1, seq, KV_ROW), lambda i, j: (i, 0, kv_block)),
                  pl.BlockSpec((1,) + x_pages.shape[1:], lambda i, j: (i, 0, 0, 0)),
                  pl.BlockSpec((1, TQ, NSA_WIDTH), lambda i, j: (i, j, 0))],
        out_specs=pl.BlockSpec((1, TQ, NSA_WIDTH), lambda i, j: (i, j, 0)),
        out_shape=jax.ShapeDtypeStruct((b, seq, NSA_WIDTH), F32),
        scratch_shapes=[pltpu.VMEM((1, NSA_KV_HEADS * nrow), F32),
                        pltpu.VMEM((NSA_KV_HEADS, HEAD_DIM + 2 * SUBLANE, nrow), F32)],
        compiler_params=_cparams("parallel", "parallel"),
        name="nsa_prompt",
    )(q_pad, gd, kc, kct, x_rows, x_pages, o_win)


def _select_blocks(imp, n_sel):
    nb = imp.shape[1]
    ncol = lax.broadcasted_iota(jnp.int32, imp.shape, 1)
    count = jnp.zeros(imp.shape, F32)
    for i in range(nb):
        ci = imp[:, i:i + 1]
        count = count + jnp.where(ci > imp, 1.0, jnp.where(ci == imp, jnp.where(ncol > i, 1.0, 0.0), 0.0))
    return jnp.where(count < n_sel - 1, 1.0, 0.0)


def _nsa_decode_kernel(pt_ref, q_ref, gd_ref, kc_ref, new_ref, ow_ref, ex_ref, *rest, pages_per_step, pos0, n_sel):
    page_refs = rest[:pages_per_step]
    o_ref, m_scr, l_scr, acc_scr, mask_scr, bias_scr, ocmp_scr, q_scr = rest[pages_per_step:]
    p = pl.program_id(1)
    n_rep = NSA_GROUP
    nq = q_ref.shape[1]
    ngrp = n_rep * nq
    nrow = NSA_KV_HEADS * ngrp
    nb = kc_ref.shape[1]
    page = page_refs[0].shape[1]
    nkey = pages_per_step * page
    slopes = _alibi(NSA_HEADS)
    slope_col = _col(slopes, nq)

    def per_query(a):
        return jnp.concatenate([a[g] for g in range(NSA_KV_HEADS) for _ in range(n_rep)], axis=0)

    @pl.when(p == 0)
    def _():
        q_scr[...] = jnp.concatenate([_q_rows(q_ref, [g * n_rep + r for r in range(n_rep)], n_rep)
                                      for g in range(NSA_KV_HEADS)], axis=0)

    qrows = q_scr[...]

    @pl.when(p == 0)
    def _():
        kcb = kc_ref[0]
        tpos = pos0 + jnp.concatenate([lax.broadcasted_iota(jnp.int32, (nq, nb), 0)] * (nrow // nq), axis=0)
        ncol = lax.broadcasted_iota(jnp.int32, (nrow, nb), 1)
        dist = tpos - ((ncol + 1) * CMP_BLOCK - 1)
        valid = dist >= 0
        s = jnp.where(valid, _dot_nt(qrows, kcb) - slope_col * dist.astype(F32), NEG)
        e = jnp.where(valid, jnp.exp(s - jnp.max(s, axis=-1, keepdims=True)), 0.0)
        pc = e / jnp.maximum(jnp.sum(e, axis=-1, keepdims=True), 1e-30)
        ocmp_scr[...] = _dot(pc.astype(BF16), kcb)
        not_taken = []
        for g in range(NSA_KV_HEADS):
            imp = pc[g * ngrp:g * ngrp + nq]
            for r in range(1, n_rep):
                imp = imp + pc[g * ngrp + r * nq:g * ngrp + (r + 1) * nq]
            not_taken.append(jnp.where(_select_blocks(imp, n_sel) > 0.5, 0.0, MASK))
        not_taken = jnp.concatenate(not_taken, axis=0).astype(BF16)
        for st in range(mask_scr.shape[0]):
            mask_scr[st] = _dot(not_taken, ex_ref[:, st * nkey:(st + 1) * nkey])
        qi = jnp.concatenate([lax.broadcasted_iota(jnp.int32, (nq, nkey), 0)] * (nrow // nq), axis=0)
        kj = lax.broadcasted_iota(jnp.int32, (nrow, nkey), 1)
        bias_scr[...] = slope_col * (pos0 + qi - kj).astype(F32)
        m_scr[...] = jnp.full_like(m_scr, NEG)
        l_scr[...] = jnp.zeros_like(l_scr)
        acc_scr[...] = jnp.zeros_like(acc_scr)

    mask = mask_scr[p]
    shift = slope_col * (p * nkey).astype(F32)
    per = pages_per_step // N_STREAMS
    for st in range(N_STREAMS):
        ks = slice(st * per * page, (st + 1) * per * page)
        x_t = jnp.concatenate([r[...] for r in page_refs[st * per:(st + 1) * per]], axis=1).astype(BF16)
        s = (_dot(qrows, x_t) - bias_scr[:, ks]
             + per_query([mask[g * nq:(g + 1) * nq, ks] for g in range(NSA_KV_HEADS)]))
        m_old = m_scr[st]
        m_new = jnp.maximum(m_old, jnp.max(s, axis=-1, keepdims=True) + shift)
        alpha = jnp.exp(m_old - m_new)
        pr = jnp.exp(s - (m_new - shift))
        l_scr[st] = alpha * l_scr[st] + jnp.sum(pr, axis=-1, keepdims=True)
        acc_scr[st] = alpha * acc_scr[st] + _dot_nt(pr.astype(BF16), x_t)
        m_scr[st] = m_new

    @pl.when(p == pl.num_programs(1) - 1)
    def _():
        xn = jnp.concatenate([new_ref[0], jnp.zeros((TQ - nq, KV_ROW), F32)], axis=0).astype(BF16)
        qi_n = jnp.concatenate([lax.broadcasted_iota(jnp.int32, (nq, TQ), 0)] * (nrow // nq), axis=0)
        kj_n = lax.broadcasted_iota(jnp.int32, (nrow, TQ), 1)
        dist_n = qi_n - kj_n
        valid_n = (dist_n >= 0) & (kj_n < nq)
        s_n = jnp.where(valid_n, _dot_nt(qrows, xn) - slope_col * dist_n.astype(F32), NEG)
        m_new = jnp.max(s_n, axis=-1, keepdims=True)
        for st in range(N_STREAMS):
            m_new = jnp.maximum(m_new, m_scr[st])
        p_n = jnp.where(valid_n, jnp.exp(s_n - m_new), 0.0)
        den = jnp.sum(p_n, axis=-1, keepdims=True)
        num = _dot(p_n.astype(BF16), xn)
        for st in range(N_STREAMS):
            alpha = jnp.exp(m_scr[st] - m_new)
            den = den + alpha * l_scr[st]
            num = num + alpha * acc_scr[st]
        o_slc = num / jnp.maximum(den, 1e-30)
        o_cmp = ocmp_scr[...]
        gates = _sigmoid(gd_ref[0])
        for h in range(NSA_HEADS):
            g = h // n_rep
            v0 = (NSA_KV_HEADS + g) * HEAD_DIM
            rr = slice(h * nq, (h + 1) * nq)
            hh = slice(h * HEAD_DIM, (h + 1) * HEAD_DIM)
            o_ref[0, :, hh] = (gates[:, 3 * h:3 * h + 1] * o_cmp[rr, v0:v0 + HEAD_DIM]
                               + gates[:, 3 * h + 1:3 * h + 2] * o_slc[rr, v0:v0 + HEAD_DIM]
                               + gates[:, 3 * h + 2:3 * h + 3] * ow_ref[0, :, hh])


def nsa_decode(page_table, q_pad, gd, kc, new_kv, o_win, pool_t, pos0, pages_per_step=64):
    b, nq = q_pad.shape[:2]
    nb = kc.shape[1]
    n_pages = page_table.shape[1]
    page = pool_t.shape[2]
    pps = math.gcd(pages_per_step, n_pages)
    assert pps % N_STREAMS == 0
    n_steps = n_pages // pps
    nkey = pps * page
    nrow = NSA_HEADS * nq
    expand = (jnp.arange(n_pages * page)[None, :] // CMP_BLOCK == jnp.arange(nb)[:, None]).astype(BF16)

    def page_spec(j):
        return pl.BlockSpec((None, KV_ROW, page), lambda i, p, pt: (pt[i, p * pps + j], 0, 0))

    grid_spec = pltpu.PrefetchScalarGridSpec(
        num_scalar_prefetch=1,
        grid=(b, n_steps),
        in_specs=[pl.BlockSpec((1, nq, NSA_WIDTH), lambda i, p, pt: (i, 0, 0)),
                  pl.BlockSpec((1, nq, LANE), lambda i, p, pt: (i, 0, 0)),
                  pl.BlockSpec((1, nb, KV_ROW), lambda i, p, pt: (i, 0, 0)),
                  pl.BlockSpec((1, nq, KV_ROW), lambda i, p, pt: (i, 0, 0)),
                  pl.BlockSpec((1, nq, NSA_WIDTH), lambda i, p, pt: (i, 0, 0)),
                  pl.BlockSpec(expand.shape, lambda i, p, pt: (0, 0))]
                 + [page_spec(j) for j in range(pps)],
        out_specs=pl.BlockSpec((1, nq, NSA_WIDTH), lambda i, p, pt: (i, 0, 0)),
        scratch_shapes=[pltpu.VMEM((N_STREAMS, nrow, 1), F32), pltpu.VMEM((N_STREAMS, nrow, 1), F32),
                        pltpu.VMEM((N_STREAMS, nrow, KV_ROW), F32),
                        pltpu.VMEM((n_steps, NSA_KV_HEADS * nq, nkey), F32),
                        pltpu.VMEM((nrow, nkey), F32),
                        pltpu.VMEM((nrow, KV_ROW), F32),
                        pltpu.VMEM((nrow, KV_ROW), BF16)])
    return pl.pallas_call(
        functools.partial(_nsa_decode_kernel, pages_per_step=pps, pos0=pos0, n_sel=min(SEL_TOPK, nb + 1)),
        grid_spec=grid_spec,
        out_shape=jax.ShapeDtypeStruct((b, nq, NSA_WIDTH), F32),
        compiler_params=_cparams("parallel", "arbitrary"),
        name="nsa_decode",
    )(page_table, q_pad, gd, kc, new_kv, o_win, expand, *([pool_t] * pps))


def _cumsum_rows(v):
    n = v.shape[0]
    ri = lax.broadcasted_iota(jnp.int32, v.shape, 0)
    sh = 1
    while sh < n:
        v = v + jnp.where(ri >= sh, pltpu.roll(v, sh, 0), 0.0)
        sh *= 2
    return v


def _ssd_kernel(z_ref, xbc_ref, gd_ref, tail_ref, s0_ref, cw_ref, cb_ref, dtb_ref, alog_ref, dsk_ref, ng_ref,
                y_ref, sfin_ref, xp_scr, st_scr, y_scr, *, n_valid):
    c = pl.program_id(1)
    q = xbc_ref.shape[1]

    @pl.when(c == 0)
    def _():
        xp_scr[0:SUBLANE, :] = tail_ref[0]
        st_scr[...] = s0_ref[0]

    xp_scr[SUBLANE:SUBLANE + q, :] = xbc_ref[0]
    acc = cb_ref[...]
    for k in range(SSM_CONV):
        lo = SUBLANE - (SSM_CONV - 1) + k
        acc = acc + xp_scr[lo:lo + q, :] * cw_ref[k:k + 1, :]
    nxt = xp_scr[q:q + SUBLANE, :]
    xp_scr[0:SUBLANE, :] = nxt
    act = acc * _sigmoid(acc)

    raw = gd_ref[0] + dtb_ref[...]
    dt = jnp.maximum(raw, 0.0) + jnp.log1p(jnp.exp(-jnp.abs(raw)))
    if n_valid < q:
        dt = jnp.where(lax.broadcasted_iota(jnp.int32, dt.shape, 0) < n_valid, dt, 0.0)
    acum = _cumsum_rows(dt * (-jnp.exp(alog_ref[...])))
    acum_t = acum.T
    ri = lax.broadcasted_iota(jnp.int32, (q, q), 0)
    ci = lax.broadcasted_iota(jnp.int32, (q, q), 1)
    causal = ri >= ci
    half = lax.broadcasted_iota(jnp.int32, (2 * SSM_HEAD_DIM, 1), 0) < SSM_HEAD_DIM
    hpg = SSM_HEADS // SSM_GROUPS
    for pair in range(SSM_HEADS // 2):
        grp = (2 * pair) // hpg
        lanes = slice(pair * LANE, (pair + 1) * LANE)
        bm = act[:, SSM_D_INNER + grp * SSM_STATE:SSM_D_INNER + (grp + 1) * SSM_STATE].astype(BF16)
        cm_lo = SSM_D_INNER + SSM_GROUPS * SSM_STATE + grp * SSM_STATE
        cm = act[:, cm_lo:cm_lo + SSM_STATE].astype(BF16)
        cb = _dot_nt(cm, bm)
        xs = act[:, lanes]
        cols = []
        for h in (2 * pair, 2 * pair + 1):
            ln = DT_LANE + h
            cols.append((acum[:, ln:ln + 1], acum_t[ln:ln + 1, :], acum[q - 1:q, ln:ln + 1], dt[:, ln:ln + 1]))
        xdt = xs * _pair_lanes(cols[0][3], cols[1][3])
        xdt_bf = xdt.astype(BF16)
        y_parts = []
        for a_col, a_row, _, _ in cols:
            seg = a_col - a_row
            lmat = jnp.where(causal, jnp.exp(jnp.where(causal, seg, 0.0)), 0.0)
            y_parts.append(_dot((cb * lmat).astype(BF16), xdt_bf))
        lane = lax.broadcasted_iota(jnp.int32, (q, LANE), 1)
        y_diag = jnp.where(lane < SSM_HEAD_DIM, y_parts[0], y_parts[1])
        st = st_scr[pair * LANE:(pair + 1) * LANE, :]
        y_off = _dot_nt(cm, st.astype(BF16)) * _pair_lanes(jnp.exp(cols[0][0]), jnp.exp(cols[1][0]))
        dec_end = _pair_lanes(jnp.exp(cols[0][2] - cols[0][0]), jnp.exp(cols[1][2] - cols[1][0]))
        cs = _dot((xdt * dec_end).T.astype(BF16), bm)
        st_scr[pair * LANE:(pair + 1) * LANE, :] = st * jnp.where(half, jnp.exp(cols[0][2]), jnp.exp(cols[1][2])) + cs
        dsk = _pair_lanes(jnp.zeros((q, 1), F32) + dsk_ref[:, 2 * pair:2 * pair + 1],
                          jnp.zeros((q, 1), F32) + dsk_ref[:, 2 * pair + 1:2 * pair + 2])
        y_scr[:, lanes] = y_diag + y_off + dsk * xs

    z = z_ref[0]
    y_ref[0] = _rms(y_scr[...] * (z * _sigmoid(z)), ng_ref[...])

    @pl.when(c == pl.num_programs(1) - 1)
    def _():
        sfin_ref[0] = st_scr[...]


def ssd_mixer(z, xbc, gd, tail8, s0, conv_w, conv_b, dt_bias, a_log, d_skip, norm_g, n_valid):
    b, seq = z.shape[:2]
    nst = SSM_HEADS * SSM_HEAD_DIM

    def lane_piece(v):
        return jnp.zeros((1, LANE), F32).at[0, DT_LANE:DT_LANE + SSM_HEADS].set(v)

    alog_piece = jnp.full((1, LANE), -100.0, F32).at[0, DT_LANE:DT_LANE + SSM_HEADS].set(a_log)
    full2 = lambda shape: pl.BlockSpec(shape, lambda i, c: (0, 0))
    return pl.pallas_call(
        functools.partial(_ssd_kernel, n_valid=n_valid),
        grid=(b, seq // TQ),
        in_specs=[pl.BlockSpec((1, TQ, SSM_D_INNER), lambda i, c: (i, c, 0)),
                  pl.BlockSpec((1, TQ, SSM_CONV_DIM), lambda i, c: (i, c, 0)),
                  pl.BlockSpec((1, TQ, LANE), lambda i, c: (i, c, 0)),
                  pl.BlockSpec((1, SUBLANE, SSM_CONV_DIM), lambda i, c: (i, 0, 0)),
                  pl.BlockSpec((1, nst, SSM_STATE), lambda i, c: (i, 0, 0)),
                  full2((SSM_CONV, SSM_CONV_DIM)), full2((1, SSM_CONV_DIM)), full2((1, LANE)),
                  full2((1, LANE)), full2((1, SSM_HEADS)), full2((1, SSM_D_INNER))],
        out_specs=[pl.BlockSpec((1, TQ, SSM_D_INNER), lambda i, c: (i, c, 0)),
                   pl.BlockSpec((1, nst, SSM_STATE), lambda i, c: (i, 0, 0))],
        out_shape=[jax.ShapeDtypeStruct((b, seq, SSM_D_INNER), F32),
                   jax.ShapeDtypeStruct((b, nst, SSM_STATE), F32)],
        scratch_shapes=[pltpu.VMEM((SUBLANE + TQ, SSM_CONV_DIM), F32),
                        pltpu.VMEM((nst, SSM_STATE), F32),
                        pltpu.VMEM((TQ, SSM_D_INNER), F32)],
        compiler_params=_cparams("parallel", "arbitrary"),
        name="ssd_mixer",
    )(z, xbc, gd, tail8, s0, conv_w, conv_b.reshape(1, -1), lane_piece(dt_bias), alog_piece,
      d_skip.reshape(1, -1), norm_g.reshape(1, -1))


def _channel_major(a6):
    b, rows = a6.shape[:2]
    return jnp.transpose(a6, (0, 2, 3, 4, 1)).reshape(b, KV_ROW, rows)


def _token_major(a_t):
    b, _, rows = a_t.shape
    return jnp.transpose(a_t.reshape(b, 2, NSA_KV_HEADS, HEAD_DIM, rows), (0, 4, 1, 2, 3))


def _even_weights(w_in):
    d = w_in.shape[0]
    o = np.cumsum([0, NSA_WIDTH, KV_ROW, KV_ROW, KV_ROW, 3 * NSA_HEADS, SSM_D_INNER, SSM_CONV_DIM, SSM_HEADS])
    q, kvc, kvs, kvw, gt, z, xbc, dtr = (w_in[:, o[i]:o[i + 1]] for i in range(8))
    gd = jnp.zeros((d, LANE), F32).at[:, :3 * NSA_HEADS].set(gt).at[:, DT_LANE:DT_LANE + SSM_HEADS].set(dtr)
    w = jnp.concatenate([q * SCALE, kvc, kvs, kvw, z, xbc, gd], axis=1).astype(BF16)
    qw = NSA_WIDTH
    off = {"q": (0, qw), "kvc": (qw, KV_ROW), "kvs": (qw + KV_ROW, KV_ROW), "kvw": (qw + 2 * KV_ROW, KV_ROW),
           "kv3": (qw, 3 * KV_ROW), "z": (qw + 3 * KV_ROW, SSM_D_INNER),
           "xbc": (qw + 3 * KV_ROW + SSM_D_INNER, SSM_CONV_DIM),
           "gd": (qw + 3 * KV_ROW + SSM_D_INNER + SSM_CONV_DIM, LANE)}
    return w, off


def _even_layer(x, past, p, cache):
    (norm_g, w_in, w_out, cw, ssm_p) = p
    b, seq, d = x.shape
    m = b * seq
    decode = cache is not None
    w, off = _even_weights(w_in)
    conv_w, conv_b, dt_bias, a_log, d_skip, ssm_norm = ssm_p
    if not decode:
        names = [("row", "q", F32), ("chan", "kvc", F32), ("chan", "kvs", F32), ("chan", "kvw", F32),
                 ("page", "kvc", F32), ("row", "kv3", BF16), ("page", "kv3", BF16),
                 ("row", "z", F32), ("row", "xbc", F32), ("row", "gd", F32)]
        q, kvc_t, kvs_t, kvw_t, kvc_pages, x_rows, x_pages, z, xbc, gd = norm_proj(
            x, norm_g, w, [(k,) + off[n] for k, n, _ in names], [dt for _, _, dt in names])
        q = q.reshape(b, seq, -1)
        gd3 = gd.reshape(b, seq, LANE)
        x_rows = x_rows.reshape(b, seq, 3 * KV_ROW)
        x_pages = x_pages.reshape(b, seq // TQ, 3 * KV_ROW, TQ)
        kc = compress(kvc_pages.reshape(-1, TQ), cw).reshape(b, seq // CMP_BLOCK, KV_ROW)
        o_win = banded_attention(q, x_rows, x_pages, 2, NSA_KV_HEADS, NSA_GROUP, NSA_WINDOW)
        o_nsa = nsa_prompt(q, gd3, kc, x_rows, x_pages, 1, o_win)
        tail8 = jnp.zeros((b, SUBLANE, SSM_CONV_DIM), F32)
        s0 = jnp.zeros((b, SSM_D_INNER, SSM_STATE), F32)
        xbc3 = xbc.reshape(b, seq, -1)
        o_ssm, s_fin = ssd_mixer(z.reshape(b, seq, -1), xbc3, gd3, tail8, s0,
                                 conv_w, conv_b, dt_bias, a_log, d_skip, ssm_norm, TQ)
        conv_new = xbc3[:, seq - (SSM_CONV - 1):]
        kv_outs = (_token_major(kvc_t), _token_major(kvs_t), _token_major(kvw_t[:, :, seq - min(NSA_WINDOW, seq):]))
    else:
        cmp_pool_t, slc_pool_t, page_table, win_state_t, conv0, ssm0 = cache
        n_pool, _, page = cmp_pool_t.shape
        assert page == TQ and seq < CMP_BLOCK and TQ % seq == 0
        names = [("row", "q", F32), ("row", "kvc", F32), ("row", "kvs", F32), ("row", "kvw", F32),
                 ("row", "z", F32), ("row", "xbc", F32), ("row", "gd", F32)]
        q, kvc, kvs, kvw, z, xbc, gd = norm_proj(
            x, norm_g, w, [(k,) + off[n] for k, n, _ in names], [dt for _, _, dt in names])
        q = q.reshape(b, seq, -1)
        gd3 = gd.reshape(b, seq, LANE)
        bpp = page // CMP_BLOCK
        kc_pool = compress(cmp_pool_t.reshape(n_pool * KV_ROW, page), cw)
        kc = kc_pool.reshape(n_pool, bpp, KV_ROW)[page_table].reshape(b, past // CMP_BLOCK, KV_ROW)
        kvw3 = kvw.reshape(b, seq, KV_ROW)
        o_win = banded_decode(q, win_state_t, kvw3, NSA_KV_HEADS, NSA_GROUP, NSA_WINDOW, past)
        o_nsa = nsa_decode(page_table, q, gd3, kc, kvs.reshape(b, seq, KV_ROW), o_win, slc_pool_t, past)
        padr = lambda a: jnp.pad(a.reshape(b, seq, -1), ((0, 0), (0, TQ - seq), (0, 0)))
        tail8 = jnp.pad(conv0, ((0, 0), (SUBLANE - (SSM_CONV - 1), 0), (0, 0)))
        o_ssm, s_fin = ssd_mixer(padr(z), padr(xbc), padr(gd), tail8, ssm0.reshape(b, SSM_D_INNER, SSM_STATE),
                                 conv_w, conv_b, dt_bias, a_log, d_skip, ssm_norm, seq)
        o_ssm = o_ssm[:, :seq]
        conv_new = jnp.concatenate([conv0, xbc.reshape(b, seq, -1)], axis=1)[:, -(SSM_CONV - 1):]
        npast = win_state_t.shape[2]
        win_out = jnp.concatenate([win_state_t, jnp.swapaxes(kvw3, 1, 2)], axis=2)[:, :, -npast:]
        kv6 = (b, seq, 2, NSA_KV_HEADS, HEAD_DIM)
        kv_outs = (kvc.reshape(kv6), kvs.reshape(kv6), _token_major(win_out))
    x = proj_res([o_nsa.reshape(m, -1), o_ssm.reshape(m, -1)],
                 [w_out[:NSA_WIDTH].astype(BF16), w_out[NSA_WIDTH:].astype(BF16)], x.reshape(m, d))
    outs = kv_outs + (s_fin.reshape(b, SSM_HEADS, SSM_HEAD_DIM, SSM_STATE), conv_new)
    return x.reshape(b, seq, d), outs


def _odd_layer(x, past, p, swa_state_t):
    norm_g, w_in, w_out, sinks = p
    b, seq, d = x.shape
    m = b * seq
    decode = swa_state_t is not None
    w = jnp.concatenate([w_in[:, :SWA_WIDTH] * SCALE, w_in[:, SWA_WIDTH:]], axis=1).astype(BF16)
    qw = SWA_WIDTH
    if not decode:
        q, kv_t, x_rows, x_pages = norm_proj(
            x, norm_g, w, [("row", 0, qw), ("chan", qw, KV_ROW), ("row", qw, KV_ROW), ("page", qw, KV_ROW)],
            [F32, F32, BF16, BF16])
        o = banded_attention(q.reshape(b, seq, -1), x_rows.reshape(b, seq, KV_ROW),
                             x_pages.reshape(b, seq // TQ, KV_ROW, TQ), 0, SWA_KV_HEADS, SWA_GROUP, SWA_WINDOW, sinks)
        kv_out = _token_major(kv_t[:, :, seq - min(SWA_WINDOW, seq):])
    else:
        q, kv = norm_proj(x, norm_g, w, [("row", 0, qw), ("row", qw, KV_ROW)], [F32, F32])
        kv3 = kv.reshape(b, seq, KV_ROW)
        o = banded_decode(q.reshape(b, seq, -1), swa_state_t, kv3, SWA_KV_HEADS, SWA_GROUP, SWA_WINDOW, past, sinks)
        npast = swa_state_t.shape[2]
        kv_out = _token_major(jnp.concatenate([swa_state_t, jnp.swapaxes(kv3, 1, 2)], axis=2)[:, :, -npast:])
    x = proj_res([o.reshape(m, -1)], [w_out.astype(BF16)], x.reshape(m, d))
    return x.reshape(b, seq, d), kv_out


def kernel(x_prompt, x_sample, cache_nsa_cmp_kv, cache_nsa_slc_kv, state_nsa_win_kv, state_ssm, state_ssm_conv, state_swa_kv, page_table, norm_mix, norm_ffn, norm_final, w_in_even, w_out_even, cmp_pe_k, cmp_w1_k, cmp_w2_k, cmp_pe_v, cmp_w1_v, cmp_w2_v, ssm_conv_w, ssm_conv_b, ssm_dt_bias, ssm_a_log, ssm_d, ssm_norm, w_in_odd, w_out_odd, swa_sinks, w_gate_up, w_down):
    depth = norm_mix.shape[0]
    page = cache_nsa_cmp_kv.shape[2]
    past = page_table.shape[1] * page
    xp, xs = x_prompt, x_sample
    outs_p = [[] for _ in range(6)]
    outs_s = [[] for _ in range(6)]
    for layer in range(depth):
        if layer % 2 == 0:
            e = layer // 2
            cw = _compress_weights(cmp_pe_k[e], cmp_w1_k[e], cmp_w2_k[e], cmp_pe_v[e], cmp_w1_v[e], cmp_w2_v[e])
            ssm_p = (ssm_conv_w[e], ssm_conv_b[e], ssm_dt_bias[e], ssm_a_log[e], ssm_d[e], ssm_norm[e])
            p = (norm_mix[layer], w_in_even[e], w_out_even[e], cw, ssm_p)
            xp, o = _even_layer(xp, 0, p, None)
            for lst, v in zip(outs_p[:5], o):
                lst.append(v)
            cache = (_channel_major(cache_nsa_cmp_kv[e]), _channel_major(cache_nsa_slc_kv[e]), page_table,
                     _channel_major(state_nsa_win_kv[e]), state_ssm_conv[e], state_ssm[e])
            xs, o = _even_layer(xs, past, p, cache)
            for lst, v in zip(outs_s[:5], o):
                lst.append(v)
        else:
            o_idx = layer // 2
            p = (norm_mix[layer], w_in_odd[o_idx], w_out_odd[o_idx], swa_sinks[o_idx])
            xp, kw = _odd_layer(xp, 0, p, None)
            outs_p[5].append(kw)
            xs, kw = _odd_layer(xs, past, p, _channel_major(state_swa_kv[o_idx]))
            outs_s[5].append(kw)
        gf = norm_final if layer == depth - 1 else None
        wgu, wd = w_gate_up[layer].astype(BF16), w_down[layer].astype(BF16)
        xp = ffn(xp.reshape(-1, xp.shape[-1]), norm_ffn[layer], wgu, wd, gf).reshape(xp.shape)
        xs = ffn(xs.reshape(-1, xs.shape[-1]), norm_ffn[layer], wgu, wd, gf).reshape(xs.shape)
    return (xp, xs) + tuple(jnp.stack(v) for v in outs_p) + tuple(jnp.stack(v) for v in outs_s)
```
